```python
import jax, jax.numpy as jnp
from jax import lax
import numpy as np

D_MODEL = 2048
BATCH = 1
SEQ = 16384
DEPTH = 2

N_EVEN = (DEPTH + 1) // 2
N_ODD = DEPTH // 2
NORM_EPS = 1e-6
NEG_INF = -1e30
ROPE_THETA = 10000.0
D_FF = 4 * D_MODEL

NSA_HEADS = 8
NSA_KV_HEADS = 2
NSA_GROUP = NSA_HEADS // NSA_KV_HEADS
NSA_DH = 128
CMP_LEN = 32
CMP_STRIDE = 16
SLC_LEN = 64
SLC_TOPN = 16
WIN = 512
Q_BLOCK = 128
SLC_FORCE = 1e4
NSA_W = NSA_HEADS * NSA_DH
NSA_KV_W = NSA_KV_HEADS * NSA_DH
NSA_COLS = NSA_W + 6 * NSA_KV_W + 3 * NSA_HEADS

RW_HEADS = 16
RW_DH = 64
RW_W = RW_HEADS * RW_DH
RW_DECAY_LORA = 96
RW_A_LORA = 96
RW_G_LORA = 256
RW_LNX_EPS = 64e-5
RW_COLS = 3 * RW_W + RW_DECAY_LORA + RW_A_LORA + RW_G_LORA

LRU_W = 1024
LRU_BLOCKS = 8
LRU_BW = LRU_W // LRU_BLOCKS
CONV_W = 4
LRU_C = 8.0

HG_HEADS = 8
HG_DK = 128
HG_DV = 128
HG_KW = HG_HEADS * HG_DK
HG_VW = HG_HEADS * HG_DV
HG_CHUNK = 64

EVEN_COLS = NSA_COLS + RW_COLS
ODD_COLS = 2 * LRU_W + 2 * HG_KW + 2 * HG_VW
MIX_EVEN = NSA_W + RW_W
MIX_ODD = LRU_W + HG_VW

kernel_name = 'hybrid_nsa_rwkv7_rglru_hgrn2_trunk'


def _split(h, sizes):
    cuts = [int(c) for c in np.cumsum(sizes)[:-1]]
    return jnp.split(h, cuts, axis=-1)


def _rms_norm(x, g, eps=NORM_EPS):
    xf = x.astype(jnp.float32)
    y = xf * lax.rsqrt(jnp.mean(xf * xf, axis=-1, keepdims=True) + eps)
    return (y * g.astype(jnp.float32)).astype(x.dtype)


def _rope_tables(pos):
    inv = ROPE_THETA ** (-(jnp.arange(0, NSA_DH, 2, dtype=jnp.float32) / NSA_DH))
    ang = pos[:, None] * inv[None, :]
    return jnp.cos(ang), jnp.sin(ang)


def _apply_rope(x, cos, sin):
    xf = x.astype(jnp.float32)
    half = xf.shape[-1] // 2
    x1, x2 = xf[..., :half], xf[..., half:]
    c, s = cos[None, :, None, :], sin[None, :, None, :]
    return jnp.concatenate([x1 * c - x2 * s, x1 * s + x2 * c], axis=-1).astype(x.dtype)


def _masked_softmax(s, mask):
    p = jax.nn.softmax(jnp.where(mask, s, NEG_INF), axis=-1)
    return p * mask


def _compress(t, w, pe):
    B, S, G, dh = t.shape
    chunks = t.reshape(B, S // CMP_STRIDE, CMP_STRIDE, G, dh)
    blocks = jnp.concatenate([chunks[:, :-1], chunks[:, 1:]], axis=2)
    blocks = blocks + pe[None, None, :, None, :]
    return jnp.einsum('bnlgd,lde->bnge', blocks, w)


def _nsa(u, qk_gain, cmp_w, cmp_pe, cos, sin):
    B, S, _ = u.shape
    f32 = jnp.float32
    q, kc, vc, ks, vs, kw, vw, gate = _split(u, [NSA_W] + [NSA_KV_W] * 6 + [3 * NSA_HEADS])
    q = q.reshape(B, S, NSA_HEADS, NSA_DH)
    kc, vc, ks, vs, kw, vw = (t.reshape(B, S, NSA_KV_HEADS, NSA_DH) for t in (kc, vc, ks, vs, kw, vw))
    n_cmp = (S - CMP_LEN) // CMP_STRIDE + 1
    n_slc = S // SLC_LEN
    n_qb = S // Q_BLOCK
    top_n = min(SLC_TOPN, n_slc)
    scale = NSA_DH ** -0.5

    q = _apply_rope(_rms_norm(q, qk_gain[0]), cos, sin)
    ks = _apply_rope(_rms_norm(ks, qk_gain[2]), cos, sin)
    kw = _apply_rope(_rms_norm(kw, qk_gain[3]), cos, sin)
    kcmp = _compress(kc, cmp_w[0], cmp_pe[0])
    vcmp = _compress(vc, cmp_w[1], cmp_pe[1])
    cmp_end = jnp.arange(n_cmp) * CMP_STRIDE + CMP_LEN - 1
    ccos, csin = _rope_tables(cmp_end.astype(f32))
    kcmp = _apply_rope(_rms_norm(kcmp, qk_gain[1]), ccos, csin)

    c0 = jnp.arange(n_cmp)[:, None] * CMP_STRIDE
    s0 = jnp.arange(n_slc)[None, :] * SLC_LEN
    overlap = jnp.clip(jnp.minimum(c0 + CMP_LEN, s0 + SLC_LEN) - jnp.maximum(c0, s0), 0, None).astype(f32) / CMP_LEN

    ks_blk = ks.reshape(B, n_slc, SLC_LEN, NSA_KV_HEADS, NSA_DH).transpose(0, 3, 1, 2, 4)
    vs_blk = vs.reshape(B, n_slc, SLC_LEN, NSA_KV_HEADS, NSA_DH).transpose(0, 3, 1, 2, 4)
    kw_pad = jnp.pad(kw, ((0, 0), (WIN, 0), (0, 0), (0, 0)))
    vw_pad = jnp.pad(vw, ((0, 0), (WIN, 0), (0, 0), (0, 0)))
    gather = jax.vmap(jax.vmap(lambda blk, idx: blk[idx]))

    q_blocks = q.reshape(B, n_qb, Q_BLOCK, NSA_KV_HEADS, NSA_GROUP, NSA_DH).transpose(1, 0, 2, 3, 4, 5)
    g_all = jax.nn.sigmoid(gate.astype(f32)).reshape(B, n_qb, Q_BLOCK, 3, NSA_KV_HEADS, NSA_GROUP)
    g_blocks = g_all.transpose(1, 0, 2, 3, 4, 5)
    slc_ids = jnp.arange(n_slc)

    def block_fn(args):
        qb, gb, blk = args
        t = blk * Q_BLOCK + jnp.arange(Q_BLOCK)
        s = jnp.einsum('bqgjd,bngd->bgjqn', qb, kcmp).astype(f32) * scale
        p = _masked_softmax(s, cmp_end[None, :] <= t[:, None])
        o_c = jnp.einsum('bgjqn,bngd->bqgjd', p.astype(vcmp.dtype), vcmp)
        imp = jnp.einsum('bgjqn,nm->bgqm', p, overlap)
        cur = t // SLC_LEN
        forced = (slc_ids[None, :] == 0) | (slc_ids[None, :] == cur[:, None]) | (slc_ids[None, :] == cur[:, None] - 1)
        valid = slc_ids[None, :] <= cur[:, None]
        score = jnp.where(forced, SLC_FORCE, jnp.where(valid, imp, -1.0))
        _, idx = lax.top_k(score, top_n)
        kg = gather(ks_blk, idx).reshape(B, NSA_KV_HEADS, Q_BLOCK, top_n * SLC_LEN, NSA_DH)
        vg = gather(vs_blk, idx).reshape(B, NSA_KV_HEADS, Q_BLOCK, top_n * SLC_LEN, NSA_DH)
        kpos = (idx[..., None] * SLC_LEN + jnp.arange(SLC_LEN)).reshape(B, NSA_KV_HEADS, Q_BLOCK, top_n * SLC_LEN)
        s = jnp.einsum('bqgjd,bgqkd->bgjqk', qb, kg).astype(f32) * scale
        p = _masked_softmax(s, (kpos <= t[None, None, :, None])[:, :, None])
        o_s = jnp.einsum('bgjqk,bgqkd->bqgjd', p.astype(vg.dtype), vg)
        kwb = lax.dynamic_slice_in_dim(kw_pad, blk * Q_BLOCK, WIN + Q_BLOCK, axis=1)
        vwb = lax.dynamic_slice_in_dim(vw_pad, blk * Q_BLOCK, WIN + Q_BLOCK, axis=1)
        wpos = blk * Q_BLOCK - WIN + jnp.arange(WIN + Q_BLOCK)
        d = t[:, None] - wpos[None, :]
        mw = (d >= 0) & (d < WIN) & (wpos[None, :] >= 0)
        s = jnp.einsum('bqgjd,bkgd->bgjqk', qb, kwb).astype(f32) * scale
        p = _masked_softmax(s, mw)
        o_w = jnp.einsum('bgjqk,bkgd->bqgjd', p.astype(vwb.dtype), vwb)
        o = (gb[:, :, 0, :, :, None] * o_c.astype(f32) + gb[:, :, 1, :, :, None] * o_s.astype(f32)
             + gb[:, :, 2, :, :, None] * o_w.astype(f32))
        return o.astype(qb.dtype)

    o = lax.map(block_fn, (q_blocks, g_blocks, jnp.arange(n_qb)))
    return o.transpose(1, 0, 2, 3, 4, 5).reshape(B, S, NSA_W)


def _rwkv7_scan(r, w, k, v, a, b):
    B, S, H, N = r.shape

    def step(state, inp):
        r_t, w_t, k_t, v_t, a_t, b_t = inp
        sa = jnp.einsum('bhij,bhj->bhi', state, a_t)
        state = state * w_t[:, :, None, :] + sa[..., None] * b_t[:, :, None, :] + v_t[..., None] * k_t[:, :, None, :]
        return state, jnp.einsum('bhij,bhj->bhi', state, r_t)

    xs = tuple(jnp.moveaxis(t, 1, 0) for t in (r, w, k, v, a, b))
    _, y = lax.scan(step, jnp.zeros((B, H, N, N), jnp.float32), xs)
    return jnp.moveaxis(y, 0, 1)


def _rwkv7(u, mu, w0, w2, a0, a2, g2, k_k, k_a, r_k, lnx_w, lnx_b):
    B, S, _ = u.shape
    f32 = jnp.float32
    u_prev = jnp.pad(u, ((0, 0), (1, 0), (0, 0)))[:, :-1]
    u = u + (u_prev - u) * mu
    r, k, v, wl, al, gl = _split(u, [RW_W, RW_W, RW_W, RW_DECAY_LORA, RW_A_LORA, RW_G_LORA])
    w = -jax.nn.softplus(-(w0 + jnp.tanh(wl) @ w2).astype(f32)) - 0.5
    decay = jnp.exp(-jnp.exp(w))
    a = jax.nn.sigmoid((a0 + al @ a2).astype(f32))
    g = (jax.nn.sigmoid(gl) @ g2).astype(f32)
    kf = k.astype(f32)
    kk = (kf * k_k.astype(f32)).reshape(B, S, RW_HEADS, RW_DH)
    kk = kk / jnp.maximum(jnp.sqrt(jnp.sum(kk * kk, axis=-1, keepdims=True)), 1e-12)
    kf = kf * (1.0 + (a - 1.0) * k_a.astype(f32))
    heads = lambda t: t.reshape(B, S, RW_HEADS, RW_DH)
    rh, kh, vh, ah, dh_ = heads(r.astype(f32)), heads(kf), heads(v.astype(f32)), heads(a), heads(decay)
    y = _rwkv7_scan(rh, dh_, kh, vh, -kk, kk * ah)
    mean = jnp.mean(y, axis=-1, keepdims=True)
    var = jnp.mean(jnp.square(y - mean), axis=-1, keepdims=True)
    y = ((y - mean) * lax.rsqrt(var + RW_LNX_EPS)).reshape(B, S, RW_W) * lnx_w.astype(f32) + lnx_b.astype(f32)
    bonus = jnp.sum(rh * kh * r_k.astype(f32), axis=-1, keepdims=True) * vh
    y = (y + bonus.reshape(B, S, RW_W)) * g
    return y.astype(u.dtype)


def _rglru(u, conv_w, conv_b, wa, ba, wx, bx, lam):
    B, S, _ = u.shape
    f32 = jnp.float32
    gate_in, xb = _split(u, [LRU_W, LRU_W])
    xc = lax.conv_general_dilated(xb, conv_w[:, None, :], window_strides=(1,), padding=[(CONV_W - 1, 0)],
                                  dimension_numbers=('NWC', 'WIO', 'NWC'), feature_group_count=LRU_W) + conv_b
    blocks = xc.reshape(B, S, LRU_BLOCKS, LRU_BW)
    r = jax.nn.sigmoid(jnp.einsum('bsnc,ncd->bsnd', blocks, wa).reshape(B, S, LRU_W).astype(f32) + ba.astype(f32))
    i = jax.nn.sigmoid(jnp.einsum('bsnc,ncd->bsnd', blocks, wx).reshape(B, S, LRU_W).astype(f32) + bx.astype(f32))
    log_a = -LRU_C * r * jax.nn.softplus(-lam.astype(f32))
    a = jnp.exp(log_a)
    b = jnp.sqrt(-jnp.expm1(2.0 * log_a)) * i * xc.astype(f32)

    def combine(lhs, rhs):
        a1, b1 = lhs
        a2, b2 = rhs
        return a1 * a2, a2 * b1 + b2

    _, h = lax.associative_scan(combine, (a, b), axis=1)
    return (h * jax.nn.gelu(gate_in.astype(f32))).astype(u.dtype)


def _hgrn2_chunkwise(q, k, v, log_f):
    B, S, H, dk = q.shape
    dv = v.shape[-1]
    n_chunk = S // HG_CHUNK

    def chunks(t):
        return t.reshape(B, n_chunk, HG_CHUNK, H, t.shape[-1]).transpose(1, 0, 3, 2, 4)

    tri = jnp.tril(jnp.ones((HG_CHUNK, HG_CHUNK), dtype=bool))

    def step(state, inp):
        qc, kc, vc, lc = inp
        bcum = jnp.cumsum(lc, axis=2)
        o = jnp.einsum('bhtk,bhkv->bhtv', qc * jnp.exp(bcum), state)
        diff = bcum[:, :, :, None, :] - bcum[:, :, None, :, :]
        decay = jnp.exp(jnp.where(tri[:, :, None], diff, NEG_INF))
        att = jnp.einsum('bhtk,bhsk,bhtsk->bhts', qc, kc, decay)
        o = o + jnp.einsum('bhts,bhsv->bhtv', att, vc)
        b_last = bcum[:, :, -1:, :]
        state = (jnp.exp(b_last[:, :, 0, :])[..., None] * state
                 + jnp.einsum('bhsk,bhsv->bhkv', kc * jnp.exp(b_last - bcum), vc))
        return state, o

    s0 = jnp.zeros((B, H, dk, dv), jnp.float32)
    _, o = lax.scan(step, s0, (chunks(q), chunks(k), chunks(v), chunks(log_f)))
    return o.transpose(1, 0, 3, 2, 4).reshape(B, S, H, dv)


def _hgrn2(u, lower, norm_g):
    B, S, _ = u.shape
    f32 = jnp.float32
    q, f, i, g = _split(u, [HG_KW, HG_KW, HG_VW, HG_VW])
    forget = lower + (1.0 - lower) * jax.nn.sigmoid(f.astype(f32))
    qh = jax.nn.silu(q.astype(f32)).reshape(B, S, HG_HEADS, HG_DK)
    kh = (1.0 - forget).reshape(B, S, HG_HEADS, HG_DK)
    lh = jnp.log(forget).reshape(B, S, HG_HEADS, HG_DK)
    vh = i.astype(f32).reshape(B, S, HG_HEADS, HG_DV)
    o = _hgrn2_chunkwise(qh, kh, vh, lh)
    o = _rms_norm(o, norm_g.reshape(HG_HEADS, HG_DV)).reshape(B, S, HG_VW)
    return (o * jax.nn.silu(g.astype(f32))).astype(u.dtype)


def _sq_relu_mlp(h, w1, w2):
    z = jax.nn.relu(h @ w1)
    return (z * z) @ w2


def setup_inputs(seed: int = 0) -> dict:
    key = jax.random.key(seed)
    keys = iter(jax.random.split(key, 48))
    f32 = jnp.float32

    def normal(shape, scale):
        return jax.random.normal(next(keys), shape, f32) * scale

    def gain(shape):
        return 1.0 + 0.02 * jax.random.normal(next(keys), shape, f32)

    def uniform(shape, lo, hi):
        return jax.random.uniform(next(keys), shape, f32, lo, hi)

    a_base = uniform((N_ODD, LRU_W), 0.9, 0.999) ** (1.0 / LRU_C)
    lam = jnp.log(a_base) - jnp.log1p(-a_base)
    return {
        'x': normal((BATCH, SEQ, D_MODEL), 1.0),
        'norm_mix': gain((DEPTH, D_MODEL)),
        'norm_mlp': gain((DEPTH, D_MODEL)),
        'w_ff1': normal((DEPTH, D_MODEL, D_FF), D_MODEL ** -0.5),
        'w_ff2': normal((DEPTH, D_FF, D_MODEL), D_FF ** -0.5),
        'w_in_a': normal((N_EVEN, D_MODEL, EVEN_COLS), D_MODEL ** -0.5),
        'w_out_a': normal((N_EVEN, MIX_EVEN, D_MODEL), MIX_EVEN ** -0.5),
        'nsa_qk_gain': gain((N_EVEN, 4, NSA_DH)),
        'nsa_cmp_w': normal((N_EVEN, 2, CMP_LEN, NSA_DH, NSA_DH), (CMP_LEN * NSA_DH) ** -0.5),
        'nsa_cmp_pe': normal((N_EVEN, 2, CMP_LEN, NSA_DH), 0.1),
        'rw_mu': uniform((N_EVEN, RW_COLS), 0.0, 1.0),
        'rw_w0': uniform((N_EVEN, RW_W), -6.0, -1.0),
        'rw_w2': normal((N_EVEN, RW_DECAY_LORA, RW_W), 0.5 * RW_DECAY_LORA ** -0.5),
        'rw_a0': normal((N_EVEN, RW_W), 0.1),
        'rw_a2': normal((N_EVEN, RW_A_LORA, RW_W), 0.5 * RW_A_LORA ** -0.5),
        'rw_g2': normal((N_EVEN, RW_G_LORA, RW_W), RW_G_LORA ** -0.5),
        'rw_k_k': 0.85 + normal((N_EVEN, RW_W), 0.02),
        'rw_k_a': gain((N_EVEN, RW_W)),
        'rw_r_k': normal((N_EVEN, RW_HEADS, RW_DH), 0.1),
        'rw_lnx_w': gain((N_EVEN, RW_W)),
        'rw_lnx_b': normal((N_EVEN, RW_W), 0.01),
        'w_in_b': normal((N_ODD, D_MODEL, ODD_COLS), D_MODEL ** -0.5),
        'w_out_b': normal((N_ODD, MIX_ODD, D_MODEL), MIX_ODD ** -0.5),
        'lru_conv_w': normal((N_ODD, CONV_W, LRU_W), CONV_W ** -0.5),
        'lru_conv_b': normal((N_ODD, LRU_W), 0.01),
        'lru_wa': normal((N_ODD, LRU_BLOCKS, LRU_BW, LRU_BW), LRU_BW ** -0.5),
        'lru_ba': normal((N_ODD, LRU_W), 0.01),
        'lru_wx': normal((N_ODD, LRU_BLOCKS, LRU_BW, LRU_BW), LRU_BW ** -0.5),
        'lru_bx': normal((N_ODD, LRU_W), 0.01),
        'lru_lambda': lam,
        'hg_lb': normal((DEPTH, HG_KW), 1.0),
        'hg_norm': gain((N_ODD, HG_VW)),
    }


def reference(x, norm_mix, norm_mlp, w_ff1, w_ff2, w_in_a, w_out_a, nsa_qk_gain, nsa_cmp_w, nsa_cmp_pe,
              rw_mu, rw_w0, rw_w2, rw_a0, rw_a2, rw_g2, rw_k_k, rw_k_a, rw_r_k, rw_lnx_w, rw_lnx_b,
              w_in_b, w_out_b, lru_conv_w, lru_conv_b, lru_wa, lru_ba, lru_wx, lru_bx, lru_lambda,
              hg_lb, hg_norm):
    B, S, _ = x.shape
    cos, sin = _rope_tables(jnp.arange(S, dtype=jnp.float32))
    lb_p = jax.nn.softmax(hg_lb.astype(jnp.float32), axis=0)
    lb_cum = jnp.cumsum(lb_p, axis=0)
    hg_lower = lb_cum - lb_cum[0:1]
    for layer in range(DEPTH):
        h = _rms_norm(x, norm_mix[layer])
        if layer % 2 == 0:
            e = layer // 2
            u = h @ w_in_a[e]
            u_nsa, u_rw = _split(u, [NSA_COLS, RW_COLS])
            y_a = _nsa(u_nsa, nsa_qk_gain[e], nsa_cmp_w[e], nsa_cmp_pe[e], cos, sin)
            y_b = _rwkv7(u_rw, rw_mu[e], rw_w0[e], rw_w2[e], rw_a0[e], rw_a2[e], rw_g2[e],
                         rw_k_k[e], rw_k_a[e], rw_r_k[e], rw_lnx_w[e], rw_lnx_b[e])
            y = jnp.concatenate([y_a, y_b], axis=-1) @ w_out_a[e]
        else:
            o = layer // 2
            u = h @ w_in_b[o]
            u_lru, u_hg = _split(u, [2 * LRU_W, 2 * HG_KW + 2 * HG_VW])
            y_c = _rglru(u_lru, lru_conv_w[o], lru_conv_b[o], lru_wa[o], lru_ba[o], lru_wx[o], lru_bx[o], lru_lambda[o])
            y_d = _hgrn2(u_hg, hg_lower[layer], hg_norm[o])
            y = jnp.concatenate([y_c, y_d], axis=-1) @ w_out_b[o]
        x = x + y.astype(x.dtype)
        x = x + _sq_relu_mlp(_rms_norm(x, norm_mlp[layer]), w_ff1[layer], w_ff2[layer]).astype(x.dtype)
    return x
```

```python
import functools

import jax
import jax.numpy as jnp
import numpy as np
from jax import lax
from jax.experimental import pallas as pl
from jax.experimental.pallas import tpu as pltpu

F32 = jnp.float32
BF16 = jnp.bfloat16
HI = lax.Precision.HIGHEST

NORM_EPS = 1e-6
NEG_INF = -1e30
ROPE_THETA = 10000.0

NSA_HEADS = 8
NSA_KV_HEADS = 2
NSA_GROUP = NSA_HEADS // NSA_KV_HEADS
NSA_DH = 128
CMP_LEN = 32
CMP_STRIDE = 16
SLC_LEN = 64
SLC_TOPN = 16
WIN = 512
Q_BLOCK = 128
SLC_FORCE = 1e4
NSA_W = NSA_HEADS * NSA_DH
NSA_KV_W = NSA_KV_HEADS * NSA_DH
NSA_COLS = NSA_W + 6 * NSA_KV_W + 3 * NSA_HEADS
NSA_COLS_PAD = NSA_W + 6 * NSA_KV_W + 128
SLC_TILE = 512

RW_HEADS = 16
RW_DH = 64
RW_W = RW_HEADS * RW_DH
RW_DECAY_LORA = 96
RW_A_LORA = 96
RW_G_LORA = 256
RW_LNX_EPS = 64e-5
RW_COLS = 3 * RW_W + RW_DECAY_LORA + RW_A_LORA + RW_G_LORA
LORA_PAD = 128
RW_COLS_PAD = 3 * RW_W + 2 * LORA_PAD + RW_G_LORA
RW_CHUNK = 64

LRU_W = 1024
LRU_BLOCKS = 8
LRU_BW = LRU_W // LRU_BLOCKS
CONV_W = 4
LRU_C = 8.0

HG_HEADS = 8
HG_DK = 128
HG_DV = 128
HG_KW = HG_HEADS * HG_DK
HG_VW = HG_HEADS * HG_DV
HG_CHUNK = 64
HG_SUB = 16

VMEM_LIMIT = 56 * 1024 * 1024


def _cp(*sem):
    return pltpu.CompilerParams(dimension_semantics=sem, vmem_limit_bytes=VMEM_LIMIT)


def _pick(n, cands):
    for c in cands:
        if n % c == 0:
            return c
    return n


def _sigmoid(x):
    return 1.0 / (1.0 + jnp.exp(-x))


def _softplus(x):
    return jnp.maximum(x, 0.0) + jnp.log(1.0 + jnp.exp(-jnp.abs(x)))


def _split3(x):
    p1 = x.astype(BF16)
    r1 = x - p1.astype(F32)
    p2 = r1.astype(BF16)
    p3 = (r1 - p2.astype(F32)).astype(BF16)
    return p1, p2, p3


def _dot(a, b):
    return jnp.dot(a, b, preferred_element_type=F32)


def _dot_hi(a, b):
    return jnp.dot(a, b, preferred_element_type=F32, precision=HI)


def _dot_nt(a, b, precision=None):
    return lax.dot_general(a, b, (((1,), (1,)), ((), ())), preferred_element_type=F32, precision=precision)


def _dot_tn(a, b, precision=None):
    return lax.dot_general(a, b, (((0,), (0,)), ((), ())), preferred_element_type=F32, precision=precision)


def _norm_matmul_kernel(x_ref, g_ref, w_ref, o_ref, hn_ref):
    @pl.when(pl.program_id(1) == 0)
    def _():
        x = x_ref[...]
        ms = jnp.mean(x * x, axis=-1, keepdims=True)
        hn_ref[...] = (x * lax.rsqrt(ms + NORM_EPS) * g_ref[...]).astype(BF16)

    o_ref[...] = _dot(hn_ref[...], w_ref[...])


def _norm_matmul(x, g, w):
    S, D = x.shape
    N = w.shape[1]
    tm = _pick(S, (512, 256, 128))
    tn = _pick(N, (896, 768, 512, 384, 256, 128))
    return pl.pallas_call(
        _norm_matmul_kernel,
        grid=(S // tm, N // tn),
        in_specs=[
            pl.BlockSpec((tm, D), lambda i, j: (i, 0)),
            pl.BlockSpec((1, D), lambda i, j: (0, 0)),
            pl.BlockSpec((D, tn), lambda i, j: (0, j)),
        ],
        out_specs=pl.BlockSpec((tm, tn), lambda i, j: (i, j)),
        out_shape=jax.ShapeDtypeStruct((S, N), F32),
        scratch_shapes=[pltpu.VMEM((tm, D), BF16)],
        compiler_params=_cp("parallel", "arbitrary"),
        name="norm_matmul",
    )(x, g.reshape(1, D), w)


def _matmul_kernel(a_ref, w_ref, o_ref):
    o_ref[...] = _dot(a_ref[...], w_ref[...])


def _matmul(a, w):
    M, K = a.shape
    N = w.shape[1]
    tm = _pick(M, (256, 128, 64, 32, 16, 8))
    return pl.pallas_call(
        _matmul_kernel,
        grid=(M // tm,),
        in_specs=[pl.BlockSpec((tm, K), lambda i: (i, 0)), pl.BlockSpec((K, N), lambda i: (0, 0))],
        out_specs=pl.BlockSpec((tm, N), lambda i: (i, 0)),
        out_shape=jax.ShapeDtypeStruct((M, N), F32),
        compiler_params=_cp("parallel"),
        name="matmul",
    )(a, w)


def _out_proj_kernel(a_ref, b_ref, wa_ref, wb_ref, r_ref, o_ref):
    o_ref[...] = r_ref[...] + _dot(a_ref[...], wa_ref[...]) + _dot(b_ref[...], wb_ref[...])


def _out_proj(ya, yb, w, res):
    S, Wa = ya.shape
    Wb = yb.shape[1]
    D = w.shape[1]
    tm = _pick(S, (512, 256, 128))
    return pl.pallas_call(
        _out_proj_kernel,
        grid=(S // tm,),
        in_specs=[
            pl.BlockSpec((tm, Wa), lambda i: (i, 0)),
            pl.BlockSpec((tm, Wb), lambda i: (i, 0)),
            pl.BlockSpec((Wa, D), lambda i: (0, 0)),
            pl.BlockSpec((Wb, D), lambda i: (0, 0)),
            pl.BlockSpec((tm, D), lambda i: (i, 0)),
        ],
        out_specs=pl.BlockSpec((tm, D), lambda i: (i, 0)),
        out_shape=jax.ShapeDtypeStruct((S, D), F32),
        compiler_params=_cp("parallel"),
        name="out_proj",
    )(ya, yb, w[:Wa], w[Wa:], res)


def _mlp_kernel(x_ref, g_ref, w1_ref, w2_ref, o_ref, hn_ref):
    f = pl.program_id(1)

    @pl.when(f == 0)
    def _():
        x = x_ref[...]
        ms = jnp.mean(x * x, axis=-1, keepdims=True)
        hn_ref[...] = (x * lax.rsqrt(ms + NORM_EPS) * g_ref[...]).astype(BF16)
        o_ref[...] = x

    z = jnp.maximum(_dot(hn_ref[...], w1_ref[...]), 0.0)
    o_ref[...] += _dot((z * z).astype(BF16), w2_ref[...])


def _mlp(x, g, w1, w2):
    S, D = x.shape
    Fdim = w1.shape[1]
    tm = _pick(S, (512, 256, 128))
    tf = _pick(Fdim, (512, 256, 128))
    return pl.pallas_call(
        _mlp_kernel,
        grid=(S // tm, Fdim // tf),
        in_specs=[
            pl.BlockSpec((tm, D), lambda i, f: (i, 0)),
            pl.BlockSpec((1, D), lambda i, f: (0, 0)),
            pl.BlockSpec((D, tf), lambda i, f: (0, f)),
            pl.BlockSpec((tf, D), lambda i, f: (f, 0)),
        ],
        out_specs=pl.BlockSpec((tm, D), lambda i, f: (i, 0)),
        out_shape=jax.ShapeDtypeStruct((S, D), F32),
        scratch_shapes=[pltpu.VMEM((tm, D), BF16)],
        compiler_params=_cp("parallel", "arbitrary"),
        name="mlp",
    )(x, g.reshape(1, D), w1, w2)


def _head_norm_rope(x, gain, cosf, sinf):
    ms = jnp.mean(x * x, axis=-1, keepdims=True)
    y = x * lax.rsqrt(ms + NORM_EPS) * gain
    return y * cosf + pltpu.roll(y, NSA_DH // 2, 1) * sinf


def _nsa_prep_kernel(u_ref, gain_ref, cos_ref, sin_ref,
                     q_ref, kc_ref, vc_ref, ks_ref, vs_ref, kw_ref, vw_ref, gate_ref):
    cosf = cos_ref[...]
    sinf = sin_ref[...]
    scale = NSA_DH ** -0.5
    for h in range(NSA_HEADS):
        sl = slice(h * NSA_DH, (h + 1) * NSA_DH)
        q_ref[:, sl] = (_head_norm_rope(u_ref[:, sl], gain_ref[0:1, :], cosf, sinf) * scale).astype(BF16)
    base = NSA_W
    kc_ref[...] = u_ref[:, base:base + NSA_KV_W].astype(BF16)
    vc_ref[...] = u_ref[:, base + NSA_KV_W:base + 2 * NSA_KV_W].astype(BF16)
    vs_ref[...] = u_ref[:, base + 3 * NSA_KV_W:base + 4 * NSA_KV_W].astype(BF16)
    vw_ref[...] = u_ref[:, base + 5 * NSA_KV_W:base + 6 * NSA_KV_W].astype(BF16)
    for g in range(NSA_KV_HEADS):
        sl = slice(g * NSA_DH, (g + 1) * NSA_DH)
        o_s = base + 2 * NSA_KV_W + g * NSA_DH
        o_w = base + 4 * NSA_KV_W + g * NSA_DH
        ks_ref[:, sl] = _head_norm_rope(u_ref[:, o_s:o_s + NSA_DH], gain_ref[2:3, :], cosf, sinf).astype(BF16)
        kw_ref[:, sl] = _head_norm_rope(u_ref[:, o_w:o_w + NSA_DH], gain_ref[3:4, :], cosf, sinf).astype(BF16)
    gate_ref[...] = _sigmoid(u_ref[:, base + 6 * NSA_KV_W:base + 6 * NSA_KV_W + 128])


def _nsa_prep(u, gain8, cosf, sinf):
    S = u.shape[0]
    tm = _pick(S, (512, 256, 128))
    row = lambda w: pl.BlockSpec((tm, w), lambda i: (i, 0))
    kv = jax.ShapeDtypeStruct((S, NSA_KV_W), BF16)
    return pl.pallas_call(
        _nsa_prep_kernel,
        grid=(S // tm,),
        in_specs=[row(NSA_COLS_PAD), pl.BlockSpec((8, NSA_DH), lambda i: (0, 0)), row(NSA_DH), row(NSA_DH)],
        out_specs=[row(NSA_W)] + [row(NSA_KV_W)] * 6 + [row(128)],
        out_shape=[jax.ShapeDtypeStruct((S, NSA_W), BF16)] + [kv] * 6 + [jax.ShapeDtypeStruct((S, 128), F32)],
        compiler_params=_cp("parallel"),
        name="nsa_prep",
    )(u, gain8, cosf, sinf)


def _nsa_cmp_finish_kernel(ak_ref, av_ref, pek_ref, pev_ref, gain_ref, cos_ref, sin_ref,
                           kcmp_ref, vcmp_ref, buf_ref):
    n = ak_ref.shape[0]
    buf_ref[n:n + 8, :] = jnp.zeros((8, NSA_KV_W), F32)

    def combine(a_ref, pe_ref):
        buf_ref[0:n, :] = a_ref[:, NSA_KV_W:]
        pe = pe_ref[0:1, 0:NSA_KV_W] + pe_ref[1:2, NSA_KV_W:]
        return a_ref[:, 0:NSA_KV_W] + buf_ref[pl.ds(1, n), :] + pe

    kc = combine(ak_ref, pek_ref)
    for g in range(NSA_KV_HEADS):
        sl = slice(g * NSA_DH, (g + 1) * NSA_DH)
        kcmp_ref[:, sl] = _head_norm_rope(kc[:, sl], gain_ref[1:2, :], cos_ref[...], sin_ref[...]).astype(BF16)
    vcmp_ref[...] = combine(av_ref, pev_ref).astype(BF16)


def _nsa_compress(kc, vc, cmp_w, cmp_pe, gain8, ccos, csin):
    S = kc.shape[0]
    n = S // CMP_STRIDE
    half = CMP_LEN // 2
    eye = jnp.eye(NSA_KV_HEADS, dtype=F32)

    def expand(w):
        lo = jnp.einsum('lde,gh->lgdhe', w[:half], eye).reshape(half * NSA_KV_W, NSA_KV_W)
        hi = jnp.einsum('lde,gh->lgdhe', w[half:], eye).reshape(half * NSA_KV_W, NSA_KV_W)
        return jnp.concatenate([lo, hi], axis=1).astype(BF16)

    def expand_pe(pe):
        lo = jnp.broadcast_to(pe[:half, None, :], (half, NSA_KV_HEADS, NSA_DH)).reshape(1, -1)
        hi = jnp.broadcast_to(pe[half:, None, :], (half, NSA_KV_HEADS, NSA_DH)).reshape(1, -1)
        return jnp.concatenate([lo, hi, jnp.zeros((6, lo.shape[1]), F32)], axis=0).astype(BF16)

    wk, wv = expand(cmp_w[0]), expand(cmp_w[1])
    ak = _matmul(kc.reshape(n, CMP_STRIDE * NSA_KV_W), wk)
    av = _matmul(vc.reshape(n, CMP_STRIDE * NSA_KV_W), wv)
    pek = _matmul(expand_pe(cmp_pe[0]), wk)
    pev = _matmul(expand_pe(cmp_pe[1]), wv)
    full = lambda shp: pl.BlockSpec(shp, lambda i: (0, 0))
    return pl.pallas_call(
        _nsa_cmp_finish_kernel,
        grid=(1,),
        in_specs=[full((n, 2 * NSA_KV_W)), full((n, 2 * NSA_KV_W)), full((8, 2 * NSA_KV_W)), full((8, 2 * NSA_KV_W)),
                  full((8, NSA_DH)), full((n, NSA_DH)), full((n, NSA_DH))],
        out_specs=[full((n, NSA_KV_W)), full((n, NSA_KV_W))],
        out_shape=[jax.ShapeDtypeStruct((n, NSA_KV_W), BF16)] * 2,
        scratch_shapes=[pltpu.VMEM((n + 8, NSA_KV_W), F32)],
        compiler_params=_cp("arbitrary"),
        name="nsa_cmp_finish",
    )(ak, av, pek, pev, gain8, ccos, csin)


def _nsa_attn_kernel(q_ref, kcmp_ref, vcmpT_ref, ovlT_ref, ks_ref, vsT_ref, kw_ref, vwT_ref, gate_ref,
                     o_ref, sel_ref, m_ref, l_ref, acc_ref, *, top_n):
    g = pl.program_id(0)
    blk = pl.program_id(1)
    t0 = blk * Q_BLOCK
    rows = NSA_GROUP * Q_BLOCK
    n_cmp = kcmp_ref.shape[0]
    n_slc = ovlT_ref.shape[0]

    qf = q_ref[...].astype(F32)
    qT = jnp.concatenate([qf[:, j * NSA_DH:(j + 1) * NSA_DH].T for j in range(NSA_GROUP)], axis=1).astype(BF16)
    lane = lax.broadcasted_iota(jnp.int32, (1, rows), 1)
    t_lane = t0 + (lane & (Q_BLOCK - 1))

    s = _dot(kcmp_ref[...], qT)
    n_io = lax.broadcasted_iota(jnp.int32, (n_cmp, 1), 0)
    valid = (n_io * CMP_STRIDE + (CMP_LEN - 1)) <= t_lane
    sm = jnp.where(valid, s, NEG_INF)
    mx = jnp.max(sm, axis=0, keepdims=True)
    e = jnp.where(valid, jnp.exp(sm - mx), 0.0)
    den = jnp.sum(e, axis=0, keepdims=True)
    p = e * jnp.where(den > 0.0, 1.0 / den, 0.0)
    o_c = _dot(vcmpT_ref[...], p.astype(BF16))
    psum = p[:, 0:Q_BLOCK]
    for j in range(1, NSA_GROUP):
        psum = psum + p[:, j * Q_BLOCK:(j + 1) * Q_BLOCK]
    p1, p2, p3 = _split3(psum)
    ovl = ovlT_ref[...]
    imp = _dot(ovl, p1) + _dot(ovl, p2) + _dot(ovl, p3)

    m_io = lax.broadcasted_iota(jnp.int32, (n_slc, 1), 0)
    q_io = lax.broadcasted_iota(jnp.int32, (1, Q_BLOCK), 1)
    cur = jnp.right_shift(t0 + q_io, 6)
    forced = (m_io == 0) | (m_io == cur) | (m_io == cur - 1)
    score = jnp.where(forced, SLC_FORCE, jnp.where(m_io <= cur, imp, -1.0))
    sel = jnp.zeros((n_slc, Q_BLOCK), F32)
    for _ in range(top_n):
        best = jnp.max(score, axis=0, keepdims=True)
        idx = jnp.min(jnp.where(score == best, m_io, n_slc), axis=0, keepdims=True)
        pick = m_io == idx
        sel = jnp.where(pick, 1.0, sel)
        score = jnp.where(pick, -3e38, score)
    sel_ref[...] = sel

    m_ref[...] = jnp.full((1, rows), NEG_INF, F32)
    l_ref[...] = jnp.zeros((1, rows), F32)
    acc_ref[...] = jnp.zeros((NSA_DH, rows), F32)
    blocks_per_tile = SLC_TILE // SLC_LEN

    def slc_step(kt, carry):
        k_tile = ks_ref[pl.ds(pl.multiple_of(kt * SLC_TILE, SLC_TILE), SLC_TILE), :]
        s = _dot(k_tile, qT)
        kpos = kt * SLC_TILE + lax.broadcasted_iota(jnp.int32, (SLC_TILE, 1), 0)
        selt = sel_ref[pl.ds(pl.multiple_of(kt * blocks_per_tile, blocks_per_tile), blocks_per_tile), :]
        selx = jnp.concatenate(
            [jnp.broadcast_to(selt[b:b + 1, :], (SLC_LEN, Q_BLOCK)) for b in range(blocks_per_tile)], axis=0)
        selx = jnp.concatenate([selx] * NSA_GROUP, axis=1)
        mask = (selx > 0.5) & (kpos <= t_lane)
        sm = jnp.where(mask, s, NEG_INF)
        m_old = m_ref[...]
        m_new = jnp.maximum(m_old, jnp.max(sm, axis=0, keepdims=True))
        p = jnp.where(mask, jnp.exp(sm - m_new), 0.0)
        alpha = jnp.exp(m_old - m_new)
        l_ref[...] = alpha * l_ref[...] + jnp.sum(p, axis=0, keepdims=True)
        acc_ref[...] = alpha * acc_ref[...] + _dot(vsT_ref[kt], p.astype(BF16))
        m_ref[...] = m_new
        return carry

    lax.fori_loop(0, (t0 + Q_BLOCK + SLC_TILE - 1) // SLC_TILE, slc_step, 0)
    o_s = acc_ref[...] * (1.0 / l_ref[...])

    wlen = WIN + Q_BLOCK
    start = pl.multiple_of(jnp.maximum(t0 - WIN, 0), Q_BLOCK)
    s = _dot(kw_ref[pl.ds(start, wlen), :], qT)
    kpos = start + lax.broadcasted_iota(jnp.int32, (wlen, 1), 0)
    d = t_lane - kpos
    mask = (d >= 0) & (d < WIN)
    sm = jnp.where(mask, s, NEG_INF)
    mx = jnp.max(sm, axis=0, keepdims=True)
    e = jnp.where(mask, jnp.exp(sm - mx), 0.0)
    p = (e * (1.0 / jnp.sum(e, axis=0, keepdims=True))).astype(BF16)
    tile0 = start // Q_BLOCK
    o_w = _dot(vwT_ref[tile0], p[0:Q_BLOCK, :])
    for i in range(1, wlen // Q_BLOCK):
        o_w = o_w + _dot(vwT_ref[tile0 + i], p[i * Q_BLOCK:(i + 1) * Q_BLOCK, :])

    gT = gate_ref[...].T
    for j in range(NSA_GROUP):
        sl = slice(j * Q_BLOCK, (j + 1) * Q_BLOCK)

        def grow(b):
            r0 = b * NSA_HEADS + j
            r1 = b * NSA_HEADS + NSA_GROUP + j
            return jnp.where(g == 0, gT[r0:r0 + 1, :], gT[r1:r1 + 1, :])

        o = grow(0) * o_c[:, sl] + grow(1) * o_s[:, sl] + grow(2) * o_w[:, sl]
        o_ref[:, j * NSA_DH:(j + 1) * NSA_DH] = o.T.astype(BF16)


def _nsa_attention(q, kcmp, vcmp, ks, vs, kw, vw, gate):
    S = q.shape[0]
    n_cmp = kcmp.shape[0]
    n_slc = S // SLC_LEN
    n_qb = S // Q_BLOCK
    top_n = min(SLC_TOPN, n_slc)
    rows = NSA_GROUP * Q_BLOCK

    c0 = np.arange(n_cmp)[None, :] * CMP_STRIDE
    s0 = np.arange(n_slc)[:, None] * SLC_LEN
    ovl = np.clip(np.minimum(c0 + CMP_LEN, s0 + SLC_LEN) - np.maximum(c0, s0), 0, None).astype(np.float32) / CMP_LEN
    ovl[:, n_cmp - 1] = 0.0
    ovlT = jnp.asarray(ovl, dtype=BF16)

    def tiles_T(v, width):
        return v.reshape(S // width, width, NSA_KV_HEADS, NSA_DH).transpose(2, 0, 3, 1)

    vcmpT = vcmp.reshape(n_cmp, NSA_KV_HEADS, NSA_DH).transpose(1, 2, 0)
    vsT = tiles_T(vs, SLC_TILE)
    vwT = tiles_T(vw, Q_BLOCK)

    return pl.pallas_call(
        functools.partial(_nsa_attn_kernel, top_n=top_n),
        grid=(NSA_KV_HEADS, n_qb),
        in_specs=[
            pl.BlockSpec((Q_BLOCK, rows), lambda g, b: (b, g)),
            pl.BlockSpec((n_cmp, NSA_DH), lambda g, b: (0, g)),
            pl.BlockSpec((None, NSA_DH, n_cmp), lambda g, b: (g, 0, 0)),
            pl.BlockSpec((n_slc, n_cmp), lambda g, b: (0, 0)),
            pl.BlockSpec((S, NSA_DH), lambda g, b: (0, g)),
            pl.BlockSpec((None, S // SLC_TILE, NSA_DH, SLC_TILE), lambda g, b: (g, 0, 0, 0)),
            pl.BlockSpec((S, NSA_DH), lambda g, b: (0, g)),
            pl.BlockSpec((None, S // Q_BLOCK, NSA_DH, Q_BLOCK), lambda g, b: (g, 0, 0, 0)),
            pl.BlockSpec((Q_BLOCK, 128), lambda g, b: (b, 0)),
        ],
        out_specs=pl.BlockSpec((Q_BLOCK, rows), lambda g, b: (b, g)),
        out_shape=jax.ShapeDtypeStruct((S, NSA_W), BF16),
        scratch_shapes=[
            pltpu.VMEM((n_slc, Q_BLOCK), F32),
            pltpu.VMEM((1, rows), F32),
            pltpu.VMEM((1, rows), F32),
            pltpu.VMEM((NSA_DH, rows), F32),
        ],
        compiler_params=_cp("arbitrary", "arbitrary"),
        name="nsa_attn",
    )(q, kcmp, vcmpT, ovlT, ks, vsT, kw, vwT, gate)


def _rope_tables(pos):
    inv = ROPE_THETA ** (-(jnp.arange(0, NSA_DH, 2, dtype=F32) / NSA_DH))
    ang = pos[:, None] * inv[None, :]
    c, s = jnp.cos(ang), jnp.sin(ang)
    return jnp.concatenate([c, c], axis=-1), jnp.concatenate([-s, s], axis=-1)


def _nsa_mixer(u, qk_gain, cmp_w, cmp_pe):
    S = u.shape[0]
    gain8 = jnp.concatenate([qk_gain.astype(F32), jnp.zeros((4, NSA_DH), F32)], axis=0)
    cosf, sinf = _rope_tables(jnp.arange(S, dtype=F32))
    q, kc, vc, ks, vs, kw, vw, gate = _nsa_prep(u, gain8, cosf, sinf)
    n = S // CMP_STRIDE
    ccos, csin = _rope_tables((jnp.arange(n) * CMP_STRIDE + CMP_LEN - 1).astype(F32))
    kcmp, vcmp = _nsa_compress(kc, vc, cmp_w, cmp_pe, gain8, ccos, csin)
    return _nsa_attention(q, kcmp, vcmp, ks, vs, kw, vw, gate)


def _seg_sum(x, ones_bd):
    outs = []
    for c in range(x.shape[1] // 128):
        p1, p2, p3 = _split3(x[:, c * 128:(c + 1) * 128])
        outs.append(_dot(p1, ones_bd) + _dot(p2, ones_bd) + _dot(p3, ones_bd))
    return jnp.concatenate(outs, axis=1)


def _rw_prep_kernel(u_ref, prev_ref, mu_ref, w0_ref, w2_ref, a0_ref, a2_ref, g2_ref, kk_ref, ka_ref, rk_ref, bd_ref,
                    r_ref, lw_ref, k_ref, v_ref, a_ref, b_ref, g_ref, bonus_ref, buf_ref):
    i = pl.program_id(0)
    tm = u_ref.shape[0]
    buf_ref[0:8, :] = jnp.where(i == 0, 0.0, prev_ref[...])
    buf_ref[8:8 + tm, :] = u_ref[...]
    u = u_ref[...]
    x = u + (buf_ref[pl.ds(7, tm), :] - u) * mu_ref[...]
    W = RW_W
    r, k, v = x[:, 0:W], x[:, W:2 * W], x[:, 2 * W:3 * W]
    wl = x[:, 3 * W:3 * W + LORA_PAD]
    al = x[:, 3 * W + LORA_PAD:3 * W + 2 * LORA_PAD]
    gl = x[:, 3 * W + 2 * LORA_PAD:]
    wx = w0_ref[...] + _dot(jnp.tanh(wl).astype(BF16), w2_ref[...])
    w = -_softplus(-wx) - 0.5
    a = _sigmoid(a0_ref[...] + _dot(al.astype(BF16), a2_ref[...]))
    g = _dot(_sigmoid(gl).astype(BF16), g2_ref[...])
    bd = bd_ref[...]
    kk = k * kk_ref[...]
    kk = kk / jnp.maximum(jnp.sqrt(_seg_sum(kk * kk, bd)), 1e-12)
    kf = k * (1.0 + (a - 1.0) * ka_ref[...])
    r_ref[...] = r
    lw_ref[...] = -jnp.exp(w)
    k_ref[...] = kf
    v_ref[...] = v
    a_ref[...] = -kk
    b_ref[...] = kk * a
    g_ref[...] = g
    bonus_ref[...] = _seg_sum(r * kf * rk_ref[...], bd) * v


def _rw_prep(u, mu, w0, w2, a0, a2, g2, k_k, k_a, r_k):
    S = u.shape[0]
    tm = _pick(S, (256, 128, 64))
    nb = tm // 8
    W = RW_W
    vec = lambda n: pl.BlockSpec((1, n), lambda i: (0, 0))
    mat = lambda a, b: pl.BlockSpec((a, b), lambda i: (0, 0))
    row = pl.BlockSpec((tm, W), lambda i: (i, 0))
    ones_bd = jnp.asarray(np.kron(np.eye(2), np.ones((RW_DH, RW_DH))), dtype=BF16)
    out = jax.ShapeDtypeStruct((S, W), F32)
    return pl.pallas_call(
        _rw_prep_kernel,
        grid=(S // tm,),
        in_specs=[
            pl.BlockSpec((tm, RW_COLS_PAD), lambda i: (i, 0)),
            pl.BlockSpec((8, RW_COLS_PAD), lambda i: (jnp.maximum(i * nb - 1, 0), 0)),
            vec(RW_COLS_PAD), vec(W), mat(LORA_PAD, W), vec(W), mat(LORA_PAD, W), mat(RW_G_LORA, W),
            vec(W), vec(W), vec(W), mat(128, 128),
        ],
        out_specs=[row] * 8,
        out_shape=[out] * 8,
        scratch_shapes=[pltpu.VMEM((tm + 8, RW_COLS_PAD), F32)],
        compiler_params=_cp("parallel"),
        name="rw_prep",
    )(u, u, mu, w0, w2, a0, a2, g2, k_k, k_a, r_k, ones_bd)


def _rw_chunk_kernel(r_ref, lw_ref, k_ref, v_ref, a_ref, b_ref, lhs_ref, add_ref):
    C = RW_CHUNK
    ti = lax.broadcasted_iota(jnp.int32, (C, C), 0)
    si = lax.broadcasted_iota(jnp.int32, (C, C), 1)
    tri_incl = (si <= ti).astype(F32)
    low = si < ti
    lowi = si <= ti
    eye = (si == ti).astype(F32)

    lw = lw_ref[...]
    cs = _dot_hi(tri_incl, lw)
    cprev = cs - lw
    clast = cs[C - 1:C, :]
    e_neg = jnp.exp(-cs)
    e_end = jnp.exp(clast - cs)
    At = a_ref[...] * jnp.exp(cprev)
    Rt = r_ref[...] * jnp.exp(cs)
    Bt = b_ref[...] * e_neg
    Kt = k_ref[...] * e_neg
    Bh = b_ref[...] * e_end
    Kh = k_ref[...] * e_end
    V = v_ref[...]
    gam = jnp.exp(clast)

    for h in range(RW_HEADS):
        sl = slice(h * RW_DH, (h + 1) * RW_DH)
        At_h, Rt_h, Bt_h, Kt_h, Bh_h, Kh_h, V_h = At[:, sl], Rt[:, sl], Bt[:, sl], Kt[:, sl], Bh[:, sl], Kh[:, sl], V[:, sl]
        L = jnp.where(low, _dot_nt(At_h, Bt_h, HI), 0.0)
        Aak = jnp.where(low, _dot_nt(At_h, Kt_h, HI), 0.0)
        Arb = jnp.where(lowi, _dot_nt(Rt_h, Bt_h, HI), 0.0)
        Ark = jnp.where(lowi, _dot_nt(Rt_h, Kt_h, HI), 0.0)
        T = eye + L
        P = L
        for _ in range(5):
            P = _dot_hi(P, P)
            T = T + _dot_hi(T, P)
        Ahat = _dot_hi(T, At_h)
        Uhat = _dot_hi(T, _dot_hi(Aak, V_h))
        Rbar = Rt_h + _dot_hi(Arb, Ahat)
        Ybar = _dot_hi(Arb, Uhat) + _dot_hi(Ark, V_h)
        Phi = eye * gam[:, sl] + _dot_tn(Bh_h, Ahat, HI)
        Hhat = _dot_tn(Bh_h, Uhat, HI) + _dot_tn(Kh_h, V_h, HI)
        lhs_ref[0:C, sl] = Rbar
        lhs_ref[C:2 * C, sl] = Phi
        add_ref[0:C, sl] = Ybar
        add_ref[C:2 * C, sl] = Hhat


def _rw_chunks(r, lw, k, v, a, b):
    S = r.shape[0]
    C = RW_CHUNK
    row = pl.BlockSpec((C, RW_W), lambda c: (c, 0))
    out = pl.BlockSpec((2 * C, RW_W), lambda c: (c, 0))
    shp = jax.ShapeDtypeStruct((2 * S, RW_W), F32)
    return pl.pallas_call(
        _rw_chunk_kernel,
        grid=(S // C,),
        in_specs=[row] * 6,
        out_specs=[out, out],
        out_shape=[shp, shp],
        compiler_params=_cp("parallel"),
        name="rw_chunk",
    )(r, lw, k, v, a, b)


def _rw_scan_kernel(lhs_ref, add_ref, bonus_ref, g_ref, lnw_ref, lnb_ref, y_ref, h_ref):
    C = RW_CHUNK

    @pl.when(pl.program_id(0) == 0)
    def _():
        h_ref[...] = jnp.zeros_like(h_ref)

    for h in range(RW_HEADS):
        sl = slice(h * RW_DH, (h + 1) * RW_DH)
        res = _dot_hi(lhs_ref[:, sl], h_ref[:, sl]) + add_ref[:, sl]
        h_ref[:, sl] = res[C:2 * C, :]
        y = res[0:C, :]
        mean = jnp.mean(y, axis=-1, keepdims=True)
        yc = y - mean
        var = jnp.mean(yc * yc, axis=-1, keepdims=True)
        yn = yc * lax.rsqrt(var + RW_LNX_EPS) * lnw_ref[:, sl] + lnb_ref[:, sl]
        y_ref[:, sl] = ((yn + bonus_ref[:, sl]) * g_ref[:, sl]).astype(BF16)


def _rw_scan(lhs, add, bonus, g, lnx_w, lnx_b):
    S = bonus.shape[0]
    C = RW_CHUNK
    row = pl.BlockSpec((C, RW_W), lambda c: (c, 0))
    two = pl.BlockSpec((2 * C, RW_W), lambda c: (c, 0))
    vec = pl.BlockSpec((1, RW_W), lambda c: (0, 0))
    return pl.pallas_call(
        _rw_scan_kernel,
        grid=(S // C,),
        in_specs=[two, two, row, row, vec, vec],
        out_specs=row,
        out_shape=jax.ShapeDtypeStruct((S, RW_W), BF16),
        scratch_shapes=[pltpu.VMEM((RW_DH, RW_W), F32)],
        compiler_params=_cp("arbitrary"),
        name="rw_scan",
    )(lhs, add, bonus, g, lnx_w, lnx_b)


def _pad_lora_cols(x, axis):
    W3 = 3 * RW_W
    parts = [lax.slice_in_dim(x, 0, W3, axis=axis),
             lax.slice_in_dim(x, W3, W3 + RW_DECAY_LORA, axis=axis),
             lax.slice_in_dim(x, W3 + RW_DECAY_LORA, W3 + RW_DECAY_LORA + RW_A_LORA, axis=axis),
             lax.slice_in_dim(x, W3 + RW_DECAY_LORA + RW_A_LORA, RW_COLS, axis=axis)]

    def padto(p, n):
        cfg = [(0, 0)] * x.ndim
        cfg[axis] = (0, n - p.shape[axis])
        return jnp.pad(p, cfg)

    return jnp.concatenate([parts[0], padto(parts[1], LORA_PAD), padto(parts[2], LORA_PAD), parts[3]], axis=axis)


def _rwkv_mixer(u, mu, w0, w2, a0, a2, g2, k_k, k_a, r_k, lnx_w, lnx_b):
    row = lambda p: p.reshape(1, -1).astype(F32)
    padrows = lambda m: jnp.pad(m, ((0, LORA_PAD - m.shape[0]), (0, 0))).astype(BF16)
    r, lw, k, v, a, b, g, bonus = _rw_prep(
        u, _pad_lora_cols(row(mu), 1), row(w0), padrows(w2), row(a0), padrows(a2), g2.astype(BF16),
        row(k_k), row(k_a), row(r_k))
    lhs, add = _rw_chunks(r, lw, k, v, a, b)
    return _rw_scan(lhs, add, bonus, g, row(lnx_w), row(lnx_b))


def _lru_kernel(u_ref, cw_ref, cb_ref, wa_ref, ba_ref, wx_ref, bx_ref, sp_ref, y_ref, buf_ref, a_sc, b_sc, h_sc):
    i = pl.program_id(0)
    tm = u_ref.shape[0]

    @pl.when(i == 0)
    def _():
        buf_ref[0:8, :] = jnp.zeros((8, LRU_W), F32)
        h_sc[...] = jnp.zeros_like(h_sc)

    xb = u_ref[:, LRU_W:]
    buf_ref[8:8 + tm, :] = xb
    xc = cb_ref[...] + cw_ref[CONV_W - 1:CONV_W, :] * xb
    for kk in range(CONV_W - 1):
        xc = xc + cw_ref[kk:kk + 1, :] * buf_ref[pl.ds(8 - (CONV_W - 1) + kk, tm), :]
    buf_ref[0:8, :] = xb[tm - 8:tm, :]

    xcb = xc.astype(BF16)
    ra, ri = [], []
    for n in range(LRU_BLOCKS):
        sl = slice(n * LRU_BW, (n + 1) * LRU_BW)
        ra.append(_dot(xcb[:, sl], wa_ref[n]))
        ri.append(_dot(xcb[:, sl], wx_ref[n]))
    rg = _sigmoid(jnp.concatenate(ra, axis=1) + ba_ref[...])
    ig = _sigmoid(jnp.concatenate(ri, axis=1) + bx_ref[...])
    log_a = -LRU_C * rg * sp_ref[...]
    a_sc[...] = jnp.exp(log_a)
    b_sc[...] = jnp.sqrt(1.0 - jnp.exp(2.0 * log_a)) * ig * xc

    def step(t, h):
        h = a_sc[pl.ds(t, 1), :] * h + b_sc[pl.ds(t, 1), :]
        b_sc[pl.ds(t, 1), :] = h
        return h

    h_sc[...] = lax.fori_loop(0, tm, step, h_sc[...], unroll=8)
    gate = u_ref[:, 0:LRU_W]
    gelu = 0.5 * gate * (1.0 + jnp.tanh(0.7978845608028654 * (gate + 0.044715 * gate * gate * gate)))
    y_ref[...] = (b_sc[...] * gelu).astype(BF16)


def _lru_mixer(u, conv_w, conv_b, wa, ba, wx, bx, lam):
    S = u.shape[0]
    tm = _pick(S, (256, 128, 64))
    row = lambda p: p.reshape(1, -1).astype(F32)
    vec = pl.BlockSpec((1, LRU_W), lambda i: (0, 0))
    blk = pl.BlockSpec((LRU_BLOCKS, LRU_BW, LRU_BW), lambda i: (0, 0, 0))
    lamf = lam.astype(F32)
    softplus_neg_lam = row(jnp.maximum(-lamf, 0.0) + jnp.log1p(jnp.exp(-jnp.abs(lamf))))
    cw = jnp.concatenate([conv_w.astype(F32), jnp.zeros((8 - CONV_W, LRU_W), F32)], axis=0)
    return pl.pallas_call(
        _lru_kernel,
        grid=(S // tm,),
        in_specs=[pl.BlockSpec((tm, 2 * LRU_W), lambda i: (i, 0)), pl.BlockSpec((8, LRU_W), lambda i: (0, 0)),
                  vec, blk, vec, blk, vec, vec],
        out_specs=pl.BlockSpec((tm, LRU_W), lambda i: (i, 0)),
        out_shape=jax.ShapeDtypeStruct((S, LRU_W), BF16),
        scratch_shapes=[pltpu.VMEM((tm + 8, LRU_W), F32), pltpu.VMEM((tm, LRU_W), F32),
                        pltpu.VMEM((tm, LRU_W), F32), pltpu.VMEM((1, LRU_W), F32)],
        compiler_params=_cp("arbitrary"),
        name="rglru",
    )(u, cw, row(conv_b), wa.astype(BF16), row(ba), wx.astype(BF16), row(bx), softplus_neg_lam)


def _hgrn_kernel(u_ref, lower_ref, ng_ref, y_ref, st_ref):
    C = HG_CHUNK
    W = HG_KW

    @pl.when(pl.program_id(0) == 0)
    def _():
        st_ref[...] = jnp.zeros_like(st_ref)

    lower = lower_ref[...]
    forget = lower + (1.0 - lower) * _sigmoid(u_ref[:, W:2 * W])
    lf = jnp.log(forget)
    kk = 1.0 - forget
    qx = u_ref[:, 0:W]
    q = qx * _sigmoid(qx)
    v = u_ref[:, 2 * W:3 * W]
    gx = u_ref[:, 3 * W:4 * W]

    ti = lax.broadcasted_iota(jnp.int32, (C, C), 0)
    si = lax.broadcasted_iota(jnp.int32, (C, C), 1)
    b = _dot_hi((si <= ti).astype(F32), lf)
    blast = b[C - 1:C, :]
    q_in = (q * jnp.exp(b)).astype(BF16)
    k_end = (kk * jnp.exp(blast - b)).astype(BF16)
    vb = v.astype(BF16)
    rowi = lax.broadcasted_iota(jnp.int32, (C, W), 0)
    sub_t = lax.broadcasted_iota(jnp.int32, (HG_SUB, C), 0)
    sub_s = lax.broadcasted_iota(jnp.int32, (HG_SUB, C), 1)

    n_sub = C // HG_SUB
    q_sub, k_sub = [], []
    for i in range(n_sub):
        ref = jnp.zeros((1, W), F32) if i == 0 else b[i * HG_SUB - 1:i * HG_SUB, :]
        q_sub.append((q[i * HG_SUB:(i + 1) * HG_SUB, :] * jnp.exp(b[i * HG_SUB:(i + 1) * HG_SUB, :] - ref)).astype(BF16))
        live = rowi < (i + 1) * HG_SUB
        k_sub.append(jnp.where(live, kk * jnp.exp(jnp.where(live, ref - b, 0.0)), 0.0).astype(BF16))

    for h in range(HG_HEADS):
        sl = slice(h * HG_DK, (h + 1) * HG_DK)
        state = st_ref[:, sl]
        o = _dot_nt(q_in[:, sl], state.astype(BF16))
        intra = []
        for i in range(n_sub):
            att = _dot_nt(q_sub[i][:, sl], k_sub[i][:, sl])
            att = jnp.where(sub_s <= sub_t + i * HG_SUB, att, 0.0)
            intra.append(_dot(att.astype(BF16), vb[:, sl]))
        o = o + jnp.concatenate(intra, axis=0)
        st_ref[:, sl] = jnp.exp(blast[:, sl]) * state + _dot_tn(vb[:, sl], k_end[:, sl])
        ms = jnp.mean(o * o, axis=-1, keepdims=True)
        on = o * lax.rsqrt(ms + NORM_EPS) * ng_ref[:, sl]
        gh = gx[:, sl]
        y_ref[:, sl] = (on * (gh * _sigmoid(gh))).astype(BF16)


def _hgrn_mixer(u, lower, norm_g):
    S = u.shape[0]
    C = HG_CHUNK
    vec = pl.BlockSpec((1, HG_KW), lambda c: (0, 0))
    return pl.pallas_call(
        _hgrn_kernel,
        grid=(S // C,),
        in_specs=[pl.BlockSpec((C, 4 * HG_KW), lambda c: (c, 0)), vec, vec],
        out_specs=pl.BlockSpec((C, HG_VW), lambda c: (c, 0)),
        out_shape=jax.ShapeDtypeStruct((S, HG_VW), BF16),
        scratch_shapes=[pltpu.VMEM((HG_DV, HG_KW), F32)],
        compiler_params=_cp("arbitrary"),
        name="hgrn2",
    )(u, lower.reshape(1, -1).astype(F32), norm_g.reshape(1, -1).astype(F32))


def kernel(x, norm_mix, norm_mlp, w_ff1, w_ff2, w_in_a, w_out_a, nsa_qk_gain, nsa_cmp_w, nsa_cmp_pe, rw_mu, rw_w0, rw_w2, rw_a0, rw_a2, rw_g2, rw_k_k, rw_k_a, rw_r_k, rw_lnx_w, rw_lnx_b, w_in_b, w_out_b, lru_conv_w, lru_conv_b, lru_wa, lru_ba, lru_wx, lru_bx, lru_lambda, hg_lb, hg_norm):
    B, S, D = x.shape
    depth = norm_mix.shape[0]
    lb_p = jax.nn.softmax(hg_lb.astype(F32), axis=0)
    lb_cum = jnp.cumsum(lb_p, axis=0)
    hg_lower = lb_cum - lb_cum[0:1]

    outs = []
    for bi in range(B):
        xb = x[bi].astype(F32)
        for layer in range(depth):
            gmix = norm_mix[layer].astype(F32)
            if layer % 2 == 0:
                e = layer // 2
                w_in = w_in_a[e]
                w_nsa = jnp.pad(w_in[:, :NSA_COLS], ((0, 0), (0, NSA_COLS_PAD - NSA_COLS))).astype(BF16)
                w_rw = _pad_lora_cols(w_in[:, NSA_COLS:], 1).astype(BF16)
                y_a = _nsa_mixer(_norm_matmul(xb, gmix, w_nsa), nsa_qk_gain[e], nsa_cmp_w[e], nsa_cmp_pe[e])
                y_b = _rwkv_mixer(_norm_matmul(xb, gmix, w_rw), rw_mu[e], rw_w0[e], rw_w2[e], rw_a0[e], rw_a2[e],
                                  rw_g2[e], rw_k_k[e], rw_k_a[e], rw_r_k[e], rw_lnx_w[e], rw_lnx_b[e])
                xb = _out_proj(y_a, y_b, w_out_a[e].astype(BF16), xb)
            else:
                o = layer // 2
                w_in = w_in_b[o].astype(BF16)
                y_c = _lru_mixer(_norm_matmul(xb, gmix, w_in[:, :2 * LRU_W]), lru_conv_w[o], lru_conv_b[o],
                                 lru_wa[o], lru_ba[o], lru_wx[o], lru_bx[o], lru_lambda[o])
                y_d = _hgrn_mixer(_norm_matmul(xb, gmix, w_in[:, 2 * LRU_W:]), hg_lower[layer], hg_norm[o])
                xb = _out_proj(y_c, y_d, w_out_b[o].astype(BF16), xb)
            xb = _mlp(xb, norm_mlp[layer].astype(F32), w_ff1[layer].astype(BF16), w_ff2[layer].astype(BF16))
        outs.append(xb)
    return jnp.stack(outs, axis=0).astype(x.dtype)
```

```python
import functools

import jax
import jax.numpy as jnp
import numpy as np
from jax import lax
from jax.experimental import pallas as pl
from jax.experimental.pallas import tpu as pltpu

F32 = jnp.float32
BF16 = jnp.bfloat16
HI = lax.Precision.HIGHEST

NORM_EPS = 1e-6
NEG_INF = -1e30
ROPE_THETA = 10000.0
LOG2E = 1.4426950408889634

NSA_HEADS = 8
NSA_KV_HEADS = 2
NSA_GROUP = NSA_HEADS // NSA_KV_HEADS
NSA_DH = 128
CMP_LEN = 32
CMP_STRIDE = 16
SLC_LEN = 64
SLC_TOPN = 16
WIN = 512
Q_BLOCK = 128
SLC_FORCE = 1e4
NSA_W = NSA_HEADS * NSA_DH
NSA_KV_W = NSA_KV_HEADS * NSA_DH
NSA_COLS = NSA_W + 6 * NSA_KV_W + 3 * NSA_HEADS
NSA_COLS_PAD = NSA_W + 6 * NSA_KV_W + 128
SLC_TILE = 512

RW_HEADS = 16
RW_DH = 64
RW_W = RW_HEADS * RW_DH
RW_DECAY_LORA = 96
RW_A_LORA = 96
RW_G_LORA = 256
RW_LNX_EPS = 64e-5
RW_COLS = 3 * RW_W + RW_DECAY_LORA + RW_A_LORA + RW_G_LORA
LORA_PAD = 128
RW_COLS_PAD = 3 * RW_W + 2 * LORA_PAD + RW_G_LORA
RW_CHUNK = 64
RW_PACK = 4

LRU_W = 1024
LRU_BLOCKS = 8
LRU_BW = LRU_W // LRU_BLOCKS
CONV_W = 4
LRU_C = 8.0

HG_HEADS = 8
HG_DK = 128
HG_DV = 128
HG_KW = HG_HEADS * HG_DK
HG_VW = HG_HEADS * HG_DV
HG_CHUNK = 64
HG_SUB = 16

VMEM_LIMIT = 56 * 1024 * 1024


def _cp(*sem):
    return pltpu.CompilerParams(dimension_semantics=sem, vmem_limit_bytes=VMEM_LIMIT)


def _pick(n, cands):
    for c in cands:
        if n % c == 0:
            return c
    return n


def _sigmoid(x):
    return 1.0 / (1.0 + jnp.exp(-x))


def _softplus(x):
    return jnp.maximum(x, 0.0) + jnp.log(1.0 + jnp.exp(-jnp.abs(x)))


def _split3(x):
    p1 = x.astype(BF16)
    r1 = x - p1.astype(F32)
    p2 = r1.astype(BF16)
    p3 = (r1 - p2.astype(F32)).astype(BF16)
    return p1, p2, p3


def _dot(a, b):
    return jnp.dot(a, b, preferred_element_type=F32)


def _dot_hi(a, b):
    return jnp.dot(a, b, preferred_element_type=F32, precision=HI)


def _dot_nt(a, b, precision=None):
    return lax.dot_general(a, b, (((1,), (1,)), ((), ())), preferred_element_type=F32, precision=precision)


def _dot_tn(a, b, precision=None):
    return lax.dot_general(a, b, (((0,), (0,)), ((), ())), preferred_element_type=F32, precision=precision)


def _norm_matmul_kernel(x_ref, g_ref, w_ref, o_ref, hn_ref):
    @pl.when(pl.program_id(1) == 0)
    def _():
        x = x_ref[...]
        ms = jnp.mean(x * x, axis=-1, keepdims=True)
        hn_ref[...] = (x * lax.rsqrt(ms + NORM_EPS) * g_ref[...]).astype(BF16)

    o_ref[...] = _dot(hn_ref[...], w_ref[...])


def _norm_matmul(x, g, w):
    S, D = x.shape
    N = w.shape[1]
    tm = _pick(S, (512, 256, 128))
    tn = _pick(N, (896, 768, 512, 384, 256, 128))
    return pl.pallas_call(
        _norm_matmul_kernel,
        grid=(S // tm, N // tn),
        in_specs=[
            pl.BlockSpec((tm, D), lambda i, j: (i, 0)),
            pl.BlockSpec((1, D), lambda i, j: (0, 0)),
            pl.BlockSpec((D, tn), lambda i, j: (0, j)),
        ],
        out_specs=pl.BlockSpec((tm, tn), lambda i, j: (i, j)),
        out_shape=jax.ShapeDtypeStruct((S, N), F32),
        scratch_shapes=[pltpu.VMEM((tm, D), BF16)],
        compiler_params=_cp("parallel", "arbitrary"),
        name="norm_matmul",
    )(x, g.reshape(1, D), w)


def _matmul_kernel(a_ref, w_ref, o_ref):
    o_ref[...] = _dot(a_ref[...], w_ref[...])


def _matmul(a, w):
    M, K = a.shape
    N = w.shape[1]
    tm = _pick(M, (256, 128, 64, 32, 16, 8))
    return pl.pallas_call(
        _matmul_kernel,
        grid=(M // tm,),
        in_specs=[pl.BlockSpec((tm, K), lambda i: (i, 0)), pl.BlockSpec((K, N), lambda i: (0, 0))],
        out_specs=pl.BlockSpec((tm, N), lambda i: (i, 0)),
        out_shape=jax.ShapeDtypeStruct((M, N), F32),
        compiler_params=_cp("parallel"),
        name="matmul",
    )(a, w)


def _out_proj_kernel(a_ref, b_ref, wa_ref, wb_ref, r_ref, o_ref):
    o_ref[...] = r_ref[...] + _dot(a_ref[...], wa_ref[...]) + _dot(b_ref[...], wb_ref[...])


def _out_proj(ya, yb, w, res):
    S, Wa = ya.shape
    Wb = yb.shape[1]
    D = w.shape[1]
    tm = _pick(S, (512, 256, 128))
    return pl.pallas_call(
        _out_proj_kernel,
        grid=(S // tm,),
        in_specs=[
            pl.BlockSpec((tm, Wa), lambda i: (i, 0)),
            pl.BlockSpec((tm, Wb), lambda i: (i, 0)),
            pl.BlockSpec((Wa, D), lambda i: (0, 0)),
            pl.BlockSpec((Wb, D), lambda i: (0, 0)),
            pl.BlockSpec((tm, D), lambda i: (i, 0)),
        ],
        out_specs=pl.BlockSpec((tm, D), lambda i: (i, 0)),
        out_shape=jax.ShapeDtypeStruct((S, D), F32),
        compiler_params=_cp("parallel"),
        name="out_proj",
    )(ya, yb, w[:Wa], w[Wa:], res)


def _mlp_kernel(x_ref, g_ref, w1_ref, w2_ref, o_ref, hn_ref):
    f = pl.program_id(1)

    @pl.when(f == 0)
    def _():
        x = x_ref[...]
        ms = jnp.mean(x * x, axis=-1, keepdims=True)
        hn_ref[...] = (x * lax.rsqrt(ms + NORM_EPS) * g_ref[...]).astype(BF16)
        o_ref[...] = x

    z = jnp.maximum(_dot(hn_ref[...], w1_ref[...]), 0.0)
    o_ref[...] += _dot((z * z).astype(BF16), w2_ref[...])


def _mlp(x, g, w1, w2):
    S, D = x.shape
    Fdim = w1.shape[1]
    tm = _pick(S, (512, 256, 128))
    tf = _pick(Fdim, (512, 256, 128))
    return pl.pallas_call(
        _mlp_kernel,
        grid=(S // tm, Fdim // tf),
        in_specs=[
            pl.BlockSpec((tm, D), lambda i, f: (i, 0)),
            pl.BlockSpec((1, D), lambda i, f: (0, 0)),
            pl.BlockSpec((D, tf), lambda i, f: (0, f)),
            pl.BlockSpec((tf, D), lambda i, f: (f, 0)),
        ],
        out_specs=pl.BlockSpec((tm, D), lambda i, f: (i, 0)),
        out_shape=jax.ShapeDtypeStruct((S, D), F32),
        scratch_shapes=[pltpu.VMEM((tm, D), BF16)],
        compiler_params=_cp("parallel", "arbitrary"),
        name="mlp",
    )(x, g.reshape(1, D), w1, w2)


def _head_norm_rope(x, gain, cosf, sinf):
    ms = jnp.mean(x * x, axis=-1, keepdims=True)
    y = x * lax.rsqrt(ms + NORM_EPS) * gain
    return y * cosf + pltpu.roll(y, NSA_DH // 2, 1) * sinf


def _nsa_prep_kernel(u_ref, gain_ref, cos_ref, sin_ref,
                     q_ref, kc_ref, vc_ref, ks_ref, vs_ref, kw_ref, vw_ref, gate_ref):
    cosf = cos_ref[...]
    sinf = sin_ref[...]
    scale = NSA_DH ** -0.5 * LOG2E
    for h in range(NSA_HEADS):
        sl = slice(h * NSA_DH, (h + 1) * NSA_DH)
        q_ref[:, sl] = (_head_norm_rope(u_ref[:, sl], gain_ref[0:1, :], cosf, sinf) * scale).astype(BF16)
    base = NSA_W
    kc_ref[...] = u_ref[:, base:base + NSA_KV_W].astype(BF16)
    vc_ref[...] = u_ref[:, base + NSA_KV_W:base + 2 * NSA_KV_W].astype(BF16)
    vs_ref[...] = u_ref[:, base + 3 * NSA_KV_W:base + 4 * NSA_KV_W].astype(BF16)
    vw_ref[...] = u_ref[:, base + 5 * NSA_KV_W:base + 6 * NSA_KV_W].astype(BF16)
    for g in range(NSA_KV_HEADS):
        sl = slice(g * NSA_DH, (g + 1) * NSA_DH)
        o_s = base + 2 * NSA_KV_W + g * NSA_DH
        o_w = base + 4 * NSA_KV_W + g * NSA_DH
        ks_ref[:, sl] = _head_norm_rope(u_ref[:, o_s:o_s + NSA_DH], gain_ref[2:3, :], cosf, sinf).astype(BF16)
        kw_ref[:, sl] = _head_norm_rope(u_ref[:, o_w:o_w + NSA_DH], gain_ref[3:4, :], cosf, sinf).astype(BF16)
    gate_ref[...] = _sigmoid(u_ref[:, base + 6 * NSA_KV_W:base + 6 * NSA_KV_W + 128])


def _nsa_prep(u, gain8, cosf, sinf):
    S = u.shape[0]
    tm = _pick(S, (512, 256, 128))
    row = lambda w: pl.BlockSpec((tm, w), lambda i: (i, 0))
    kv = jax.ShapeDtypeStruct((S, NSA_KV_W), BF16)
    return pl.pallas_call(
        _nsa_prep_kernel,
        grid=(S // tm,),
        in_specs=[row(NSA_COLS_PAD), pl.BlockSpec((8, NSA_DH), lambda i: (0, 0)), row(NSA_DH), row(NSA_DH)],
        out_specs=[row(NSA_W)] + [row(NSA_KV_W)] * 6 + [row(128)],
        out_shape=[jax.ShapeDtypeStruct((S, NSA_W), BF16)] + [kv] * 6 + [jax.ShapeDtypeStruct((S, 128), F32)],
        compiler_params=_cp("parallel"),
        name="nsa_prep",
    )(u, gain8, cosf, sinf)


def _nsa_cmp_finish_kernel(ak_ref, av_ref, pek_ref, pev_ref, gain_ref, cos_ref, sin_ref,
                           kcmp_ref, vcmp_ref, buf_ref):
    n = ak_ref.shape[0]
    buf_ref[n:n + 8, :] = jnp.zeros((8, NSA_KV_W), F32)

    def combine(a_ref, pe_ref):
        buf_ref[0:n, :] = a_ref[:, NSA_KV_W:]
        pe = pe_ref[0:1, 0:NSA_KV_W] + pe_ref[1:2, NSA_KV_W:]
        return a_ref[:, 0:NSA_KV_W] + buf_ref[pl.ds(1, n), :] + pe

    kc = combine(ak_ref, pek_ref)
    for g in range(NSA_KV_HEADS):
        sl = slice(g * NSA_DH, (g + 1) * NSA_DH)
        kcmp_ref[:, sl] = _head_norm_rope(kc[:, sl], gain_ref[1:2, :], cos_ref[...], sin_ref[...]).astype(BF16)
    vcmp_ref[...] = combine(av_ref, pev_ref).astype(BF16)


def _nsa_compress(kc, vc, cmp_w, cmp_pe, gain8, ccos, csin):
    S = kc.shape[0]
    n = S // CMP_STRIDE
    half = CMP_LEN // 2
    eye = jnp.eye(NSA_KV_HEADS, dtype=F32)

    def expand(w):
        lo = jnp.einsum('lde,gh->lgdhe', w[:half], eye).reshape(half * NSA_KV_W, NSA_KV_W)
        hi = jnp.einsum('lde,gh->lgdhe', w[half:], eye).reshape(half * NSA_KV_W, NSA_KV_W)
        return jnp.concatenate([lo, hi], axis=1).astype(BF16)

    def expand_pe(pe):
        lo = jnp.broadcast_to(pe[:half, None, :], (half, NSA_KV_HEADS, NSA_DH)).reshape(1, -1)
        hi = jnp.broadcast_to(pe[half:, None, :], (half, NSA_KV_HEADS, NSA_DH)).reshape(1, -1)
        return jnp.concatenate([lo, hi, jnp.zeros((6, lo.shape[1]), F32)], axis=0).astype(BF16)

    wk, wv = expand(cmp_w[0]), expand(cmp_w[1])
    ak = _matmul(kc.reshape(n, CMP_STRIDE * NSA_KV_W), wk)
    av = _matmul(vc.reshape(n, CMP_STRIDE * NSA_KV_W), wv)
    pek = _matmul(expand_pe(cmp_pe[0]), wk)
    pev = _matmul(expand_pe(cmp_pe[1]), wv)
    full = lambda shp: pl.BlockSpec(shp, lambda i: (0, 0))
    return pl.pallas_call(
        _nsa_cmp_finish_kernel,
        grid=(1,),
        in_specs=[full((n, 2 * NSA_KV_W)), full((n, 2 * NSA_KV_W)), full((8, 2 * NSA_KV_W)), full((8, 2 * NSA_KV_W)),
                  full((8, NSA_DH)), full((n, NSA_DH)), full((n, NSA_DH))],
        out_specs=[full((n, NSA_KV_W)), full((n, NSA_KV_W))],
        out_shape=[jax.ShapeDtypeStruct((n, NSA_KV_W), BF16)] * 2,
        scratch_shapes=[pltpu.VMEM((n + 8, NSA_KV_W), F32)],
        compiler_params=_cp("arbitrary"),
        name="nsa_cmp_finish",
    )(ak, av, pek, pev, gain8, ccos, csin)


def _nsa_attn_kernel(q_ref, kcmp_ref, vcmpT_ref, ovlT_ref, ks_ref, vsT_ref, kw_ref, vwT_ref, gate_ref,
                     o_ref, sel_ref, m_ref, l_ref, acc_ref, *, top_n):
    g = pl.program_id(0)
    blk = pl.program_id(1)
    t0 = blk * Q_BLOCK
    rows = NSA_GROUP * Q_BLOCK
    n_cmp = kcmp_ref.shape[0]
    n_slc = ovlT_ref.shape[0]

    qf = q_ref[...].astype(F32)
    qT = jnp.concatenate([qf[:, j * NSA_DH:(j + 1) * NSA_DH].T for j in range(NSA_GROUP)], axis=1).astype(BF16)
    lane = lax.broadcasted_iota(jnp.int32, (1, rows), 1)
    t_lane = t0 + (lane & (Q_BLOCK - 1))

    s = _dot(kcmp_ref[...], qT)
    n_io = lax.broadcasted_iota(jnp.int32, (n_cmp, 1), 0)
    valid = (n_io * CMP_STRIDE + (CMP_LEN - 1)) <= t_lane
    sm = jnp.where(valid, s, NEG_INF)
    mx = jnp.max(sm, axis=0, keepdims=True)
    e = jnp.where(valid, jnp.exp2(sm - mx), 0.0)
    den = jnp.sum(e, axis=0, keepdims=True)
    p = e * jnp.where(den > 0.0, 1.0 / den, 0.0)
    o_c = _dot(vcmpT_ref[...], p.astype(BF16))
    psum = p[:, 0:Q_BLOCK]
    for j in range(1, NSA_GROUP):
        psum = psum + p[:, j * Q_BLOCK:(j + 1) * Q_BLOCK]
    p1, p2, p3 = _split3(psum)
    ovl = ovlT_ref[...]
    imp = _dot(ovl, p1) + _dot(ovl, p2) + _dot(ovl, p3)

    m_io = lax.broadcasted_iota(jnp.int32, (n_slc, 1), 0)
    q_io = lax.broadcasted_iota(jnp.int32, (1, Q_BLOCK), 1)
    cur = jnp.right_shift(t0 + q_io, 6)
    forced = (m_io == 0) | (m_io == cur) | (m_io == cur - 1)
    score = jnp.where(forced, SLC_FORCE, jnp.where(m_io <= cur, imp, -1.0))
    bias = jnp.full((n_slc, Q_BLOCK), NEG_INF, F32)
    for _ in range(top_n):
        best = jnp.max(score, axis=0, keepdims=True)
        idx = jnp.min(jnp.where(score == best, m_io, n_slc), axis=0, keepdims=True)
        pick = m_io == idx
        bias = jnp.where(pick, 0.0, bias)
        score = jnp.where(pick, -3e38, score)
    sel_ref[...] = bias

    m_ref[...] = jnp.full((1, rows), NEG_INF, F32)
    l_ref[...] = jnp.zeros((1, rows), F32)
    acc_ref[...] = jnp.zeros((NSA_DH, rows), F32)
    blocks_per_tile = SLC_TILE // SLC_LEN

    def slc_step(kt, causal):
        k_tile = ks_ref[pl.ds(pl.multiple_of(kt * SLC_TILE, SLC_TILE), SLC_TILE), :]
        s = _dot(k_tile, qT)
        selt = sel_ref[pl.ds(pl.multiple_of(kt * blocks_per_tile, blocks_per_tile), blocks_per_tile), :]
        bias_t = jnp.concatenate(
            [jnp.broadcast_to(selt[b:b + 1, :], (SLC_LEN, Q_BLOCK)) for b in range(blocks_per_tile)], axis=0)
        if causal:
            kpos = kt * SLC_TILE + lax.broadcasted_iota(jnp.int32, (SLC_TILE, Q_BLOCK), 0)
            bias_t = jnp.where(kpos <= t0 + q_io, bias_t, NEG_INF)
        sm = s + jnp.concatenate([bias_t] * NSA_GROUP, axis=1)
        m_old = m_ref[...]
        m_new = jnp.maximum(m_old, jnp.max(sm, axis=0, keepdims=True))
        p = jnp.exp2(sm - m_new)
        alpha = jnp.exp2(m_old - m_new)
        l_ref[...] = alpha * l_ref[...] + jnp.sum(p, axis=0, keepdims=True)
        acc_ref[...] = alpha * acc_ref[...] + _dot(vsT_ref[kt], p.astype(BF16))
        m_ref[...] = m_new

    last_kt = (t0 + Q_BLOCK - 1) // SLC_TILE

    def slc_body(kt, carry):
        slc_step(kt, False)
        return carry

    lax.fori_loop(0, last_kt, slc_body, 0)
    slc_step(last_kt, True)
    o_s = acc_ref[...] * (1.0 / l_ref[...])

    wlen = WIN + Q_BLOCK
    start = pl.multiple_of(jnp.maximum(t0 - WIN, 0), Q_BLOCK)
    s = _dot(kw_ref[pl.ds(start, wlen), :], qT)
    kpos = start + lax.broadcasted_iota(jnp.int32, (wlen, 1), 0)
    d = t_lane - kpos
    mask = (d >= 0) & (d < WIN)
    sm = jnp.where(mask, s, NEG_INF)
    mx = jnp.max(sm, axis=0, keepdims=True)
    e = jnp.where(mask, jnp.exp2(sm - mx), 0.0)
    p = (e * (1.0 / jnp.sum(e, axis=0, keepdims=True))).astype(BF16)
    tile0 = start // Q_BLOCK
    o_w = _dot(vwT_ref[tile0], p[0:Q_BLOCK, :])
    for i in range(1, wlen // Q_BLOCK):
        o_w = o_w + _dot(vwT_ref[tile0 + i], p[i * Q_BLOCK:(i + 1) * Q_BLOCK, :])

    gT = gate_ref[...].T
    for j in range(NSA_GROUP):
        sl = slice(j * Q_BLOCK, (j + 1) * Q_BLOCK)

        def grow(b):
            r0 = b * NSA_HEADS + j
            r1 = b * NSA_HEADS + NSA_GROUP + j
            return jnp.where(g == 0, gT[r0:r0 + 1, :], gT[r1:r1 + 1, :])

        o = grow(0) * o_c[:, sl] + grow(1) * o_s[:, sl] + grow(2) * o_w[:, sl]
        o_ref[:, j * NSA_DH:(j + 1) * NSA_DH] = o.T.astype(BF16)


def _nsa_attention(q, kcmp, vcmp, ks, vs, kw, vw, gate):
    S = q.shape[0]
    n_cmp = kcmp.shape[0]
    n_slc = S // SLC_LEN
    n_qb = S // Q_BLOCK
    top_n = min(SLC_TOPN, n_slc)
    rows = NSA_GROUP * Q_BLOCK

    c0 = np.arange(n_cmp)[None, :] * CMP_STRIDE
    s0 = np.arange(n_slc)[:, None] * SLC_LEN
    ovl = np.clip(np.minimum(c0 + CMP_LEN, s0 + SLC_LEN) - np.maximum(c0, s0), 0, None).astype(np.float32) / CMP_LEN
    ovl[:, n_cmp - 1] = 0.0
    ovlT = jnp.asarray(ovl, dtype=BF16)

    def tiles_T(v, width):
        return v.reshape(S // width, width, NSA_KV_HEADS, NSA_DH).transpose(2, 0, 3, 1)

    vcmpT = vcmp.reshape(n_cmp, NSA_KV_HEADS, NSA_DH).transpose(1, 2, 0)
    vsT = tiles_T(vs, SLC_TILE)
    vwT = tiles_T(vw, Q_BLOCK)

    return pl.pallas_call(
        functools.partial(_nsa_attn_kernel, top_n=top_n),
        grid=(NSA_KV_HEADS, n_qb),
        in_specs=[
            pl.BlockSpec((Q_BLOCK, rows), lambda g, b: (b, g)),
            pl.BlockSpec((n_cmp, NSA_DH), lambda g, b: (0, g)),
            pl.BlockSpec((None, NSA_DH, n_cmp), lambda g, b: (g, 0, 0)),
            pl.BlockSpec((n_slc, n_cmp), lambda g, b: (0, 0)),
            pl.BlockSpec((S, NSA_DH), lambda g, b: (0, g)),
            pl.BlockSpec((None, S // SLC_TILE, NSA_DH, SLC_TILE), lambda g, b: (g, 0, 0, 0)),
            pl.BlockSpec((S, NSA_DH), lambda g, b: (0, g)),
            pl.BlockSpec((None, S // Q_BLOCK, NSA_DH, Q_BLOCK), lambda g, b: (g, 0, 0, 0)),
            pl.BlockSpec((Q_BLOCK, 128), lambda g, b: (b, 0)),
        ],
        out_specs=pl.BlockSpec((Q_BLOCK, rows), lambda g, b: (b, g)),
        out_shape=jax.ShapeDtypeStruct((S, NSA_W), BF16),
        scratch_shapes=[
            pltpu.VMEM((n_slc, Q_BLOCK), F32),
            pltpu.VMEM((1, rows), F32),
            pltpu.VMEM((1, rows), F32),
            pltpu.VMEM((NSA_DH, rows), F32),
        ],
        compiler_params=_cp("arbitrary", "arbitrary"),
        name="nsa_attn",
    )(q, kcmp, vcmpT, ovlT, ks, vsT, kw, vwT, gate)


def _rope_tables(pos):
    inv = ROPE_THETA ** (-(jnp.arange(0, NSA_DH, 2, dtype=F32) / NSA_DH))
    ang = pos[:, None] * inv[None, :]
    c, s = jnp.cos(ang), jnp.sin(ang)
    return jnp.concatenate([c, c], axis=-1), jnp.concatenate([-s, s], axis=-1)


def _nsa_mixer(u, qk_gain, cmp_w, cmp_pe):
    S = u.shape[0]
    gain8 = jnp.concatenate([qk_gain.astype(F32), jnp.zeros((4, NSA_DH), F32)], axis=0)
    cosf, sinf = _rope_tables(jnp.arange(S, dtype=F32))
    q, kc, vc, ks, vs, kw, vw, gate = _nsa_prep(u, gain8, cosf, sinf)
    n = S // CMP_STRIDE
    ccos, csin = _rope_tables((jnp.arange(n) * CMP_STRIDE + CMP_LEN - 1).astype(F32))
    kcmp, vcmp = _nsa_compress(kc, vc, cmp_w, cmp_pe, gain8, ccos, csin)
    return _nsa_attention(q, kcmp, vcmp, ks, vs, kw, vw, gate)


def _seg_sum(x, ones_bd):
    outs = []
    for c in range(x.shape[1] // 128):
        p1, p2, p3 = _split3(x[:, c * 128:(c + 1) * 128])
        outs.append(_dot(p1, ones_bd) + _dot(p2, ones_bd) + _dot(p3, ones_bd))
    return jnp.concatenate(outs, axis=1)


def _rw_prep_kernel(u_ref, prev_ref, mu_ref, w0_ref, w2_ref, a0_ref, a2_ref, g2_ref, kk_ref, ka_ref, rk_ref, bd_ref,
                    r_ref, lw_ref, k_ref, v_ref, a_ref, b_ref, g_ref, bonus_ref, buf_ref):
    i = pl.program_id(0)
    tm = u_ref.shape[0]
    buf_ref[0:8, :] = jnp.where(i == 0, 0.0, prev_ref[...])
    buf_ref[8:8 + tm, :] = u_ref[...]
    u = u_ref[...]
    x = u + (buf_ref[pl.ds(7, tm), :] - u) * mu_ref[...]
    W = RW_W
    r, k, v = x[:, 0:W], x[:, W:2 * W], x[:, 2 * W:3 * W]
    wl = x[:, 3 * W:3 * W + LORA_PAD]
    al = x[:, 3 * W + LORA_PAD:3 * W + 2 * LORA_PAD]
    gl = x[:, 3 * W + 2 * LORA_PAD:]
    wx = w0_ref[...] + _dot(jnp.tanh(wl).astype(BF16), w2_ref[...])
    w = -_softplus(-wx) - 0.5
    a = _sigmoid(a0_ref[...] + _dot(al.astype(BF16), a2_ref[...]))
    g = _dot(_sigmoid(gl).astype(BF16), g2_ref[...])
    bd = bd_ref[...]
    kk = k * kk_ref[...]
    kk = kk / jnp.maximum(jnp.sqrt(_seg_sum(kk * kk, bd)), 1e-12)
    kf = k * (1.0 + (a - 1.0) * ka_ref[...])
    r_ref[...] = r
    lw_ref[...] = -jnp.exp(w)
    k_ref[...] = kf
    v_ref[...] = v
    a_ref[...] = -kk
    b_ref[...] = kk * a
    g_ref[...] = g
    bonus_ref[...] = _seg_sum(r * kf * rk_ref[...], bd) * v


def _rw_prep(u, mu, w0, w2, a0, a2, g2, k_k, k_a, r_k):
    S = u.shape[0]
    tm = _pick(S, (256, 128, 64))
    nb = tm // 8
    W = RW_W
    vec = lambda n: pl.BlockSpec((1, n), lambda i: (0, 0))
    mat = lambda a, b: pl.BlockSpec((a, b), lambda i: (0, 0))
    row = pl.BlockSpec((tm, W), lambda i: (i, 0))
    ones_bd = jnp.asarray(np.kron(np.eye(2), np.ones((RW_DH, RW_DH))), dtype=BF16)
    out = jax.ShapeDtypeStruct((S, W), F32)
    return pl.pallas_call(
        _rw_prep_kernel,
        grid=(S // tm,),
        in_specs=[
            pl.BlockSpec((tm, RW_COLS_PAD), lambda i: (i, 0)),
            pl.BlockSpec((8, RW_COLS_PAD), lambda i: (jnp.maximum(i * nb - 1, 0), 0)),
            vec(RW_COLS_PAD), vec(W), mat(LORA_PAD, W), vec(W), mat(LORA_PAD, W), mat(RW_G_LORA, W),
            vec(W), vec(W), vec(W), mat(128, 128),
        ],
        out_specs=[row] * 8,
        out_shape=[out] * 8,
        scratch_shapes=[pltpu.VMEM((tm + 8, RW_COLS_PAD), F32)],
        compiler_params=_cp("parallel"),
        name="rw_prep",
    )(u, u, mu, w0, w2, a0, a2, g2, k_k, k_a, r_k, ones_bd)


def _bd_rows(x, lane_head):
    lh = lane_head[0:x.shape[0], :]
    return jnp.concatenate([jnp.where(lh == h, x, jnp.zeros_like(x)) for h in range(RW_PACK)], axis=0)


def _diag_pack(full, lane_head):
    out = None
    for h in range(RW_PACK):
        blk = jnp.where(lane_head == h, full[h * RW_DH:(h + 1) * RW_DH, :], 0.0)
        out = blk if out is None else out + blk
    return out


def _rw_chunk_kernel(r_ref, lw_ref, k_ref, v_ref, a_ref, b_ref, lhs_ref, add_ref):
    C = RW_CHUNK
    ti = lax.broadcasted_iota(jnp.int32, (C, C), 0)
    si = lax.broadcasted_iota(jnp.int32, (C, C), 1)
    tri_incl = (si <= ti).astype(F32)
    GW = RW_PACK * RW_DH
    lane_head = lax.broadcasted_iota(jnp.int32, (C, GW), 1) // RW_DH
    lane_pos = lax.broadcasted_iota(jnp.int32, (C, GW), 1) % RW_DH
    row_t = lax.broadcasted_iota(jnp.int32, (C, GW), 0)
    eye_p = (lane_pos == row_t).astype(F32)
    strict = jnp.concatenate([lane_pos < row_t] * 2, axis=1)
    incl = jnp.concatenate([lane_pos <= row_t] * 2, axis=1)

    lw = lw_ref[...]
    cs = _dot_hi(tri_incl, lw)
    cprev = cs - lw
    clast = cs[C - 1:C, :]
    e_neg = jnp.exp(-cs)
    e_end = jnp.exp(clast - cs)
    At = (a_ref[...] * jnp.exp(cprev)).astype(BF16)
    Rt32 = r_ref[...] * jnp.exp(cs)
    Rt = Rt32.astype(BF16)
    Bt = (b_ref[...] * e_neg).astype(BF16)
    Kt = (k_ref[...] * e_neg).astype(BF16)
    Bh = (b_ref[...] * e_end).astype(BF16)
    Kh = (k_ref[...] * e_end).astype(BF16)
    V = v_ref[...].astype(BF16)
    gam = jnp.exp(clast)

    for gi in range(RW_HEADS // RW_PACK):
        sl = slice(gi * GW, (gi + 1) * GW)
        bd = functools.partial(_bd_rows, lane_head=lane_head)
        G = _dot_nt(jnp.concatenate([At[:, sl], Rt[:, sl]], axis=0),
                    jnp.concatenate([bd(Bt[:, sl]), bd(Kt[:, sl])], axis=0))
        top = jnp.where(strict, G[0:C, :], 0.0)
        bot = jnp.where(incl, G[C:2 * C, :], 0.0)
        L, Aak = top[:, 0:GW], top[:, GW:]
        Arb, Ark = bot[:, 0:GW].astype(BF16), bot[:, GW:].astype(BF16)
        T = eye_p + L
        Lb = L.astype(BF16)
        P = _dot(Lb, bd(Lb))
        for _ in range(4):
            Pb = P.astype(BF16)
            TP = _dot(jnp.concatenate([T.astype(BF16), Pb], axis=0), bd(Pb))
            T = T + TP[0:C, :]
            P = TP[C:2 * C, :]
        T = T + _dot(T.astype(BF16), bd(P.astype(BF16)))
        Tb = T.astype(BF16)
        AV = _dot(jnp.concatenate([Aak.astype(BF16), Ark], axis=0), bd(V[:, sl]))
        AU = _dot(Tb, jnp.concatenate([bd(At[:, sl]), bd(AV[0:C, :].astype(BF16))], axis=1))
        Ahat, Uhat = AU[:, 0:GW].astype(BF16), AU[:, GW:].astype(BF16)
        RY = _dot(Arb, jnp.concatenate([bd(Ahat), bd(Uhat)], axis=1))
        lhs_ref[0:C, sl] = Rt32[:, sl] + RY[:, 0:GW]
        add_ref[0:C, sl] = RY[:, GW:] + AV[C:2 * C, :]
        phi_full = _dot_tn(Bh[:, sl], Ahat)
        hh_full = _dot_tn(jnp.concatenate([Bh[:, sl], Kh[:, sl]], axis=0),
                          jnp.concatenate([Uhat, V[:, sl]], axis=0))
        lhs_ref[C:2 * C, sl] = eye_p * gam[:, sl] + _diag_pack(phi_full, lane_head)
        add_ref[C:2 * C, sl] = _diag_pack(hh_full, lane_head)


def _rw_chunks(r, lw, k, v, a, b):
    S = r.shape[0]
    C = RW_CHUNK
    row = pl.BlockSpec((C, RW_W), lambda c: (c, 0))
    out = pl.BlockSpec((2 * C, RW_W), lambda c: (c, 0))
    shp = jax.ShapeDtypeStruct((2 * S, RW_W), F32)
    return pl.pallas_call(
        _rw_chunk_kernel,
        grid=(S // C,),
        in_specs=[row] * 6,
        out_specs=[out, out],
        out_shape=[shp, shp],
        compiler_params=_cp("parallel"),
        name="rw_chunk",
    )(r, lw, k, v, a, b)


def _rw_scan_kernel(lhs_ref, add_ref, bonus_ref, g_ref, lnw_ref, lnb_ref, y_ref, h_ref):
    C = RW_CHUNK

    @pl.when(pl.program_id(0) == 0)
    def _():
        h_ref[...] = jnp.zeros_like(h_ref)

    GW = RW_PACK * RW_DH
    lane_head = lax.broadcasted_iota(jnp.int32, (RW_DH, GW), 1) // RW_DH
    for gi in range(RW_HEADS // RW_PACK):
        gsl = slice(gi * GW, (gi + 1) * GW)
        res = _dot(lhs_ref[:, gsl].astype(BF16), _bd_rows(h_ref[:, gsl].astype(BF16), lane_head)) + add_ref[:, gsl]
        h_ref[:, gsl] = res[C:2 * C, :]
        for hh in range(RW_PACK):
            sl = slice(gi * GW + hh * RW_DH, gi * GW + (hh + 1) * RW_DH)
            y = res[0:C, hh * RW_DH:(hh + 1) * RW_DH]
            mean = jnp.mean(y, axis=-1, keepdims=True)
            yc = y - mean
            var = jnp.mean(yc * yc, axis=-1, keepdims=True)
            yn = yc * lax.rsqrt(var + RW_LNX_EPS) * lnw_ref[:, sl] + lnb_ref[:, sl]
            y_ref[:, sl] = ((yn + bonus_ref[:, sl]) * g_ref[:, sl]).astype(BF16)


def _rw_scan(lhs, add, bonus, g, lnx_w, lnx_b):
    S = bonus.shape[0]
    C = RW_CHUNK
    row = pl.BlockSpec((C, RW_W), lambda c: (c, 0))
    two = pl.BlockSpec((2 * C, RW_W), lambda c: (c, 0))
    vec = pl.BlockSpec((1, RW_W), lambda c: (0, 0))
    return pl.pallas_call(
        _rw_scan_kernel,
        grid=(S // C,),
        in_specs=[two, two, row, row, vec, vec],
        out_specs=row,
        out_shape=jax.ShapeDtypeStruct((S, RW_W), BF16),
        scratch_shapes=[pltpu.VMEM((RW_DH, RW_W), F32)],
        compiler_params=_cp("arbitrary"),
        name="rw_scan",
    )(lhs, add, bonus, g, lnx_w, lnx_b)


def _pad_lora_cols(x, axis):
    W3 = 3 * RW_W
    parts = [lax.slice_in_dim(x, 0, W3, axis=axis),
             lax.slice_in_dim(x, W3, W3 + RW_DECAY_LORA, axis=axis),
             lax.slice_in_dim(x, W3 + RW_DECAY_LORA, W3 + RW_DECAY_LORA + RW_A_LORA, axis=axis),
             lax.slice_in_dim(x, W3 + RW_DECAY_LORA + RW_A_LORA, RW_COLS, axis=axis)]

    def padto(p, n):
        cfg = [(0, 0)] * x.ndim
        cfg[axis] = (0, n - p.shape[axis])
        return jnp.pad(p, cfg)

    return jnp.concatenate([parts[0], padto(parts[1], LORA_PAD), padto(parts[2], LORA_PAD), parts[3]], axis=axis)


def _rwkv_mixer(u, mu, w0, w2, a0, a2, g2, k_k, k_a, r_k, lnx_w, lnx_b):
    row = lambda p: p.reshape(1, -1).astype(F32)
    padrows = lambda m: jnp.pad(m, ((0, LORA_PAD - m.shape[0]), (0, 0))).astype(BF16)
    r, lw, k, v, a, b, g, bonus = _rw_prep(
        u, _pad_lora_cols(row(mu), 1), row(w0), padrows(w2), row(a0), padrows(a2), g2.astype(BF16),
        row(k_k), row(k_a), row(r_k))
    lhs, add = _rw_chunks(r, lw, k, v, a, b)
    return _rw_scan(lhs, add, bonus, g, row(lnx_w), row(lnx_b))


def _lru_kernel(u_ref, cw_ref, cb_ref, wa_ref, ba_ref, wx_ref, bx_ref, sp_ref, y_ref, buf_ref, a_sc, b_sc, h_sc):
    i = pl.program_id(0)
    tm = u_ref.shape[0]

    @pl.when(i == 0)
    def _():
        buf_ref[0:8, :] = jnp.zeros((8, LRU_W), F32)
        h_sc[...] = jnp.zeros_like(h_sc)

    xb = u_ref[:, LRU_W:]
    buf_ref[8:8 + tm, :] = xb
    xc = cb_ref[...] + cw_ref[CONV_W - 1:CONV_W, :] * xb
    for kk in range(CONV_W - 1):
        xc = xc + cw_ref[kk:kk + 1, :] * buf_ref[pl.ds(8 - (CONV_W - 1) + kk, tm), :]
    buf_ref[0:8, :] = xb[tm - 8:tm, :]

    xcb = xc.astype(BF16)
    ra, ri = [], []
    for n in range(LRU_BLOCKS):
        sl = slice(n * LRU_BW, (n + 1) * LRU_BW)
        ra.append(_dot(xcb[:, sl], wa_ref[n]))
        ri.append(_dot(xcb[:, sl], wx_ref[n]))
    rg = _sigmoid(jnp.concatenate(ra, axis=1) + ba_ref[...])
    ig = _sigmoid(jnp.concatenate(ri, axis=1) + bx_ref[...])
    log_a = -LRU_C * rg * sp_ref[...]
    a_sc[...] = jnp.exp(log_a)
    b_sc[...] = jnp.sqrt(1.0 - jnp.exp(2.0 * log_a)) * ig * xc

    def step(t, h):
        h = a_sc[pl.ds(t, 1), :] * h + b_sc[pl.ds(t, 1), :]
        b_sc[pl.ds(t, 1), :] = h
        return h

    h_sc[...] = lax.fori_loop(0, tm, step, h_sc[...], unroll=8)
    gate = u_ref[:, 0:LRU_W]
    gelu = 0.5 * gate * (1.0 + jnp.tanh(0.7978845608028654 * (gate + 0.044715 * gate * gate * gate)))
    y_ref[...] = (b_sc[...] * gelu).astype(BF16)


def _lru_mixer(u, conv_w, conv_b, wa, ba, wx, bx, lam):
    S = u.shape[0]
    tm = _pick(S, (256, 128, 64))
    row = lambda p: p.reshape(1, -1).astype(F32)
    vec = pl.BlockSpec((1, LRU_W), lambda i: (0, 0))
    blk = pl.BlockSpec((LRU_BLOCKS, LRU_BW, LRU_BW), lambda i: (0, 0, 0))
    lamf = lam.astype(F32)
    softplus_neg_lam = row(jnp.maximum(-lamf, 0.0) + jnp.log1p(jnp.exp(-jnp.abs(lamf))))
    cw = jnp.concatenate([conv_w.astype(F32), jnp.zeros((8 - CONV_W, LRU_W), F32)], axis=0)
    return pl.pallas_call(
        _lru_kernel,
        grid=(S // tm,),
        in_specs=[pl.BlockSpec((tm, 2 * LRU_W), lambda i: (i, 0)), pl.BlockSpec((8, LRU_W), lambda i: (0, 0)),
                  vec, blk, vec, blk, vec, vec],
        out_specs=pl.BlockSpec((tm, LRU_W), lambda i: (i, 0)),
        out_shape=jax.ShapeDtypeStruct((S, LRU_W), BF16),
        scratch_shapes=[pltpu.VMEM((tm + 8, LRU_W), F32), pltpu.VMEM((tm, LRU_W), F32),
                        pltpu.VMEM((tm, LRU_W), F32), pltpu.VMEM((1, LRU_W), F32)],
        compiler_params=_cp("arbitrary"),
        name="rglru",
    )(u, cw, row(conv_b), wa.astype(BF16), row(ba), wx.astype(BF16), row(bx), softplus_neg_lam)


def _hgrn_kernel(u_ref, lower_ref, ng_ref, y_ref, st_ref):
    C = HG_CHUNK
    W = HG_KW

    @pl.when(pl.program_id(0) == 0)
    def _():
        st_ref[...] = jnp.zeros_like(st_ref)

    lower = lower_ref[...]
    forget = lower + (1.0 - lower) * _sigmoid(u_ref[:, W:2 * W])
    lf = jnp.log(forget)
    kk = 1.0 - forget
    qx = u_ref[:, 0:W]
    q = qx * _sigmoid(qx)
    v = u_ref[:, 2 * W:3 * W]
    gx = u_ref[:, 3 * W:4 * W]

    ti = lax.broadcasted_iota(jnp.int32, (C, C), 0)
    si = lax.broadcasted_iota(jnp.int32, (C, C), 1)
    b = _dot_hi((si <= ti).astype(F32), lf)
    blast = b[C - 1:C, :]
    q_in = (q * jnp.exp(b)).astype(BF16)
    k_end = (kk * jnp.exp(blast - b)).astype(BF16)
    vb = v.astype(BF16)
    rowi = lax.broadcasted_iota(jnp.int32, (C, W), 0)
    sub_t = lax.broadcasted_iota(jnp.int32, (HG_SUB, C), 0)
    sub_s = lax.broadcasted_iota(jnp.int32, (HG_SUB, C), 1)

    n_sub = C // HG_SUB
    q_sub, k_sub = [], []
    for i in range(n_sub):
        ref = jnp.zeros((1, W), F32) if i == 0 else b[i * HG_SUB - 1:i * HG_SUB, :]
        q_sub.append((q[i * HG_SUB:(i + 1) * HG_SUB, :] * jnp.exp(b[i * HG_SUB:(i + 1) * HG_SUB, :] - ref)).astype(BF16))
        live = rowi < (i + 1) * HG_SUB
        k_sub.append(jnp.where(live, kk * jnp.exp(jnp.where(live, ref - b, 0.0)), 0.0).astype(BF16))

    for h in range(HG_HEADS):
        sl = slice(h * HG_DK, (h + 1) * HG_DK)
        state = st_ref[:, sl]
        o = _dot_nt(q_in[:, sl], state.astype(BF16))
        intra = []
        for i in range(n_sub):
            att = _dot_nt(q_sub[i][:, sl], k_sub[i][:, sl])
            att = jnp.where(sub_s <= sub_t + i * HG_SUB, att, 0.0)
            intra.append(_dot(att.astype(BF16), vb[:, sl]))
        o = o + jnp.concatenate(intra, axis=0)
        st_ref[:, sl] = jnp.exp(blast[:, sl]) * state + _dot_tn(vb[:, sl], k_end[:, sl])
        ms = jnp.mean(o * o, axis=-1, keepdims=True)
        on = o * lax.rsqrt(ms + NORM_EPS) * ng_ref[:, sl]
        gh = gx[:, sl]
        y_ref[:, sl] = (on * (gh * _sigmoid(gh))).astype(BF16)


def _hgrn_mixer(u, lower, norm_g):
    S = u.shape[0]
    C = HG_CHUNK
    vec = pl.BlockSpec((1, HG_KW), lambda c: (0, 0))
    return pl.pallas_call(
        _hgrn_kernel,
        grid=(S // C,),
        in_specs=[pl.BlockSpec((C, 4 * HG_KW), lambda c: (c, 0)), vec, vec],
        out_specs=pl.BlockSpec((C, HG_VW), lambda c: (c, 0)),
        out_shape=jax.ShapeDtypeStruct((S, HG_VW), BF16),
        scratch_shapes=[pltpu.VMEM((HG_DV, HG_KW), F32)],
        compiler_params=_cp("arbitrary"),
        name="hgrn2",
    )(u, lower.reshape(1, -1).astype(F32), norm_g.reshape(1, -1).astype(F32))


def kernel(x, norm_mix, norm_mlp, w_ff1, w_ff2, w_in_a, w_out_a, nsa_qk_gain, nsa_cmp_w, nsa_cmp_pe, rw_mu, rw_w0, rw_w2, rw_a0, rw_a2, rw_g2, rw_k_k, rw_k_a, rw_r_k, rw_lnx_w, rw_lnx_b, w_in_b, w_out_b, lru_conv_w, lru_conv_b, lru_wa, lru_ba, lru_wx, lru_bx, lru_lambda, hg_lb, hg_norm):
    B, S, D = x.shape
    depth = norm_mix.shape[0]
    lb_p = jax.nn.softmax(hg_lb.astype(F32), axis=0)
    lb_cum = jnp.cumsum(lb_p, axis=0)
    hg_lower = lb_cum - lb_cum[0:1]

    outs = []
    for bi in range(B):
        xb = x[bi].astype(F32)
        for layer in range(depth):
            gmix = norm_mix[layer].astype(F32)
            if layer % 2 == 0:
                e = layer // 2
                w_in = w_in_a[e]
                w_nsa = jnp.pad(w_in[:, :NSA_COLS], ((0, 0), (0, NSA_COLS_PAD - NSA_COLS))).astype(BF16)
                w_rw = _pad_lora_cols(w_in[:, NSA_COLS:], 1).astype(BF16)
                y_a = _nsa_mixer(_norm_matmul(xb, gmix, w_nsa), nsa_qk_gain[e], nsa_cmp_w[e], nsa_cmp_pe[e])
                y_b = _rwkv_mixer(_norm_matmul(xb, gmix, w_rw), rw_mu[e], rw_w0[e], rw_w2[e], rw_a0[e], rw_a2[e],
                                  rw_g2[e], rw_k_k[e], rw_k_a[e], rw_r_k[e], rw_lnx_w[e], rw_lnx_b[e])
                xb = _out_proj(y_a, y_b, w_out_a[e].astype(BF16), xb)
            else:
                o = layer // 2
                w_in = w_in_b[o].astype(BF16)
                y_c = _lru_mixer(_norm_matmul(xb, gmix, w_in[:, :2 * LRU_W]), lru_conv_w[o], lru_conv_b[o],
                                 lru_wa[o], lru_ba[o], lru_wx[o], lru_bx[o], lru_lambda[o])
                y_d = _hgrn_mixer(_norm_matmul(xb, gmix, w_in[:, 2 * LRU_W:]), hg_lower[layer], hg_norm[o])
                xb = _out_proj(y_c, y_d, w_out_b[o].astype(BF16), xb)
            xb = _mlp(xb, norm_mlp[layer].astype(F32), w_ff1[layer].astype(BF16), w_ff2[layer].astype(BF16))
        outs.append(xb)
    return jnp.stack(outs, axis=0).astype(x.dtype)
```

```python
import functools

import jax
import jax.numpy as jnp
import numpy as np
from jax import lax
from jax.experimental import pallas as pl
from jax.experimental.pallas import tpu as pltpu

F32 = jnp.float32
BF16 = jnp.bfloat16
HI = lax.Precision.HIGHEST

NORM_EPS = 1e-6
NEG_INF = -1e30
ROPE_THETA = 10000.0
LOG2E = 1.4426950408889634

NSA_HEADS = 8
NSA_KV_HEADS = 2
NSA_GROUP = NSA_HEADS // NSA_KV_HEADS
NSA_DH = 128
CMP_LEN = 32
CMP_STRIDE = 16
SLC_LEN = 64
SLC_TOPN = 16
WIN = 512
Q_BLOCK = 128
SLC_FORCE = 1e4
NSA_W = NSA_HEADS * NSA_DH
NSA_KV_W = NSA_KV_HEADS * NSA_DH
NSA_COLS = NSA_W + 6 * NSA_KV_W + 3 * NSA_HEADS
NSA_COLS_PAD = NSA_W + 6 * NSA_KV_W + 128
SLC_TILE = 512
NSA_QPAIR = 2

RW_HEADS = 16
RW_DH = 64
RW_W = RW_HEADS * RW_DH
RW_DECAY_LORA = 96
RW_A_LORA = 96
RW_G_LORA = 256
RW_LNX_EPS = 64e-5
RW_COLS = 3 * RW_W + RW_DECAY_LORA + RW_A_LORA + RW_G_LORA
LORA_PAD = 128
RW_COLS_PAD = 3 * RW_W + 2 * LORA_PAD + RW_G_LORA
RW_CHUNK = 64
RW_PACK = 4

LRU_W = 1024
LRU_BLOCKS = 8
LRU_BW = LRU_W // LRU_BLOCKS
CONV_W = 4
LRU_C = 8.0

HG_HEADS = 8
HG_DK = 128
HG_DV = 128
HG_KW = HG_HEADS * HG_DK
HG_VW = HG_HEADS * HG_DV
HG_CHUNK = 64
HG_SUB = 16

MXU_WIDTH = 256
VMEM_LIMIT = 56 * 1024 * 1024


def _cp(*sem, flags=None):
    return pltpu.CompilerParams(dimension_semantics=sem, vmem_limit_bytes=VMEM_LIMIT, flags=flags)


def _pick(n, cands):
    for c in cands:
        if n % c == 0:
            return c
    return n


def _sigmoid(x):
    return 1.0 / (1.0 + jnp.exp(-x))


def _softplus(x):
    return jnp.maximum(x, 0.0) + jnp.log(1.0 + jnp.exp(-jnp.abs(x)))


def _split3(x):
    p1 = x.astype(BF16)
    r1 = x - p1.astype(F32)
    p2 = r1.astype(BF16)
    p3 = (r1 - p2.astype(F32)).astype(BF16)
    return p1, p2, p3


def _dot(a, b):
    return jnp.dot(a, b, preferred_element_type=F32)


def _dot_hi(a, b):
    return jnp.dot(a, b, preferred_element_type=F32, precision=HI)


def _dot_nt(a, b, precision=None):
    return lax.dot_general(a, b, (((1,), (1,)), ((), ())), preferred_element_type=F32, precision=precision)


def _dot_tn(a, b, precision=None):
    return lax.dot_general(a, b, (((0,), (0,)), ((), ())), preferred_element_type=F32, precision=precision)


def _norm_matmul_kernel(x_ref, g_ref, w_ref, o_ref, *, tn):
    x = x_ref[...]
    ms = jnp.mean(x * x, axis=-1, keepdims=True)
    hn = (x * lax.rsqrt(ms + NORM_EPS) * g_ref[...]).astype(BF16)
    n = w_ref.shape[1]
    for lo in range(0, n, tn):
        hi = min(lo + tn, n)
        o_ref[:, lo:hi] = _dot(hn, w_ref[:, lo:hi])


def _norm_matmul(x, g, w):
    S, D = x.shape
    N = w.shape[1]
    tm = _pick(S, (512, 256, 128))
    tn = 3 * MXU_WIDTH
    return pl.pallas_call(
        functools.partial(_norm_matmul_kernel, tn=tn),
        grid=(S // tm,),
        in_specs=[
            pl.BlockSpec((tm, D), lambda i: (i, 0)),
            pl.BlockSpec((1, D), lambda i: (0, 0)),
            pl.BlockSpec((D, N), lambda i: (0, 0), pipeline_mode=pl.Buffered(1)),
        ],
        out_specs=pl.BlockSpec((tm, N), lambda i: (i, 0)),
        out_shape=jax.ShapeDtypeStruct((S, N), F32),
        compiler_params=_cp("parallel"),
        name="norm_matmul",
    )(x, g.reshape(1, D), w)


def _matmul_kernel(a_ref, w_ref, o_ref):
    o_ref[...] = _dot(a_ref[...], w_ref[...])


def _matmul(a, w):
    M, K = a.shape
    N = w.shape[1]
    tm = _pick(M, (256, 128, 64, 32, 16, 8))
    return pl.pallas_call(
        _matmul_kernel,
        grid=(M // tm,),
        in_specs=[pl.BlockSpec((tm, K), lambda i: (i, 0)), pl.BlockSpec((K, N), lambda i: (0, 0))],
        out_specs=pl.BlockSpec((tm, N), lambda i: (i, 0)),
        out_shape=jax.ShapeDtypeStruct((M, N), F32),
        compiler_params=_cp("parallel"),
        name="matmul",
    )(a, w)


def _out_proj_kernel(a_ref, b_ref, wa_ref, wb_ref, r_ref, o_ref):
    o_ref[...] = r_ref[...] + _dot(a_ref[...], wa_ref[...]) + _dot(b_ref[...], wb_ref[...])


def _out_proj(ya, yb, w, res):
    S, Wa = ya.shape
    Wb = yb.shape[1]
    D = w.shape[1]
    tm = _pick(S, (512, 256, 128))
    return pl.pallas_call(
        _out_proj_kernel,
        grid=(S // tm,),
        in_specs=[
            pl.BlockSpec((tm, Wa), lambda i: (i, 0)),
            pl.BlockSpec((tm, Wb), lambda i: (i, 0)),
            pl.BlockSpec((Wa, D), lambda i: (0, 0)),
            pl.BlockSpec((Wb, D), lambda i: (0, 0)),
            pl.BlockSpec((tm, D), lambda i: (i, 0)),
        ],
        out_specs=pl.BlockSpec((tm, D), lambda i: (i, 0)),
        out_shape=jax.ShapeDtypeStruct((S, D), F32),
        compiler_params=_cp("parallel"),
        name="out_proj",
    )(ya, yb, w[:Wa], w[Wa:], res)


def _mlp_kernel(x_ref, g_ref, w1_ref, w2_ref, o_ref, hn_ref):
    f = pl.program_id(1)

    @pl.when(f == 0)
    def _():
        x = x_ref[...]
        ms = jnp.mean(x * x, axis=-1, keepdims=True)
        hn_ref[...] = (x * lax.rsqrt(ms + NORM_EPS) * g_ref[...]).astype(BF16)
        o_ref[...] = x

    hn = hn_ref[...]
    half = w1_ref.shape[1] // 2
    zs = [jnp.maximum(_dot(hn, w1_ref[:, c * half:(c + 1) * half]), 0.0) for c in range(2)]
    acc = _dot((zs[0] * zs[0]).astype(BF16), w2_ref[0:half, :])
    acc = acc + _dot((zs[1] * zs[1]).astype(BF16), w2_ref[half:2 * half, :])
    o_ref[...] += acc


def _mlp(x, g, w1, w2):
    S, D = x.shape
    Fdim = w1.shape[1]
    tm = _pick(S, (1024, 512, 256, 128))
    tf = _pick(Fdim, (512, 256, 128))
    return pl.pallas_call(
        _mlp_kernel,
        grid=(S // tm, Fdim // tf),
        in_specs=[
            pl.BlockSpec((tm, D), lambda i, f: (i, 0)),
            pl.BlockSpec((1, D), lambda i, f: (0, 0)),
            pl.BlockSpec((D, tf), lambda i, f: (0, f)),
            pl.BlockSpec((tf, D), lambda i, f: (f, 0)),
        ],
        out_specs=pl.BlockSpec((tm, D), lambda i, f: (i, 0)),
        out_shape=jax.ShapeDtypeStruct((S, D), F32),
        scratch_shapes=[pltpu.VMEM((tm, D), BF16)],
        compiler_params=_cp("parallel", "arbitrary"),
        name="mlp",
    )(x, g.reshape(1, D), w1, w2)


def _head_norm_rope(x, gain, cosf, sinf):
    ms = jnp.mean(x * x, axis=-1, keepdims=True)
    y = x * lax.rsqrt(ms + NORM_EPS) * gain
    return y * cosf + pltpu.roll(y, NSA_DH // 2, 1) * sinf


def _nsa_prep_kernel(u_ref, gain_ref, cos_ref, sin_ref,
                     q_ref, kc_ref, vc_ref, ks_ref, vs_ref, kw_ref, vw_ref, gate_ref):
    cosf = cos_ref[...]
    sinf = sin_ref[...]
    scale = NSA_DH ** -0.5 * LOG2E
    for h in range(NSA_HEADS):
        sl = slice(h * NSA_DH, (h + 1) * NSA_DH)
        q_ref[:, sl] = (_head_norm_rope(u_ref[:, sl], gain_ref[0:1, :], cosf, sinf) * scale).astype(BF16)
    base = NSA_W
    kc_ref[...] = u_ref[:, base:base + NSA_KV_W].astype(BF16)
    vc_ref[...] = u_ref[:, base + NSA_KV_W:base + 2 * NSA_KV_W].astype(BF16)
    vs_ref[...] = u_ref[:, base + 3 * NSA_KV_W:base + 4 * NSA_KV_W].astype(BF16)
    vw_ref[...] = u_ref[:, base + 5 * NSA_KV_W:base + 6 * NSA_KV_W].astype(BF16)
    for g in range(NSA_KV_HEADS):
        sl = slice(g * NSA_DH, (g + 1) * NSA_DH)
        o_s = base + 2 * NSA_KV_W + g * NSA_DH
        o_w = base + 4 * NSA_KV_W + g * NSA_DH
        ks_ref[:, sl] = _head_norm_rope(u_ref[:, o_s:o_s + NSA_DH], gain_ref[2:3, :], cosf, sinf).astype(BF16)
        kw_ref[:, sl] = _head_norm_rope(u_ref[:, o_w:o_w + NSA_DH], gain_ref[3:4, :], cosf, sinf).astype(BF16)
    gate_ref[...] = _sigmoid(u_ref[:, base + 6 * NSA_KV_W:base + 6 * NSA_KV_W + 128])


def _nsa_prep(u, gain8, cosf, sinf):
    S = u.shape[0]
    tm = _pick(S, (512, 256, 128))
    row = lambda w: pl.BlockSpec((tm, w), lambda i: (i, 0))
    kv = jax.ShapeDtypeStruct((S, NSA_KV_W), BF16)
    return pl.pallas_call(
        _nsa_prep_kernel,
        grid=(S // tm,),
        in_specs=[row(NSA_COLS_PAD), pl.BlockSpec((8, NSA_DH), lambda i: (0, 0)), row(NSA_DH), row(NSA_DH)],
        out_specs=[row(NSA_W)] + [row(NSA_KV_W)] * 6 + [row(128)],
        out_shape=[jax.ShapeDtypeStruct((S, NSA_W), BF16)] + [kv] * 6 + [jax.ShapeDtypeStruct((S, 128), F32)],
        compiler_params=_cp("parallel"),
        name="nsa_prep",
    )(u, gain8, cosf, sinf)


def _nsa_cmp_finish_kernel(ak_ref, av_ref, pek_ref, pev_ref, gain_ref, cos_ref, sin_ref,
                           kcmp_ref, vcmp_ref, buf_ref):
    n = ak_ref.shape[0]
    buf_ref[n:n + 8, :] = jnp.zeros((8, NSA_KV_W), F32)

    def combine(a_ref, pe_ref):
        buf_ref[0:n, :] = a_ref[:, NSA_KV_W:]
        pe = pe_ref[0:1, 0:NSA_KV_W] + pe_ref[1:2, NSA_KV_W:]
        return a_ref[:, 0:NSA_KV_W] + buf_ref[pl.ds(1, n), :] + pe

    kc = combine(ak_ref, pek_ref)
    for g in range(NSA_KV_HEADS):
        sl = slice(g * NSA_DH, (g + 1) * NSA_DH)
        kcmp_ref[:, sl] = _head_norm_rope(kc[:, sl], gain_ref[1:2, :], cos_ref[...], sin_ref[...]).astype(BF16)
    vcmp_ref[...] = combine(av_ref, pev_ref).astype(BF16)


def _nsa_compress(kc, vc, cmp_w, cmp_pe, gain8, ccos, csin):
    S = kc.shape[0]
    n = S // CMP_STRIDE
    half = CMP_LEN // 2
    eye = jnp.eye(NSA_KV_HEADS, dtype=F32)

    def expand(w):
        lo = jnp.einsum('lde,gh->lgdhe', w[:half], eye).reshape(half * NSA_KV_W, NSA_KV_W)
        hi = jnp.einsum('lde,gh->lgdhe', w[half:], eye).reshape(half * NSA_KV_W, NSA_KV_W)
        return jnp.concatenate([lo, hi], axis=1).astype(BF16)

    def expand_pe(pe):
        lo = jnp.broadcast_to(pe[:half, None, :], (half, NSA_KV_HEADS, NSA_DH)).reshape(1, -1)
        hi = jnp.broadcast_to(pe[half:, None, :], (half, NSA_KV_HEADS, NSA_DH)).reshape(1, -1)
        return jnp.concatenate([lo, hi, jnp.zeros((6, lo.shape[1]), F32)], axis=0).astype(BF16)

    wk, wv = expand(cmp_w[0]), expand(cmp_w[1])
    ak = _matmul(kc.reshape(n, CMP_STRIDE * NSA_KV_W), wk)
    av = _matmul(vc.reshape(n, CMP_STRIDE * NSA_KV_W), wv)
    pek = _matmul(expand_pe(cmp_pe[0]), wk)
    pev = _matmul(expand_pe(cmp_pe[1]), wv)
    full = lambda shp: pl.BlockSpec(shp, lambda i: (0, 0))
    return pl.pallas_call(
        _nsa_cmp_finish_kernel,
        grid=(1,),
        in_specs=[full((n, 2 * NSA_KV_W)), full((n, 2 * NSA_KV_W)), full((8, 2 * NSA_KV_W)), full((8, 2 * NSA_KV_W)),
                  full((8, NSA_DH)), full((n, NSA_DH)), full((n, NSA_DH))],
        out_specs=[full((n, NSA_KV_W)), full((n, NSA_KV_W))],
        out_shape=[jax.ShapeDtypeStruct((n, NSA_KV_W), BF16)] * 2,
        scratch_shapes=[pltpu.VMEM((n + 8, NSA_KV_W), F32)],
        compiler_params=_cp("arbitrary"),
        name="nsa_cmp_finish",
    )(ak, av, pek, pev, gain8, ccos, csin)


def _nsa_attn_kernel(q_ref, kcmp_ref, vcmpT_ref, ks_ref, vsT_ref, kw_ref, vwT_ref, gate_ref, khot_ref,
                     o_ref, qT_ref, psum_ref, sel_ref, oc_ref, m_ref, l_ref, acc_ref, s_ref, *, top_n):
    g = pl.program_id(0)
    pb = pl.program_id(1)
    rows = NSA_GROUP * Q_BLOCK
    n_cmp = kcmp_ref.shape[0]
    n_slc = sel_ref.shape[1]
    streams = range(NSA_QPAIR)
    t0s = [(pb * NSA_QPAIR + i) * Q_BLOCK for i in streams]

    lane = lax.broadcasted_iota(jnp.int32, (1, rows), 1)
    q_io = lax.broadcasted_iota(jnp.int32, (1, Q_BLOCK), 1)
    n_io = lax.broadcasted_iota(jnp.int32, (n_cmp, 1), 0)
    m_io = lax.broadcasted_iota(jnp.int32, (n_slc, 1), 0)
    for i in streams:
        t0 = t0s[i]
        t_lane = t0 + (lane & (Q_BLOCK - 1))
        qf = q_ref[i * Q_BLOCK:(i + 1) * Q_BLOCK, :].astype(F32)
        qT = jnp.concatenate([qf[:, j * NSA_DH:(j + 1) * NSA_DH].T for j in range(NSA_GROUP)], axis=1).astype(BF16)
        qT_ref[i, 0:NSA_DH, :] = qT
        qT_ref[i, NSA_DH:2 * NSA_DH, :] = jnp.zeros((NSA_DH, rows), BF16)

        s = _dot(kcmp_ref[...], qT)
        valid = (n_io * CMP_STRIDE + (CMP_LEN - 1)) <= t_lane
        sm = jnp.where(valid, s, NEG_INF)
        mx = jnp.max(sm, axis=0, keepdims=True)
        e = jnp.where(valid, jnp.exp2(sm - mx), 0.0)
        den = jnp.sum(e, axis=0, keepdims=True)
        p = e * jnp.where(den > 0.0, 1.0 / den, 0.0)
        oc_ref[i] = _dot(vcmpT_ref[...], p.astype(BF16))
        psum = p[:, 0:Q_BLOCK]
        for j in range(1, NSA_GROUP):
            psum = psum + p[:, j * Q_BLOCK:(j + 1) * Q_BLOCK]
        psum_ref[i, 0:8, :] = jnp.zeros((8, Q_BLOCK), F32)
        psum_ref[i, 8:8 + n_cmp, :] = psum
        ratio = SLC_LEN // CMP_STRIDE
        imp = 0.5 * (psum_ref[i, pl.ds(7, n_slc, stride=ratio), :] + psum_ref[i, pl.ds(8 + ratio - 1, n_slc, stride=ratio), :])
        for c in range(ratio - 1):
            imp = imp + psum_ref[i, pl.ds(8 + c, n_slc, stride=ratio), :]

        cur = jnp.right_shift(t0 + q_io, 6)
        forced = (m_io == 0) | (m_io == cur) | (m_io == cur - 1)
        score = jnp.where(forced, SLC_FORCE, jnp.where(m_io <= cur, imp, -1.0))
        bias = jnp.full((n_slc, Q_BLOCK), NEG_INF, F32)
        for _ in range(top_n):
            best = jnp.max(score, axis=0, keepdims=True)
            idx = jnp.min(jnp.where(score == best, m_io, n_slc), axis=0, keepdims=True)
            pick = m_io == idx
            bias = jnp.where(pick, 0.0, bias)
            score = jnp.where(pick, -3e38, score)
        sel_ref[i] = bias

    m_ref[...] = jnp.full(m_ref.shape, NEG_INF, F32)
    l_ref[...] = jnp.zeros(l_ref.shape, F32)
    acc_ref[...] = jnp.zeros(acc_ref.shape, F32)
    blocks_per_tile = SLC_TILE // SLC_LEN

    sub = SLC_TILE // 4

    def scores_into(kt, slot):
        k_tile = ks_ref[pl.ds(pl.multiple_of(kt * SLC_TILE, SLC_TILE), SLC_TILE), :]
        k_aug = jnp.concatenate([k_tile, khot_ref[slot]], axis=1)
        band = NSA_DH + 16 * slot
        for i in streams:
            selt = sel_ref[i, pl.ds(pl.multiple_of(kt * blocks_per_tile, blocks_per_tile), blocks_per_tile), :]
            rows8 = jnp.concatenate([selt] * NSA_GROUP, axis=1)
            qT_ref[i, band:band + 16, :] = jnp.concatenate([rows8, jnp.zeros_like(rows8)], axis=0).astype(BF16)
        for i in streams:
            for c in range(4):
                s_ref[i, slot, c * sub:(c + 1) * sub, :] = _dot(k_aug[c * sub:(c + 1) * sub, :], qT_ref[i])

    def absorb(kt, slot, causal):
        v_tile = vsT_ref[kt]
        for i in streams:
            sm = s_ref[i, slot]
            if causal:
                kpos = kt * SLC_TILE + lax.broadcasted_iota(jnp.int32, (SLC_TILE, rows), 0)
                sm = jnp.where(kpos <= t0s[i] + (lane & (Q_BLOCK - 1)), sm, NEG_INF)
            m_old = m_ref[i]
            m_new = jnp.maximum(m_old, jnp.max(sm, axis=0, keepdims=True))
            p = jnp.exp2(sm - m_new)
            alpha = jnp.exp2(m_old - m_new)
            l_ref[i] = alpha * l_ref[i] + jnp.sum(p, axis=0, keepdims=True)
            acc_ref[i] = alpha * acc_ref[i] + _dot(v_tile, p.astype(BF16))
            m_ref[i] = m_new

    last_kt = (t0s[0] + Q_BLOCK - 1) // SLC_TILE
    scores_into(0, 0)

    def slc_pair(kk, carry):
        kt = 2 * kk
        scores_into(kt + 1, 1)
        absorb(kt, 0, False)
        scores_into(kt + 2, 0)
        absorb(kt + 1, 1, False)
        return carry

    lax.fori_loop(0, last_kt // 2, slc_pair, 0)

    @pl.when(last_kt % 2 == 0)
    def _():
        absorb(last_kt, 0, True)

    @pl.when(last_kt % 2 == 1)
    def _():
        scores_into(last_kt, 1)
        absorb(last_kt - 1, 0, False)
        absorb(last_kt, 1, True)

    gT = gate_ref[...].T
    wlen = WIN + Q_BLOCK
    for i in streams:
        t0 = t0s[i]
        t_lane = t0 + (lane & (Q_BLOCK - 1))
        o_s = acc_ref[i] * (1.0 / l_ref[i])

        start = pl.multiple_of(jnp.maximum(t0 - WIN, 0), Q_BLOCK)
        s = _dot(kw_ref[pl.ds(start, wlen), :], qT_ref[i, 0:NSA_DH, :])
        kpos = start + lax.broadcasted_iota(jnp.int32, (wlen, 1), 0)
        d = t_lane - kpos
        mask = (d >= 0) & (d < WIN)
        sm = jnp.where(mask, s, NEG_INF)
        mx = jnp.max(sm, axis=0, keepdims=True)
        e = jnp.where(mask, jnp.exp2(sm - mx), 0.0)
        p = (e * (1.0 / jnp.sum(e, axis=0, keepdims=True))).astype(BF16)
        tile0 = start // Q_BLOCK
        o_w = _dot(vwT_ref[tile0], p[0:Q_BLOCK, :])
        for w in range(1, wlen // Q_BLOCK):
            o_w = o_w + _dot(vwT_ref[tile0 + w], p[w * Q_BLOCK:(w + 1) * Q_BLOCK, :])

        for j in range(NSA_GROUP):
            sl = slice(j * Q_BLOCK, (j + 1) * Q_BLOCK)

            def grow(b):
                r0 = b * NSA_HEADS + j
                r1 = b * NSA_HEADS + NSA_GROUP + j
                cols = slice(i * Q_BLOCK, (i + 1) * Q_BLOCK)
                return jnp.where(g == 0, gT[r0:r0 + 1, cols], gT[r1:r1 + 1, cols])

            o = grow(0) * oc_ref[i, :, sl] + grow(1) * o_s[:, sl] + grow(2) * o_w[:, sl]
            o_ref[i * Q_BLOCK:(i + 1) * Q_BLOCK, j * NSA_DH:(j + 1) * NSA_DH] = o.T.astype(BF16)


def _nsa_attention(q, kcmp, vcmp, ks, vs, kw, vw, gate):
    S = q.shape[0]
    n_cmp = kcmp.shape[0]
    n_slc = S // SLC_LEN
    n_qb = S // Q_BLOCK
    top_n = min(SLC_TOPN, n_slc)
    rows = NSA_GROUP * Q_BLOCK

    nq = NSA_QPAIR
    assert n_qb % nq == 0 and SLC_TILE % (nq * Q_BLOCK) == 0 and n_cmp * CMP_STRIDE == n_slc * SLC_LEN

    def tiles_T(v, width):
        return v.reshape(S // width, width, NSA_KV_HEADS, NSA_DH).transpose(2, 0, 3, 1)

    blk_of_key = np.arange(SLC_TILE)[:, None] // SLC_LEN
    khot = jnp.asarray(np.stack([blk_of_key + 16 * s == np.arange(NSA_DH)[None, :] for s in range(2)]), dtype=BF16)
    vcmpT = vcmp.reshape(n_cmp, NSA_KV_HEADS, NSA_DH).transpose(1, 2, 0)
    vsT = tiles_T(vs, SLC_TILE)
    vwT = tiles_T(vw, Q_BLOCK)

    return pl.pallas_call(
        functools.partial(_nsa_attn_kernel, top_n=top_n),
        grid=(NSA_KV_HEADS, n_qb // nq),
        in_specs=[
            pl.BlockSpec((nq * Q_BLOCK, rows), lambda g, b: (b, g)),
            pl.BlockSpec((n_cmp, NSA_DH), lambda g, b: (0, g)),
            pl.BlockSpec((None, NSA_DH, n_cmp), lambda g, b: (g, 0, 0)),
            pl.BlockSpec((S, NSA_DH), lambda g, b: (0, g)),
            pl.BlockSpec((None, S // SLC_TILE, NSA_DH, SLC_TILE), lambda g, b: (g, 0, 0, 0)),
            pl.BlockSpec((S, NSA_DH), lambda g, b: (0, g)),
            pl.BlockSpec((None, S // Q_BLOCK, NSA_DH, Q_BLOCK), lambda g, b: (g, 0, 0, 0)),
            pl.BlockSpec((nq * Q_BLOCK, 128), lambda g, b: (b, 0)),
            pl.BlockSpec((2, SLC_TILE, NSA_DH), lambda g, b: (0, 0, 0)),
        ],
        out_specs=pl.BlockSpec((nq * Q_BLOCK, rows), lambda g, b: (b, g)),
        out_shape=jax.ShapeDtypeStruct((S, NSA_W), BF16),
        scratch_shapes=[
            pltpu.VMEM((nq, 2 * NSA_DH, rows), BF16),
            pltpu.VMEM((nq, n_cmp + 8, Q_BLOCK), F32),
            pltpu.VMEM((nq, n_slc, Q_BLOCK), F32),
            pltpu.VMEM((nq, NSA_DH, rows), F32),
            pltpu.VMEM((nq, 1, rows), F32),
            pltpu.VMEM((nq, 1, rows), F32),
            pltpu.VMEM((nq, NSA_DH, rows), F32),
            pltpu.VMEM((nq, 2, SLC_TILE, rows), F32),
        ],
        compiler_params=_cp("arbitrary", "arbitrary"),
        name="nsa_attn",
    )(q, kcmp, vcmpT, ks, vsT, kw, vwT, gate, khot)


def _rope_tables(pos):
    inv = ROPE_THETA ** (-(jnp.arange(0, NSA_DH, 2, dtype=F32) / NSA_DH))
    ang = pos[:, None] * inv[None, :]
    c, s = jnp.cos(ang), jnp.sin(ang)
    return jnp.concatenate([c, c], axis=-1), jnp.concatenate([-s, s], axis=-1)


def _nsa_mixer(u, qk_gain, cmp_w, cmp_pe):
    S = u.shape[0]
    gain8 = jnp.concatenate([qk_gain.astype(F32), jnp.zeros((4, NSA_DH), F32)], axis=0)
    cosf, sinf = _rope_tables(jnp.arange(S, dtype=F32))
    q, kc, vc, ks, vs, kw, vw, gate = _nsa_prep(u, gain8, cosf, sinf)
    n = S // CMP_STRIDE
    ccos, csin = _rope_tables((jnp.arange(n) * CMP_STRIDE + CMP_LEN - 1).astype(F32))
    kcmp, vcmp = _nsa_compress(kc, vc, cmp_w, cmp_pe, gain8, ccos, csin)
    return _nsa_attention(q, kcmp, vcmp, ks, vs, kw, vw, gate)


def _seg_sum(x, ones_bd):
    outs = []
    for c in range(x.shape[1] // 128):
        p1, p2, p3 = _split3(x[:, c * 128:(c + 1) * 128])
        outs.append(_dot(p1, ones_bd) + _dot(p2, ones_bd) + _dot(p3, ones_bd))
    return jnp.concatenate(outs, axis=1)


def _rw_prep_kernel(u_ref, prev_ref, mu_ref, w0_ref, w2_ref, a0_ref, a2_ref, g2_ref, kk_ref, ka_ref, rk_ref, bd_ref,
                    r_ref, lw_ref, k_ref, v_ref, a_ref, b_ref, g_ref, bonus_ref, buf_ref):
    i = pl.program_id(0)
    tm = u_ref.shape[0]
    buf_ref[0:8, :] = jnp.where(i == 0, 0.0, prev_ref[...])
    buf_ref[8:8 + tm, :] = u_ref[...]
    u = u_ref[...]
    x = u + (buf_ref[pl.ds(7, tm), :] - u) * mu_ref[...]
    W = RW_W
    r, k, v = x[:, 0:W], x[:, W:2 * W], x[:, 2 * W:3 * W]
    wl = x[:, 3 * W:3 * W + LORA_PAD]
    al = x[:, 3 * W + LORA_PAD:3 * W + 2 * LORA_PAD]
    gl = x[:, 3 * W + 2 * LORA_PAD:]
    wx = w0_ref[...] + _dot(jnp.tanh(wl).astype(BF16), w2_ref[...])
    w = -_softplus(-wx) - 0.5
    a = _sigmoid(a0_ref[...] + _dot(al.astype(BF16), a2_ref[...]))
    g = _dot(_sigmoid(gl).astype(BF16), g2_ref[...])
    bd = bd_ref[...]
    kk = k * kk_ref[...]
    kk = kk / jnp.maximum(jnp.sqrt(_seg_sum(kk * kk, bd)), 1e-12)
    kf = k * (1.0 + (a - 1.0) * ka_ref[...])
    r_ref[...] = r
    lw_ref[...] = -jnp.exp(w)
    k_ref[...] = kf
    v_ref[...] = v
    a_ref[...] = -kk
    b_ref[...] = kk * a
    g_ref[...] = g
    bonus_ref[...] = _seg_sum(r * kf * rk_ref[...], bd) * v


def _rw_prep(u, mu, w0, w2, a0, a2, g2, k_k, k_a, r_k):
    S = u.shape[0]
    tm = _pick(S, (256, 128, 64))
    nb = tm // 8
    W = RW_W
    vec = lambda n: pl.BlockSpec((1, n), lambda i: (0, 0))
    mat = lambda a, b: pl.BlockSpec((a, b), lambda i: (0, 0))
    row = pl.BlockSpec((tm, W), lambda i: (i, 0))
    ones_bd = jnp.asarray(np.kron(np.eye(2), np.ones((RW_DH, RW_DH))), dtype=BF16)
    out = jax.ShapeDtypeStruct((S, W), F32)
    return pl.pallas_call(
        _rw_prep_kernel,
        grid=(S // tm,),
        in_specs=[
            pl.BlockSpec((tm, RW_COLS_PAD), lambda i: (i, 0)),
            pl.BlockSpec((8, RW_COLS_PAD), lambda i: (jnp.maximum(i * nb - 1, 0), 0)),
            vec(RW_COLS_PAD), vec(W), mat(LORA_PAD, W), vec(W), mat(LORA_PAD, W), mat(RW_G_LORA, W),
            vec(W), vec(W), vec(W), mat(128, 128),
        ],
        out_specs=[row] * 8,
        out_shape=[out] * 8,
        scratch_shapes=[pltpu.VMEM((tm + 8, RW_COLS_PAD), F32)],
        compiler_params=_cp("parallel"),
        name="rw_prep",
    )(u, u, mu, w0, w2, a0, a2, g2, k_k, k_a, r_k, ones_bd)


def _bd_rows(x, lane_head):
    lh = lane_head[0:x.shape[0], :]
    return jnp.concatenate([jnp.where(lh == h, x, jnp.zeros_like(x)) for h in range(RW_PACK)], axis=0)


def _diag_pack(full, lane_head):
    out = None
    for h in range(RW_PACK):
        blk = jnp.where(lane_head == h, full[h * RW_DH:(h + 1) * RW_DH, :], 0.0)
        out = blk if out is None else out + blk
    return out


def _rw_chunk_kernel(r_ref, lw_ref, k_ref, v_ref, a_ref, b_ref, lhs_ref, add_ref):
    C = RW_CHUNK
    ti = lax.broadcasted_iota(jnp.int32, (C, C), 0)
    si = lax.broadcasted_iota(jnp.int32, (C, C), 1)
    tri_incl = (si <= ti).astype(F32)
    GW = RW_PACK * RW_DH
    lane_head = lax.broadcasted_iota(jnp.int32, (C, GW), 1) // RW_DH
    lane_pos = lax.broadcasted_iota(jnp.int32, (C, GW), 1) % RW_DH
    row_t = lax.broadcasted_iota(jnp.int32, (C, GW), 0)
    eye_p = (lane_pos == row_t).astype(F32)
    strict = jnp.concatenate([lane_pos < row_t] * 2, axis=1)
    incl = jnp.concatenate([lane_pos <= row_t] * 2, axis=1)

    lw = lw_ref[...]
    cs = _dot_hi(tri_incl, lw)
    cprev = cs - lw
    clast = cs[C - 1:C, :]
    e_neg = jnp.exp(-cs)
    e_end = jnp.exp(clast - cs)
    At = (a_ref[...] * jnp.exp(cprev)).astype(BF16)
    Rt32 = r_ref[...] * jnp.exp(cs)
    Rt = Rt32.astype(BF16)
    Bt = (b_ref[...] * e_neg).astype(BF16)
    Kt = (k_ref[...] * e_neg).astype(BF16)
    Bh = (b_ref[...] * e_end).astype(BF16)
    Kh = (k_ref[...] * e_end).astype(BF16)
    V = v_ref[...].astype(BF16)
    gam = jnp.exp(clast)

    groups = range(RW_HEADS // RW_PACK)
    sls = [slice(gi * GW, (gi + 1) * GW) for gi in groups]
    bd = functools.partial(_bd_rows, lane_head=lane_head)
    G = [_dot_nt(jnp.concatenate([At[:, sl], Rt[:, sl]], axis=0),
                 jnp.concatenate([bd(Bt[:, sl]), bd(Kt[:, sl])], axis=0)) for sl in sls]
    top = [jnp.where(strict, g_[0:C, :], 0.0) for g_ in G]
    bot = [jnp.where(incl, g_[C:2 * C, :], 0.0).astype(BF16) for g_ in G]
    L = [t_[:, 0:GW] for t_ in top]
    AV = [_dot(jnp.concatenate([top[gi][:, GW:].astype(BF16), bot[gi][:, GW:]], axis=0), bd(V[:, sls[gi]])) for gi in groups]
    T = [eye_p + l_ for l_ in L]
    P = [_dot(l_.astype(BF16), bd(l_.astype(BF16))) for l_ in L]
    for _ in range(4):
        TP = [_dot(jnp.concatenate([T[gi].astype(BF16), P[gi].astype(BF16)], axis=0), bd(P[gi].astype(BF16)))
              for gi in groups]
        T = [T[gi] + TP[gi][0:C, :] for gi in groups]
        P = [tp[C:2 * C, :] for tp in TP]
    T = [T[gi] + _dot(T[gi].astype(BF16), bd(P[gi].astype(BF16))) for gi in groups]
    AU = [_dot(T[gi].astype(BF16), jnp.concatenate([bd(At[:, sls[gi]]), bd(AV[gi][0:C, :].astype(BF16))], axis=1))
          for gi in groups]
    Ahat = [au[:, 0:GW].astype(BF16) for au in AU]
    Uhat = [au[:, GW:].astype(BF16) for au in AU]
    RY = [_dot(bot[gi][:, 0:GW], jnp.concatenate([bd(Ahat[gi]), bd(Uhat[gi])], axis=1)) for gi in groups]
    phi_full = [_dot_tn(Bh[:, sls[gi]], Ahat[gi]) for gi in groups]
    hh_full = [_dot_tn(jnp.concatenate([Bh[:, sls[gi]], Kh[:, sls[gi]]], axis=0),
                       jnp.concatenate([Uhat[gi], V[:, sls[gi]]], axis=0)) for gi in groups]
    for gi in groups:
        sl = sls[gi]
        lhs_ref[0:C, sl] = Rt32[:, sl] + RY[gi][:, 0:GW]
        add_ref[0:C, sl] = RY[gi][:, GW:] + AV[gi][C:2 * C, :]
        lhs_ref[C:2 * C, sl] = eye_p * gam[:, sl] + _diag_pack(phi_full[gi], lane_head)
        add_ref[C:2 * C, sl] = _diag_pack(hh_full[gi], lane_head)


def _rw_chunks(r, lw, k, v, a, b):
    S = r.shape[0]
    C = RW_CHUNK
    row = pl.BlockSpec((C, RW_W), lambda c: (c, 0))
    out = pl.BlockSpec((2 * C, RW_W), lambda c: (c, 0))
    shp = jax.ShapeDtypeStruct((2 * S, RW_W), F32)
    return pl.pallas_call(
        _rw_chunk_kernel,
        grid=(S // C,),
        in_specs=[row] * 6,
        out_specs=[out, out],
        out_shape=[shp, shp],
        compiler_params=_cp("parallel"),
        name="rw_chunk",
    )(r, lw, k, v, a, b)


def _rw_scan_kernel(lhs_ref, add_ref, bonus_ref, g_ref, lnw_ref, lnb_ref, y_ref, h_ref):
    C = RW_CHUNK

    @pl.when(pl.program_id(0) == 0)
    def _():
        h_ref[...] = jnp.zeros_like(h_ref)

    GW = RW_PACK * RW_DH
    lane_head = lax.broadcasted_iota(jnp.int32, (RW_DH, GW), 1) // RW_DH
    groups = range(RW_HEADS // RW_PACK)
    gsls = [slice(gi * GW, (gi + 1) * GW) for gi in groups]
    allres = [_dot(lhs_ref[:, gsl].astype(BF16), _bd_rows(h_ref[:, gsl].astype(BF16), lane_head)) + add_ref[:, gsl]
              for gsl in gsls]
    for gi in groups:
        res = allres[gi]
        h_ref[:, gsls[gi]] = res[C:2 * C, :]
        for hh in range(RW_PACK):
            sl = slice(gi * GW + hh * RW_DH, gi * GW + (hh + 1) * RW_DH)
            y = res[0:C, hh * RW_DH:(hh + 1) * RW_DH]
            mean = jnp.mean(y, axis=-1, keepdims=True)
            yc = y - mean
            var = jnp.mean(yc * yc, axis=-1, keepdims=True)
            yn = yc * lax.rsqrt(var + RW_LNX_EPS) * lnw_ref[:, sl] + lnb_ref[:, sl]
            y_ref[:, sl] = ((yn + bonus_ref[:, sl]) * g_ref[:, sl]).astype(BF16)


def _rw_scan(lhs, add, bonus, g, lnx_w, lnx_b):
    S = bonus.shape[0]
    C = RW_CHUNK
    row = pl.BlockSpec((C, RW_W), lambda c: (c, 0))
    two = pl.BlockSpec((2 * C, RW_W), lambda c: (c, 0))
    vec = pl.BlockSpec((1, RW_W), lambda c: (0, 0))
    return pl.pallas_call(
        _rw_scan_kernel,
        grid=(S // C,),
        in_specs=[two, two, row, row, vec, vec],
        out_specs=row,
        out_shape=jax.ShapeDtypeStruct((S, RW_W), BF16),
        scratch_shapes=[pltpu.VMEM((RW_DH, RW_W), F32)],
        compiler_params=_cp("arbitrary"),
        name="rw_scan",
    )(lhs, add, bonus, g, lnx_w, lnx_b)


def _pad_lora_cols(x, axis):
    W3 = 3 * RW_W
    parts = [lax.slice_in_dim(x, 0, W3, axis=axis),
             lax.slice_in_dim(x, W3, W3 + RW_DECAY_LORA, axis=axis),
             lax.slice_in_dim(x, W3 + RW_DECAY_LORA, W3 + RW_DECAY_LORA + RW_A_LORA, axis=axis),
             lax.slice_in_dim(x, W3 + RW_DECAY_LORA + RW_A_LORA, RW_COLS, axis=axis)]

    def padto(p, n):
        cfg = [(0, 0)] * x.ndim
        cfg[axis] = (0, n - p.shape[axis])
        return jnp.pad(p, cfg)

    return jnp.concatenate([parts[0], padto(parts[1], LORA_PAD), padto(parts[2], LORA_PAD), parts[3]], axis=axis)


def _rwkv_mixer(u, mu, w0, w2, a0, a2, g2, k_k, k_a, r_k, lnx_w, lnx_b):
    row = lambda p: p.reshape(1, -1).astype(F32)
    padrows = lambda m: jnp.pad(m, ((0, LORA_PAD - m.shape[0]), (0, 0))).astype(BF16)
    r, lw, k, v, a, b, g, bonus = _rw_prep(
        u, _pad_lora_cols(row(mu), 1), row(w0), padrows(w2), row(a0), padrows(a2), g2.astype(BF16),
        row(k_k), row(k_a), row(r_k))
    lhs, add = _rw_chunks(r, lw, k, v, a, b)
    return _rw_scan(lhs, add, bonus, g, row(lnx_w), row(lnx_b))


def _lru_kernel(u_ref, cw_ref, cb_ref, wa_ref, ba_ref, wx_ref, bx_ref, sp_ref, y_ref, buf_ref, a_sc, b_sc, h_sc):
    i = pl.program_id(0)
    tm = u_ref.shape[0]

    @pl.when(i == 0)
    def _():
        buf_ref[0:8, :] = jnp.zeros((8, LRU_W), F32)
        h_sc[...] = jnp.zeros_like(h_sc)

    xb = u_ref[:, LRU_W:]
    buf_ref[8:8 + tm, :] = xb
    xc = cb_ref[...] + cw_ref[CONV_W - 1:CONV_W, :] * xb
    for kk in range(CONV_W - 1):
        xc = xc + cw_ref[kk:kk + 1, :] * buf_ref[pl.ds(8 - (CONV_W - 1) + kk, tm), :]
    buf_ref[0:8, :] = xb[tm - 8:tm, :]

    xcb = xc.astype(BF16)
    ra, ri = [], []
    for n in range(LRU_BLOCKS):
        sl = slice(n * LRU_BW, (n + 1) * LRU_BW)
        ra.append(_dot(xcb[:, sl], wa_ref[n]))
        ri.append(_dot(xcb[:, sl], wx_ref[n]))
    rg = _sigmoid(jnp.concatenate(ra, axis=1) + ba_ref[...])
    ig = _sigmoid(jnp.concatenate(ri, axis=1) + bx_ref[...])
    log_a = -LRU_C * rg * sp_ref[...]
    a_sc[...] = jnp.exp(log_a)
    b_sc[...] = jnp.sqrt(1.0 - jnp.exp(2.0 * log_a)) * ig * xc

    def step(t, h):
        h = a_sc[pl.ds(t, 1), :] * h + b_sc[pl.ds(t, 1), :]
        b_sc[pl.ds(t, 1), :] = h
        return h

    h_sc[...] = lax.fori_loop(0, tm, step, h_sc[...], unroll=8)
    gate = u_ref[:, 0:LRU_W]
    gelu = 0.5 * gate * (1.0 + jnp.tanh(0.7978845608028654 * (gate + 0.044715 * gate * gate * gate)))
    y_ref[...] = (b_sc[...] * gelu).astype(BF16)


def _lru_mixer(u, conv_w, conv_b, wa, ba, wx, bx, lam):
    S = u.shape[0]
    tm = _pick(S, (256, 128, 64))
    row = lambda p: p.reshape(1, -1).astype(F32)
    vec = pl.BlockSpec((1, LRU_W), lambda i: (0, 0))
    blk = pl.BlockSpec((LRU_BLOCKS, LRU_BW, LRU_BW), lambda i: (0, 0, 0))
    lamf = lam.astype(F32)
    softplus_neg_lam = row(jnp.maximum(-lamf, 0.0) + jnp.log1p(jnp.exp(-jnp.abs(lamf))))
    cw = jnp.concatenate([conv_w.astype(F32), jnp.zeros((8 - CONV_W, LRU_W), F32)], axis=0)
    return pl.pallas_call(
        _lru_kernel,
        grid=(S // tm,),
        in_specs=[pl.BlockSpec((tm, 2 * LRU_W), lambda i: (i, 0)), pl.BlockSpec((8, LRU_W), lambda i: (0, 0)),
                  vec, blk, vec, blk, vec, vec],
        out_specs=pl.BlockSpec((tm, LRU_W), lambda i: (i, 0)),
        out_shape=jax.ShapeDtypeStruct((S, LRU_W), BF16),
        scratch_shapes=[pltpu.VMEM((tm + 8, LRU_W), F32), pltpu.VMEM((tm, LRU_W), F32),
                        pltpu.VMEM((tm, LRU_W), F32), pltpu.VMEM((1, LRU_W), F32)],
        compiler_params=_cp("arbitrary"),
        name="rglru",
    )(u, cw, row(conv_b), wa.astype(BF16), row(ba), wx.astype(BF16), row(bx), softplus_neg_lam)


def _hgrn_kernel(u_ref, lower_ref, ng_ref, y_ref, st_ref):
    C = HG_CHUNK
    W = HG_KW

    @pl.when(pl.program_id(0) == 0)
    def _():
        st_ref[...] = jnp.zeros_like(st_ref)

    lower = lower_ref[...]
    forget = lower + (1.0 - lower) * _sigmoid(u_ref[:, W:2 * W])
    lf = jnp.log(forget)
    kk = 1.0 - forget
    qx = u_ref[:, 0:W]
    q = qx * _sigmoid(qx)
    v = u_ref[:, 2 * W:3 * W]
    gx = u_ref[:, 3 * W:4 * W]

    ti = lax.broadcasted_iota(jnp.int32, (C, C), 0)
    si = lax.broadcasted_iota(jnp.int32, (C, C), 1)
    b = _dot_hi((si <= ti).astype(F32), lf)
    blast = b[C - 1:C, :]
    q_in = (q * jnp.exp(b)).astype(BF16)
    k_end = (kk * jnp.exp(blast - b)).astype(BF16)
    vb = v.astype(BF16)
    rowi = lax.broadcasted_iota(jnp.int32, (C, W), 0)

    n_sub = C // HG_SUB
    refs = [jnp.zeros((1, W), F32)] + [b[i * HG_SUB - 1:i * HG_SUB, :] for i in range(1, n_sub)]
    ref_rows = jnp.concatenate([jnp.broadcast_to(r_, (HG_SUB, W)) for r_ in refs], axis=0)
    q_sub = (q * jnp.exp(b - ref_rows)).astype(BF16)
    k_sub = []
    for i in range(n_sub):
        live = rowi < (i + 1) * HG_SUB
        k_sub.append(jnp.where(live, kk * jnp.exp(jnp.where(live, refs[i] - b, 0.0)), 0.0).astype(BF16))
    k_sub = jnp.concatenate(k_sub, axis=0)
    row_blk = ti // HG_SUB

    heads = range(HG_HEADS)
    sls = [slice(h * HG_DK, (h + 1) * HG_DK) for h in heads]
    states = [st_ref[:, sl] for sl in sls]
    inter = [_dot_nt(q_in[:, sls[h]], states[h].astype(BF16)) for h in heads]
    pair = [_dot_nt(q_sub[:, sl], k_sub[:, sl]) for sl in sls]
    att = []
    for h in heads:
        a_h = jnp.zeros((C, C), F32)
        for i in range(n_sub):
            a_h = jnp.where((row_blk == i) & (si <= ti), pair[h][:, i * C:(i + 1) * C], a_h)
        att.append(a_h.astype(BF16))
    intra = [_dot(att[h], vb[:, sls[h]]) for h in heads]
    upd = [_dot_tn(vb[:, sl], k_end[:, sl]) for sl in sls]
    for h in heads:
        sl = sls[h]
        st_ref[:, sl] = jnp.exp(blast[:, sl]) * states[h] + upd[h]
        o = inter[h] + intra[h]
        ms = jnp.mean(o * o, axis=-1, keepdims=True)
        on = o * lax.rsqrt(ms + NORM_EPS) * ng_ref[:, sl]
        gh = gx[:, sl]
        y_ref[:, sl] = (on * (gh * _sigmoid(gh))).astype(BF16)


def _hgrn_mixer(u, lower, norm_g):
    S = u.shape[0]
    C = HG_CHUNK
    vec = pl.BlockSpec((1, HG_KW), lambda c: (0, 0))
    return pl.pallas_call(
        _hgrn_kernel,
        grid=(S // C,),
        in_specs=[pl.BlockSpec((C, 4 * HG_KW), lambda c: (c, 0)), vec, vec],
        out_specs=pl.BlockSpec((C, HG_VW), lambda c: (c, 0)),
        out_shape=jax.ShapeDtypeStruct((S, HG_VW), BF16),
        scratch_shapes=[pltpu.VMEM((HG_DV, HG_KW), F32)],
        compiler_params=_cp("arbitrary"),
        name="hgrn2",
    )(u, lower.reshape(1, -1).astype(F32), norm_g.reshape(1, -1).astype(F32))


def kernel(x, norm_mix, norm_mlp, w_ff1, w_ff2, w_in_a, w_out_a, nsa_qk_gain, nsa_cmp_w, nsa_cmp_pe, rw_mu, rw_w0, rw_w2, rw_a0, rw_a2, rw_g2, rw_k_k, rw_k_a, rw_r_k, rw_lnx_w, rw_lnx_b, w_in_b, w_out_b, lru_conv_w, lru_conv_b, lru_wa, lru_ba, lru_wx, lru_bx, lru_lambda, hg_lb, hg_norm):
    B, S, D = x.shape
    depth = norm_mix.shape[0]
    lb_p = jax.nn.softmax(hg_lb.astype(F32), axis=0)
    lb_cum = jnp.cumsum(lb_p, axis=0)
    hg_lower = lb_cum - lb_cum[0:1]

    outs = []
    for bi in range(B):
        xb = x[bi].astype(F32)
        for layer in range(depth):
            gmix = norm_mix[layer].astype(F32)
            if layer % 2 == 0:
                e = layer // 2
                w_in = w_in_a[e]
                w_nsa = jnp.pad(w_in[:, :NSA_COLS], ((0, 0), (0, NSA_COLS_PAD - NSA_COLS))).astype(BF16)
                w_rw = _pad_lora_cols(w_in[:, NSA_COLS:], 1).astype(BF16)
                y_a = _nsa_mixer(_norm_matmul(xb, gmix, w_nsa), nsa_qk_gain[e], nsa_cmp_w[e], nsa_cmp_pe[e])
                y_b = _rwkv_mixer(_norm_matmul(xb, gmix, w_rw), rw_mu[e], rw_w0[e], rw_w2[e], rw_a0[e], rw_a2[e],
                                  rw_g2[e], rw_k_k[e], rw_k_a[e], rw_r_k[e], rw_lnx_w[e], rw_lnx_b[e])
                xb = _out_proj(y_a, y_b, w_out_a[e].astype(BF16), xb)
            else:
                o = layer // 2
                w_in = w_in_b[o].astype(BF16)
                y_c = _lru_mixer(_norm_matmul(xb, gmix, w_in[:, :2 * LRU_W]), lru_conv_w[o], lru_conv_b[o],
                                 lru_wa[o], lru_ba[o], lru_wx[o], lru_bx[o], lru_lambda[o])
                y_d = _hgrn_mixer(_norm_matmul(xb, gmix, w_in[:, 2 * LRU_W:]), hg_lower[layer], hg_norm[o])
                xb = _out_proj(y_c, y_d, w_out_b[o].astype(BF16), xb)
            xb = _mlp(xb, norm_mlp[layer].astype(F32), w_ff1[layer].astype(BF16), w_ff2[layer].astype(BF16))
        outs.append(xb)
    return jnp.stack(outs, axis=0).astype(x.dtype)
```

```python
import functools

import jax
import jax.numpy as jnp
import numpy as np
from jax import lax
from jax.experimental import pallas as pl
from jax.experimental.pallas import tpu as pltpu

F32 = jnp.float32
BF16 = jnp.bfloat16
HI = lax.Precision.HIGHEST

NORM_EPS = 1e-6
NEG_INF = -1e30
ROPE_THETA = 10000.0
LOG2E = 1.4426950408889634

NSA_HEADS = 8
NSA_KV_HEADS = 2
NSA_GROUP = NSA_HEADS // NSA_KV_HEADS
NSA_DH = 128
CMP_LEN = 32
CMP_STRIDE = 16
SLC_LEN = 64
SLC_TOPN = 16
WIN = 512
Q_BLOCK = 128
SLC_FORCE = 1e4
NSA_W = NSA_HEADS * NSA_DH
NSA_KV_W = NSA_KV_HEADS * NSA_DH
NSA_COLS = NSA_W + 6 * NSA_KV_W + 3 * NSA_HEADS
NSA_COLS_PAD = NSA_W + 6 * NSA_KV_W + 128
SLC_TILE = 512
NSA_QPAIR = 2

RW_HEADS = 16
RW_DH = 64
RW_W = RW_HEADS * RW_DH
RW_DECAY_LORA = 96
RW_A_LORA = 96
RW_G_LORA = 256
RW_LNX_EPS = 64e-5
RW_COLS = 3 * RW_W + RW_DECAY_LORA + RW_A_LORA + RW_G_LORA
LORA_PAD = 128
RW_COLS_PAD = 3 * RW_W + 2 * LORA_PAD + RW_G_LORA
RW_CHUNK = 64
RW_PACK = 4

LRU_W = 1024
LRU_BLOCKS = 8
LRU_BW = LRU_W // LRU_BLOCKS
CONV_W = 4
LRU_C = 8.0

HG_HEADS = 8
HG_DK = 128
HG_DV = 128
HG_KW = HG_HEADS * HG_DK
HG_VW = HG_HEADS * HG_DV
HG_CHUNK = 64
HG_SUB = 16

MXU_WIDTH = 256
VMEM_LIMIT = 56 * 1024 * 1024


def _cp(*sem, flags=None):
    return pltpu.CompilerParams(dimension_semantics=sem, vmem_limit_bytes=VMEM_LIMIT, flags=flags)


def _pick(n, cands):
    for c in cands:
        if n % c == 0:
            return c
    return n


def _sigmoid(x):
    return 1.0 / (1.0 + jnp.exp(-x))


def _softplus(x):
    return jnp.maximum(x, 0.0) + jnp.log(1.0 + jnp.exp(-jnp.abs(x)))


def _split3(x):
    p1 = x.astype(BF16)
    r1 = x - p1.astype(F32)
    p2 = r1.astype(BF16)
    p3 = (r1 - p2.astype(F32)).astype(BF16)
    return p1, p2, p3


def _dot(a, b):
    return jnp.dot(a, b, preferred_element_type=F32)


def _dot_hi(a, b):
    return jnp.dot(a, b, preferred_element_type=F32, precision=HI)


def _dot_nt(a, b, precision=None):
    return lax.dot_general(a, b, (((1,), (1,)), ((), ())), preferred_element_type=F32, precision=precision)


def _dot_tn(a, b, precision=None):
    return lax.dot_general(a, b, (((0,), (0,)), ((), ())), preferred_element_type=F32, precision=precision)


def _norm_matmul_kernel(x_ref, g_ref, w_ref, o_ref, *, tn):
    x = x_ref[...]
    ms = jnp.mean(x * x, axis=-1, keepdims=True)
    hn = (x * lax.rsqrt(ms + NORM_EPS) * g_ref[...]).astype(BF16)
    n = w_ref.shape[1]
    for lo in range(0, n, tn):
        hi = min(lo + tn, n)
        o_ref[:, lo:hi] = _dot(hn, w_ref[:, lo:hi])


def _norm_matmul(x, g, w):
    S, D = x.shape
    N = w.shape[1]
    tm = _pick(S, (512, 256, 128))
    tn = 3 * MXU_WIDTH
    return pl.pallas_call(
        functools.partial(_norm_matmul_kernel, tn=tn),
        grid=(S // tm,),
        in_specs=[
            pl.BlockSpec((tm, D), lambda i: (i, 0)),
            pl.BlockSpec((1, D), lambda i: (0, 0)),
            pl.BlockSpec((D, N), lambda i: (0, 0), pipeline_mode=pl.Buffered(1)),
        ],
        out_specs=pl.BlockSpec((tm, N), lambda i: (i, 0)),
        out_shape=jax.ShapeDtypeStruct((S, N), F32),
        compiler_params=_cp("parallel"),
        name="norm_matmul",
    )(x, g.reshape(1, D), w)


def _matmul_kernel(a_ref, w_ref, o_ref):
    o_ref[...] = _dot(a_ref[...], w_ref[...])


def _matmul(a, w):
    M, K = a.shape
    N = w.shape[1]
    tm = _pick(M, (256, 128, 64, 32, 16, 8))
    return pl.pallas_call(
        _matmul_kernel,
        grid=(M // tm,),
        in_specs=[pl.BlockSpec((tm, K), lambda i: (i, 0)), pl.BlockSpec((K, N), lambda i: (0, 0))],
        out_specs=pl.BlockSpec((tm, N), lambda i: (i, 0)),
        out_shape=jax.ShapeDtypeStruct((M, N), F32),
        compiler_params=_cp("parallel"),
        name="matmul",
    )(a, w)


def _out_proj_kernel(a_ref, b_ref, wa_ref, wb_ref, r_ref, o_ref):
    o_ref[...] = r_ref[...] + _dot(a_ref[...], wa_ref[...]) + _dot(b_ref[...], wb_ref[...])


def _out_proj(ya, yb, w, res):
    S, Wa = ya.shape
    Wb = yb.shape[1]
    D = w.shape[1]
    tm = _pick(S, (512, 256, 128))
    return pl.pallas_call(
        _out_proj_kernel,
        grid=(S // tm,),
        in_specs=[
            pl.BlockSpec((tm, Wa), lambda i: (i, 0)),
            pl.BlockSpec((tm, Wb), lambda i: (i, 0)),
            pl.BlockSpec((Wa, D), lambda i: (0, 0)),
            pl.BlockSpec((Wb, D), lambda i: (0, 0)),
            pl.BlockSpec((tm, D), lambda i: (i, 0)),
        ],
        out_specs=pl.BlockSpec((tm, D), lambda i: (i, 0)),
        out_shape=jax.ShapeDtypeStruct((S, D), F32),
        compiler_params=_cp("parallel"),
        name="out_proj",
    )(ya, yb, w[:Wa], w[Wa:], res)


def _mlp_kernel(x_ref, g_ref, w1_ref, w2_ref, o_ref, hn_ref):
    f = pl.program_id(1)

    @pl.when(f == 0)
    def _():
        x = x_ref[...]
        ms = jnp.mean(x * x, axis=-1, keepdims=True)
        hn_ref[...] = (x * lax.rsqrt(ms + NORM_EPS) * g_ref[...]).astype(BF16)
        o_ref[...] = x

    hn = hn_ref[...]
    half = w1_ref.shape[1] // 2
    zs = [jnp.maximum(_dot(hn, w1_ref[:, c * half:(c + 1) * half].astype(BF16)), 0.0) for c in range(2)]
    acc = _dot((zs[0] * zs[0]).astype(BF16), w2_ref[0:half, :].astype(BF16))
    acc = acc + _dot((zs[1] * zs[1]).astype(BF16), w2_ref[half:2 * half, :].astype(BF16))
    o_ref[...] += acc


def _mlp(x, g, w1, w2, layer):
    S, D = x.shape
    Fdim = w1.shape[2]
    tm = _pick(S, (1024, 512, 256, 128))
    tf = _pick(Fdim, (512, 256, 128))
    return pl.pallas_call(
        _mlp_kernel,
        grid=(S // tm, Fdim // tf),
        in_specs=[
            pl.BlockSpec((tm, D), lambda i, f: (i, 0), pipeline_mode=pl.Buffered(1)),
            pl.BlockSpec((1, D), lambda i, f: (0, 0)),
            pl.BlockSpec((None, D, tf), lambda i, f: (layer, 0, f)),
            pl.BlockSpec((None, tf, D), lambda i, f: (layer, f, 0)),
        ],
        out_specs=pl.BlockSpec((tm, D), lambda i, f: (i, 0)),
        out_shape=jax.ShapeDtypeStruct((S, D), F32),
        scratch_shapes=[pltpu.VMEM((tm, D), BF16)],
        compiler_params=_cp("parallel", "arbitrary"),
        name="mlp",
    )(x, g.reshape(1, D), w1, w2)


def _head_norm_rope(x, gain, cosf, sinf):
    ms = jnp.mean(x * x, axis=-1, keepdims=True)
    y = x * lax.rsqrt(ms + NORM_EPS) * gain
    return y * cosf + pltpu.roll(y, NSA_DH // 2, 1) * sinf


def _nsa_prep_kernel(u_ref, gain_ref, cos_ref, sin_ref,
                     q_ref, kc_ref, vc_ref, ks_ref, vs_ref, kw_ref, vw_ref, gate_ref):
    cosf = cos_ref[...]
    sinf = sin_ref[...]
    scale = NSA_DH ** -0.5 * LOG2E
    for h in range(NSA_HEADS):
        sl = slice(h * NSA_DH, (h + 1) * NSA_DH)
        q_ref[:, sl] = (_head_norm_rope(u_ref[:, sl], gain_ref[0:1, :], cosf, sinf) * scale).astype(BF16)
    base = NSA_W
    kc_ref[...] = u_ref[:, base:base + NSA_KV_W].astype(BF16)
    vc_ref[...] = u_ref[:, base + NSA_KV_W:base + 2 * NSA_KV_W].astype(BF16)
    vs_ref[...] = u_ref[:, base + 3 * NSA_KV_W:base + 4 * NSA_KV_W].astype(BF16)
    vw_ref[...] = u_ref[:, base + 5 * NSA_KV_W:base + 6 * NSA_KV_W].astype(BF16)
    for g in range(NSA_KV_HEADS):
        sl = slice(g * NSA_DH, (g + 1) * NSA_DH)
        o_s = base + 2 * NSA_KV_W + g * NSA_DH
        o_w = base + 4 * NSA_KV_W + g * NSA_DH
        ks_ref[:, sl] = _head_norm_rope(u_ref[:, o_s:o_s + NSA_DH], gain_ref[2:3, :], cosf, sinf).astype(BF16)
        kw_ref[:, sl] = _head_norm_rope(u_ref[:, o_w:o_w + NSA_DH], gain_ref[3:4, :], cosf, sinf).astype(BF16)
    gate_ref[...] = _sigmoid(u_ref[:, base + 6 * NSA_KV_W:base + 6 * NSA_KV_W + 128])


def _nsa_prep(u, gain8, cosf, sinf):
    S = u.shape[0]
    tm = _pick(S, (512, 256, 128))
    row = lambda w: pl.BlockSpec((tm, w), lambda i: (i, 0))
    kv = jax.ShapeDtypeStruct((S, NSA_KV_W), BF16)
    return pl.pallas_call(
        _nsa_prep_kernel,
        grid=(S // tm,),
        in_specs=[row(NSA_COLS_PAD), pl.BlockSpec((8, NSA_DH), lambda i: (0, 0)), row(NSA_DH), row(NSA_DH)],
        out_specs=[row(NSA_W)] + [row(NSA_KV_W)] * 6 + [row(128)],
        out_shape=[jax.ShapeDtypeStruct((S, NSA_W), BF16)] + [kv] * 6 + [jax.ShapeDtypeStruct((S, 128), F32)],
        compiler_params=_cp("parallel"),
        name="nsa_prep",
    )(u, gain8, cosf, sinf)


def _nsa_cmp_finish_kernel(ak_ref, av_ref, pek_ref, pev_ref, gain_ref, cos_ref, sin_ref,
                           kcmp_ref, vcmp_ref, buf_ref):
    n = ak_ref.shape[0]
    buf_ref[n:n + 8, :] = jnp.zeros((8, NSA_KV_W), F32)

    def combine(a_ref, pe_ref):
        buf_ref[0:n, :] = a_ref[:, NSA_KV_W:]
        pe = pe_ref[0:1, 0:NSA_KV_W] + pe_ref[1:2, NSA_KV_W:]
        return a_ref[:, 0:NSA_KV_W] + buf_ref[pl.ds(1, n), :] + pe

    kc = combine(ak_ref, pek_ref)
    for g in range(NSA_KV_HEADS):
        sl = slice(g * NSA_DH, (g + 1) * NSA_DH)
        kcmp_ref[:, sl] = _head_norm_rope(kc[:, sl], gain_ref[1:2, :], cos_ref[...], sin_ref[...]).astype(BF16)
    vcmp_ref[...] = combine(av_ref, pev_ref).astype(BF16)


def _nsa_compress(kc, vc, cmp_w, cmp_pe, gain8, ccos, csin):
    S = kc.shape[0]
    n = S // CMP_STRIDE
    half = CMP_LEN // 2
    eye = jnp.eye(NSA_KV_HEADS, dtype=F32)

    def expand(w):
        lo = jnp.einsum('lde,gh->lgdhe', w[:half], eye).reshape(half * NSA_KV_W, NSA_KV_W)
        hi = jnp.einsum('lde,gh->lgdhe', w[half:], eye).reshape(half * NSA_KV_W, NSA_KV_W)
        return jnp.concatenate([lo, hi], axis=1).astype(BF16)

    def expand_pe(pe):
        lo = jnp.broadcast_to(pe[:half, None, :], (half, NSA_KV_HEADS, NSA_DH)).reshape(1, -1)
        hi = jnp.broadcast_to(pe[half:, None, :], (half, NSA_KV_HEADS, NSA_DH)).reshape(1, -1)
        return jnp.concatenate([lo, hi, jnp.zeros((6, lo.shape[1]), F32)], axis=0).astype(BF16)

    wk, wv = expand(cmp_w[0]), expand(cmp_w[1])
    ak = _matmul(kc.reshape(n, CMP_STRIDE * NSA_KV_W), wk)
    av = _matmul(vc.reshape(n, CMP_STRIDE * NSA_KV_W), wv)
    pek = _matmul(expand_pe(cmp_pe[0]), wk)
    pev = _matmul(expand_pe(cmp_pe[1]), wv)
    full = lambda shp: pl.BlockSpec(shp, lambda i: (0, 0))
    return pl.pallas_call(
        _nsa_cmp_finish_kernel,
        grid=(1,),
        in_specs=[full((n, 2 * NSA_KV_W)), full((n, 2 * NSA_KV_W)), full((8, 2 * NSA_KV_W)), full((8, 2 * NSA_KV_W)),
                  full((8, NSA_DH)), full((n, NSA_DH)), full((n, NSA_DH))],
        out_specs=[full((n, NSA_KV_W)), full((n, NSA_KV_W))],
        out_shape=[jax.ShapeDtypeStruct((n, NSA_KV_W), BF16)] * 2,
        scratch_shapes=[pltpu.VMEM((n + 8, NSA_KV_W), F32)],
        compiler_params=_cp("arbitrary"),
        name="nsa_cmp_finish",
    )(ak, av, pek, pev, gain8, ccos, csin)


def _nsa_attn_kernel(q_ref, kcmp_ref, vcmpT_ref, ks_ref, vsT_ref, kw_ref, vwT_ref, gate_ref, khot_ref,
                     o_ref, qT_ref, psum_ref, sel_ref, oc_ref, m_ref, l_ref, acc_ref, s_ref, *, top_n):
    g = pl.program_id(0)
    pb = pl.program_id(1)
    rows = NSA_GROUP * Q_BLOCK
    n_cmp = kcmp_ref.shape[0]
    n_slc = sel_ref.shape[1]
    streams = range(NSA_QPAIR)
    t0s = [(pb * NSA_QPAIR + i) * Q_BLOCK for i in streams]

    lane = lax.broadcasted_iota(jnp.int32, (1, rows), 1)
    q_io = lax.broadcasted_iota(jnp.int32, (1, Q_BLOCK), 1)
    n_io = lax.broadcasted_iota(jnp.int32, (n_cmp, 1), 0)
    m_io = lax.broadcasted_iota(jnp.int32, (n_slc, 1), 0)
    for i in streams:
        t0 = t0s[i]
        t_lane = t0 + (lane & (Q_BLOCK - 1))
        qf = q_ref[i * Q_BLOCK:(i + 1) * Q_BLOCK, :].astype(F32)
        qT = jnp.concatenate([qf[:, j * NSA_DH:(j + 1) * NSA_DH].T for j in range(NSA_GROUP)], axis=1).astype(BF16)
        qT_ref[i, 0:NSA_DH, :] = qT
        qT_ref[i, NSA_DH:2 * NSA_DH, :] = jnp.zeros((NSA_DH, rows), BF16)

        s = _dot(kcmp_ref[...], qT)
        valid = (n_io * CMP_STRIDE + (CMP_LEN - 1)) <= t_lane
        sm = jnp.where(valid, s, NEG_INF)
        mx = jnp.max(sm, axis=0, keepdims=True)
        e = jnp.where(valid, jnp.exp2(sm - mx), 0.0)
        den = jnp.sum(e, axis=0, keepdims=True)
        p = e * jnp.where(den > 0.0, 1.0 / den, 0.0)
        oc_ref[i] = _dot(vcmpT_ref[...], p.astype(BF16))
        psum = p[:, 0:Q_BLOCK]
        for j in range(1, NSA_GROUP):
            psum = psum + p[:, j * Q_BLOCK:(j + 1) * Q_BLOCK]
        psum_ref[i, 0:8, :] = jnp.zeros((8, Q_BLOCK), F32)
        psum_ref[i, 8:8 + n_cmp, :] = psum
        ratio = SLC_LEN // CMP_STRIDE
        imp = 0.5 * (psum_ref[i, pl.ds(7, n_slc, stride=ratio), :] + psum_ref[i, pl.ds(8 + ratio - 1, n_slc, stride=ratio), :])
        for c in range(ratio - 1):
            imp = imp + psum_ref[i, pl.ds(8 + c, n_slc, stride=ratio), :]

        cur = jnp.right_shift(t0 + q_io, 6)
        forced = (m_io == 0) | (m_io == cur) | (m_io == cur - 1)
        score = jnp.where(forced, SLC_FORCE, jnp.where(m_io <= cur, imp, -1.0))
        bias = jnp.full((n_slc, Q_BLOCK), NEG_INF, F32)
        for _ in range(top_n):
            best = jnp.max(score, axis=0, keepdims=True)
            idx = jnp.min(jnp.where(score == best, m_io, n_slc), axis=0, keepdims=True)
            pick = m_io == idx
            bias = jnp.where(pick, 0.0, bias)
            score = jnp.where(pick, -3e38, score)
        sel_ref[i] = bias

    m_ref[...] = jnp.full(m_ref.shape, NEG_INF, F32)
    l_ref[...] = jnp.zeros(l_ref.shape, F32)
    acc_ref[...] = jnp.zeros(acc_ref.shape, F32)
    blocks_per_tile = SLC_TILE // SLC_LEN

    sub = SLC_TILE // 4

    def scores_into(kt, slot):
        k_tile = ks_ref[pl.ds(pl.multiple_of(kt * SLC_TILE, SLC_TILE), SLC_TILE), :]
        k_aug = jnp.concatenate([k_tile, khot_ref[slot]], axis=1)
        band = NSA_DH + 16 * slot
        for i in streams:
            selt = sel_ref[i, pl.ds(pl.multiple_of(kt * blocks_per_tile, blocks_per_tile), blocks_per_tile), :]
            rows8 = jnp.concatenate([selt] * NSA_GROUP, axis=1)
            qT_ref[i, band:band + 16, :] = jnp.concatenate([rows8, jnp.zeros_like(rows8)], axis=0).astype(BF16)
        for i in streams:
            for c in range(4):
                s_ref[i, slot, c * sub:(c + 1) * sub, :] = _dot(k_aug[c * sub:(c + 1) * sub, :], qT_ref[i])

    def absorb(kt, slot, causal):
        v_tile = vsT_ref[kt]
        for i in streams:
            sm = s_ref[i, slot]
            if causal:
                kpos = kt * SLC_TILE + lax.broadcasted_iota(jnp.int32, (SLC_TILE, rows), 0)
                sm = jnp.where(kpos <= t0s[i] + (lane & (Q_BLOCK - 1)), sm, NEG_INF)
            m_old = m_ref[i]
            m_new = jnp.maximum(m_old, jnp.max(sm, axis=0, keepdims=True))
            p = jnp.exp2(sm - m_new)
            alpha = jnp.exp2(m_old - m_new)
            l_ref[i] = alpha * l_ref[i] + jnp.sum(p, axis=0, keepdims=True)
            acc_ref[i] = alpha * acc_ref[i] + _dot(v_tile, p.astype(BF16))
            m_ref[i] = m_new

    last_kt = (t0s[0] + Q_BLOCK - 1) // SLC_TILE
    scores_into(0, 0)

    def slc_pair(kk, carry):
        kt = 2 * kk
        scores_into(kt + 1, 1)
        absorb(kt, 0, False)
        scores_into(kt + 2, 0)
        absorb(kt + 1, 1, False)
        return carry

    lax.fori_loop(0, last_kt // 2, slc_pair, 0)

    @pl.when(last_kt % 2 == 0)
    def _():
        absorb(last_kt, 0, True)

    @pl.when(last_kt % 2 == 1)
    def _():
        scores_into(last_kt, 1)
        absorb(last_kt - 1, 0, False)
        absorb(last_kt, 1, True)

    gT = gate_ref[...].T
    wlen = WIN + Q_BLOCK
    for i in streams:
        t0 = t0s[i]
        t_lane = t0 + (lane & (Q_BLOCK - 1))
        o_s = acc_ref[i] * (1.0 / l_ref[i])

        start = pl.multiple_of(jnp.maximum(t0 - WIN, 0), Q_BLOCK)
        s = _dot(kw_ref[pl.ds(start, wlen), :], qT_ref[i, 0:NSA_DH, :])
        kpos = start + lax.broadcasted_iota(jnp.int32, (wlen, 1), 0)
        d = t_lane - kpos
        mask = (d >= 0) & (d < WIN)
        sm = jnp.where(mask, s, NEG_INF)
        mx = jnp.max(sm, axis=0, keepdims=True)
        e = jnp.where(mask, jnp.exp2(sm - mx), 0.0)
        p = (e * (1.0 / jnp.sum(e, axis=0, keepdims=True))).astype(BF16)
        tile0 = start // Q_BLOCK
        o_w = _dot(vwT_ref[tile0], p[0:Q_BLOCK, :])
        for w in range(1, wlen // Q_BLOCK):
            o_w = o_w + _dot(vwT_ref[tile0 + w], p[w * Q_BLOCK:(w + 1) * Q_BLOCK, :])

        for j in range(NSA_GROUP):
            sl = slice(j * Q_BLOCK, (j + 1) * Q_BLOCK)

            def grow(b):
                r0 = b * NSA_HEADS + j
                r1 = b * NSA_HEADS + NSA_GROUP + j
                cols = slice(i * Q_BLOCK, (i + 1) * Q_BLOCK)
                return jnp.where(g == 0, gT[r0:r0 + 1, cols], gT[r1:r1 + 1, cols])

            o = grow(0) * oc_ref[i, :, sl] + grow(1) * o_s[:, sl] + grow(2) * o_w[:, sl]
            o_ref[i * Q_BLOCK:(i + 1) * Q_BLOCK, j * NSA_DH:(j + 1) * NSA_DH] = o.T.astype(BF16)


def _nsa_attention(q, kcmp, vcmp, ks, vs, kw, vw, gate):
    S = q.shape[0]
    n_cmp = kcmp.shape[0]
    n_slc = S // SLC_LEN
    n_qb = S // Q_BLOCK
    top_n = min(SLC_TOPN, n_slc)
    rows = NSA_GROUP * Q_BLOCK

    nq = NSA_QPAIR
    assert n_qb % nq == 0 and SLC_TILE % (nq * Q_BLOCK) == 0 and n_cmp * CMP_STRIDE == n_slc * SLC_LEN

    def tiles_T(v, width):
        return v.reshape(S // width, width, NSA_KV_HEADS, NSA_DH).transpose(2, 0, 3, 1)

    blk_of_key = np.arange(SLC_TILE)[:, None] // SLC_LEN
    khot = jnp.asarray(np.stack([blk_of_key + 16 * s == np.arange(NSA_DH)[None, :] for s in range(2)]), dtype=BF16)
    vcmpT = vcmp.reshape(n_cmp, NSA_KV_HEADS, NSA_DH).transpose(1, 2, 0)
    vsT = tiles_T(vs, SLC_TILE)
    vwT = tiles_T(vw, Q_BLOCK)

    return pl.pallas_call(
        functools.partial(_nsa_attn_kernel, top_n=top_n),
        grid=(NSA_KV_HEADS, n_qb // nq),
        in_specs=[
            pl.BlockSpec((nq * Q_BLOCK, rows), lambda g, b: (b, g)),
            pl.BlockSpec((n_cmp, NSA_DH), lambda g, b: (0, g)),
            pl.BlockSpec((None, NSA_DH, n_cmp), lambda g, b: (g, 0, 0)),
            pl.BlockSpec((S, NSA_DH), lambda g, b: (0, g)),
            pl.BlockSpec((None, S // SLC_TILE, NSA_DH, SLC_TILE), lambda g, b: (g, 0, 0, 0)),
            pl.BlockSpec((S, NSA_DH), lambda g, b: (0, g)),
            pl.BlockSpec((None, S // Q_BLOCK, NSA_DH, Q_BLOCK), lambda g, b: (g, 0, 0, 0)),
            pl.BlockSpec((nq * Q_BLOCK, 128), lambda g, b: (b, 0)),
            pl.BlockSpec((2, SLC_TILE, NSA_DH), lambda g, b: (0, 0, 0)),
        ],
        out_specs=pl.BlockSpec((nq * Q_BLOCK, rows), lambda g, b: (b, g)),
        out_shape=jax.ShapeDtypeStruct((S, NSA_W), BF16),
        scratch_shapes=[
            pltpu.VMEM((nq, 2 * NSA_DH, rows), BF16),
            pltpu.VMEM((nq, n_cmp + 8, Q_BLOCK), F32),
            pltpu.VMEM((nq, n_slc, Q_BLOCK), F32),
            pltpu.VMEM((nq, NSA_DH, rows), F32),
            pltpu.VMEM((nq, 1, rows), F32),
            pltpu.VMEM((nq, 1, rows), F32),
            pltpu.VMEM((nq, NSA_DH, rows), F32),
            pltpu.VMEM((nq, 2, SLC_TILE, rows), F32),
        ],
        compiler_params=_cp("arbitrary", "arbitrary"),
        name="nsa_attn",
    )(q, kcmp, vcmpT, ks, vsT, kw, vwT, gate, khot)


def _rope_tables(pos):
    inv = ROPE_THETA ** (-(jnp.arange(0, NSA_DH, 2, dtype=F32) / NSA_DH))
    ang = pos[:, None] * inv[None, :]
    c, s = jnp.cos(ang), jnp.sin(ang)
    return jnp.concatenate([c, c], axis=-1), jnp.concatenate([-s, s], axis=-1)


def _nsa_mixer(u, qk_gain, cmp_w, cmp_pe):
    S = u.shape[0]
    gain8 = jnp.concatenate([qk_gain.astype(F32), jnp.zeros((4, NSA_DH), F32)], axis=0)
    cosf, sinf = _rope_tables(jnp.arange(S, dtype=F32))
    q, kc, vc, ks, vs, kw, vw, gate = _nsa_prep(u, gain8, cosf, sinf)
    n = S // CMP_STRIDE
    ccos, csin = _rope_tables((jnp.arange(n) * CMP_STRIDE + CMP_LEN - 1).astype(F32))
    kcmp, vcmp = _nsa_compress(kc, vc, cmp_w, cmp_pe, gain8, ccos, csin)
    return _nsa_attention(q, kcmp, vcmp, ks, vs, kw, vw, gate)


def _head_ones():
    return jnp.asarray(np.kron(np.eye(128 // RW_DH), np.ones((RW_DH, RW_DH))), dtype=BF16)


def _seg_sum(x, ones_bd):
    outs = []
    for c in range(x.shape[1] // 128):
        p1, p2, p3 = _split3(x[:, c * 128:(c + 1) * 128])
        outs.append(_dot(p1, ones_bd) + _dot(p2, ones_bd) + _dot(p3, ones_bd))
    return jnp.concatenate(outs, axis=1)


def _rw_prep_kernel(u_ref, prev_ref, mu_ref, w0_ref, w2_ref, a0_ref, a2_ref, g2_ref, kk_ref, ka_ref, rk_ref, bd_ref,
                    r_ref, lw_ref, k_ref, v_ref, a_ref, b_ref, g_ref, bonus_ref, buf_ref):
    i = pl.program_id(0)
    tm = u_ref.shape[0]
    buf_ref[0:8, :] = jnp.where(i == 0, 0.0, prev_ref[...])
    buf_ref[8:8 + tm, :] = u_ref[...]
    u = u_ref[...]
    x = u + (buf_ref[pl.ds(7, tm), :] - u) * mu_ref[...]
    W = RW_W
    r, k, v = x[:, 0:W], x[:, W:2 * W], x[:, 2 * W:3 * W]
    wl = x[:, 3 * W:3 * W + LORA_PAD]
    al = x[:, 3 * W + LORA_PAD:3 * W + 2 * LORA_PAD]
    gl = x[:, 3 * W + 2 * LORA_PAD:]
    wx = w0_ref[...] + _dot(jnp.tanh(wl).astype(BF16), w2_ref[...])
    w = -_softplus(-wx) - 0.5
    a = _sigmoid(a0_ref[...] + _dot(al.astype(BF16), a2_ref[...]))
    g = _dot(_sigmoid(gl).astype(BF16), g2_ref[...])
    bd = bd_ref[...]
    kk = k * kk_ref[...]
    kk = kk / jnp.maximum(jnp.sqrt(_seg_sum(kk * kk, bd)), 1e-12)
    kf = k * (1.0 + (a - 1.0) * ka_ref[...])
    r_ref[...] = r
    lw_ref[...] = -jnp.exp(w)
    k_ref[...] = kf
    v_ref[...] = v
    a_ref[...] = -kk
    b_ref[...] = kk * a
    g_ref[...] = g
    bonus_ref[...] = _seg_sum(r * kf * rk_ref[...], bd) * v


def _rw_prep(u, mu, w0, w2, a0, a2, g2, k_k, k_a, r_k):
    S = u.shape[0]
    tm = _pick(S, (256, 128, 64))
    nb = tm // 8
    W = RW_W
    vec = lambda n: pl.BlockSpec((1, n), lambda i: (0, 0))
    mat = lambda a, b: pl.BlockSpec((a, b), lambda i: (0, 0))
    row = pl.BlockSpec((tm, W), lambda i: (i, 0))
    ones_bd = _head_ones()
    out = jax.ShapeDtypeStruct((S, W), F32)
    return pl.pallas_call(
        _rw_prep_kernel,
        grid=(S // tm,),
        in_specs=[
            pl.BlockSpec((tm, RW_COLS_PAD), lambda i: (i, 0)),
            pl.BlockSpec((8, RW_COLS_PAD), lambda i: (jnp.maximum(i * nb - 1, 0), 0)),
            vec(RW_COLS_PAD), vec(W), mat(LORA_PAD, W), vec(W), mat(LORA_PAD, W), mat(RW_G_LORA, W),
            vec(W), vec(W), vec(W), mat(128, 128),
        ],
        out_specs=[row] * 8,
        out_shape=[out] * 8,
        scratch_shapes=[pltpu.VMEM((tm + 8, RW_COLS_PAD), F32)],
        compiler_params=_cp("parallel"),
        name="rw_prep",
    )(u, u, mu, w0, w2, a0, a2, g2, k_k, k_a, r_k, ones_bd)


def _bd_rows(x, lane_head):
    lh = lane_head[0:x.shape[0], :]
    return jnp.concatenate([jnp.where(lh == h, x, jnp.zeros_like(x)) for h in range(RW_PACK)], axis=0)


def _diag_pack(full, lane_head):
    out = None
    for h in range(RW_PACK):
        blk = jnp.where(lane_head == h, full[h * RW_DH:(h + 1) * RW_DH, :], 0.0)
        out = blk if out is None else out + blk
    return out


def _rw_chunk_kernel(r_ref, lw_ref, k_ref, v_ref, a_ref, b_ref, lhs_ref, add_ref):
    for cb in range(r_ref.shape[0] // RW_CHUNK):
        _rw_chunk_body(r_ref, lw_ref, k_ref, v_ref, a_ref, b_ref, lhs_ref, add_ref, cb * RW_CHUNK)


def _rw_chunk_body(r_ref, lw_ref, k_ref, v_ref, a_ref, b_ref, lhs_ref, add_ref, r0):
    C = RW_CHUNK
    rs = slice(r0, r0 + C)
    ti = lax.broadcasted_iota(jnp.int32, (C, C), 0)
    si = lax.broadcasted_iota(jnp.int32, (C, C), 1)
    tri_incl = (si <= ti).astype(F32)
    GW = RW_PACK * RW_DH
    lane_head = lax.broadcasted_iota(jnp.int32, (C, GW), 1) // RW_DH
    lane_pos = lax.broadcasted_iota(jnp.int32, (C, GW), 1) % RW_DH
    row_t = lax.broadcasted_iota(jnp.int32, (C, GW), 0)
    eye_p = (lane_pos == row_t).astype(F32)
    strict = jnp.concatenate([lane_pos < row_t] * 2, axis=1)
    incl = jnp.concatenate([lane_pos <= row_t] * 2, axis=1)

    lw = lw_ref[rs, :]
    cs = _dot_hi(tri_incl, lw)
    cprev = cs - lw
    clast = cs[C - 1:C, :]
    e_neg = jnp.exp(-cs)
    e_end = jnp.exp(clast - cs)
    At = (a_ref[rs, :] * jnp.exp(cprev)).astype(BF16)
    Rt32 = r_ref[rs, :] * jnp.exp(cs)
    Rt = Rt32.astype(BF16)
    Bt = (b_ref[rs, :] * e_neg).astype(BF16)
    Kt = (k_ref[rs, :] * e_neg).astype(BF16)
    Bh = (b_ref[rs, :] * e_end).astype(BF16)
    Kh = (k_ref[rs, :] * e_end).astype(BF16)
    V = v_ref[rs, :].astype(BF16)
    gam = jnp.exp(clast)

    groups = range(RW_HEADS // RW_PACK)
    sls = [slice(gi * GW, (gi + 1) * GW) for gi in groups]
    bd = functools.partial(_bd_rows, lane_head=lane_head)
    G = [_dot_nt(jnp.concatenate([At[:, sl], Rt[:, sl]], axis=0),
                 jnp.concatenate([bd(Bt[:, sl]), bd(Kt[:, sl])], axis=0)) for sl in sls]
    top = [jnp.where(strict, g_[0:C, :], 0.0) for g_ in G]
    bot = [jnp.where(incl, g_[C:2 * C, :], 0.0).astype(BF16) for g_ in G]
    L = [t_[:, 0:GW] for t_ in top]
    AV = [_dot(jnp.concatenate([top[gi][:, GW:].astype(BF16), bot[gi][:, GW:]], axis=0), bd(V[:, sls[gi]])) for gi in groups]
    T = [eye_p + l_ for l_ in L]
    P = [_dot(l_.astype(BF16), bd(l_.astype(BF16))) for l_ in L]
    for _ in range(4):
        TP = [_dot(jnp.concatenate([T[gi].astype(BF16), P[gi].astype(BF16)], axis=0), bd(P[gi].astype(BF16)))
              for gi in groups]
        T = [T[gi] + TP[gi][0:C, :] for gi in groups]
        P = [tp[C:2 * C, :] for tp in TP]
    T = [T[gi] + _dot(T[gi].astype(BF16), bd(P[gi].astype(BF16))) for gi in groups]
    AU = [_dot(T[gi].astype(BF16), jnp.concatenate([bd(At[:, sls[gi]]), bd(AV[gi][0:C, :].astype(BF16))], axis=1))
          for gi in groups]
    Ahat = [au[:, 0:GW].astype(BF16) for au in AU]
    Uhat = [au[:, GW:].astype(BF16) for au in AU]
    RY = [_dot(bot[gi][:, 0:GW], jnp.concatenate([bd(Ahat[gi]), bd(Uhat[gi])], axis=1)) for gi in groups]
    phi_full = [_dot_tn(Bh[:, sls[gi]], Ahat[gi]) for gi in groups]
    hh_full = [_dot_tn(jnp.concatenate([Bh[:, sls[gi]], Kh[:, sls[gi]]], axis=0),
                       jnp.concatenate([Uhat[gi], V[:, sls[gi]]], axis=0)) for gi in groups]
    for gi in groups:
        sl = sls[gi]
        o0 = 2 * r0
        lhs_ref[o0:o0 + C, sl] = Rt32[:, sl] + RY[gi][:, 0:GW]
        add_ref[o0:o0 + C, sl] = RY[gi][:, GW:] + AV[gi][C:2 * C, :]
        lhs_ref[o0 + C:o0 + 2 * C, sl] = eye_p * gam[:, sl] + _diag_pack(phi_full[gi], lane_head)
        add_ref[o0 + C:o0 + 2 * C, sl] = _diag_pack(hh_full[gi], lane_head)


def _rw_chunks(r, lw, k, v, a, b):
    S = r.shape[0]
    C = RW_CHUNK * _pick(S // RW_CHUNK, (2, 1))
    row = pl.BlockSpec((C, RW_W), lambda c: (c, 0))
    out = pl.BlockSpec((2 * C, RW_W), lambda c: (c, 0))
    shp = jax.ShapeDtypeStruct((2 * S, RW_W), F32)
    return pl.pallas_call(
        _rw_chunk_kernel,
        grid=(S // C,),
        in_specs=[row] * 6,
        out_specs=[out, out],
        out_shape=[shp, shp],
        compiler_params=_cp("parallel"),
        name="rw_chunk",
    )(r, lw, k, v, a, b)


def _rw_scan_kernel(lhs_ref, add_ref, bonus_ref, g_ref, lnw_ref, lnb_ref, bd_ref, y_ref, h_ref):
    C = RW_CHUNK

    @pl.when(pl.program_id(0) == 0)
    def _():
        h_ref[...] = jnp.zeros_like(h_ref)

    GW = RW_PACK * RW_DH
    lane_head = lax.broadcasted_iota(jnp.int32, (RW_DH, GW), 1) // RW_DH
    groups = range(RW_HEADS // RW_PACK)
    gsls = [slice(gi * GW, (gi + 1) * GW) for gi in groups]
    bd = bd_ref[...]
    ys = []
    for cb in range(y_ref.shape[0] // C):
        two = slice(2 * C * cb, 2 * C * (cb + 1))
        allres = [_dot(lhs_ref[two, gsl].astype(BF16), _bd_rows(h_ref[:, gsl].astype(BF16), lane_head))
                  + add_ref[two, gsl] for gsl in gsls]
        for gi in groups:
            h_ref[:, gsls[gi]] = allres[gi][C:2 * C, :]
        ys.append(jnp.concatenate([allres[gi][0:C, :] for gi in groups], axis=1))
    y = jnp.concatenate(ys, axis=0)
    yc = y - _seg_sum(y, bd) * (1.0 / RW_DH)
    var = _seg_sum(yc * yc, bd) * (1.0 / RW_DH)
    yn = yc * lax.rsqrt(var + RW_LNX_EPS) * lnw_ref[...] + lnb_ref[...]
    y_ref[...] = ((yn + bonus_ref[...]) * g_ref[...]).astype(BF16)


def _rw_scan(lhs, add, bonus, g, lnx_w, lnx_b):
    S = bonus.shape[0]
    C = RW_CHUNK * _pick(S // RW_CHUNK, (4, 2, 1))
    row = pl.BlockSpec((C, RW_W), lambda c: (c, 0))
    two = pl.BlockSpec((2 * C, RW_W), lambda c: (c, 0))
    vec = pl.BlockSpec((1, RW_W), lambda c: (0, 0))
    return pl.pallas_call(
        _rw_scan_kernel,
        grid=(S // C,),
        in_specs=[two, two, row, row, vec, vec, pl.BlockSpec((128, 128), lambda c: (0, 0))],
        out_specs=row,
        out_shape=jax.ShapeDtypeStruct((S, RW_W), BF16),
        scratch_shapes=[pltpu.VMEM((RW_DH, RW_W), F32)],
        compiler_params=_cp("arbitrary"),
        name="rw_scan",
    )(lhs, add, bonus, g, lnx_w, lnx_b, _head_ones())


def _pad_lora_cols(x, axis):
    W3 = 3 * RW_W
    parts = [lax.slice_in_dim(x, 0, W3, axis=axis),
             lax.slice_in_dim(x, W3, W3 + RW_DECAY_LORA, axis=axis),
             lax.slice_in_dim(x, W3 + RW_DECAY_LORA, W3 + RW_DECAY_LORA + RW_A_LORA, axis=axis),
             lax.slice_in_dim(x, W3 + RW_DECAY_LORA + RW_A_LORA, RW_COLS, axis=axis)]

    def padto(p, n):
        cfg = [(0, 0)] * x.ndim
        cfg[axis] = (0, n - p.shape[axis])
        return jnp.pad(p, cfg)

    return jnp.concatenate([parts[0], padto(parts[1], LORA_PAD), padto(parts[2], LORA_PAD), parts[3]], axis=axis)


def _rwkv_mixer(u, mu, w0, w2, a0, a2, g2, k_k, k_a, r_k, lnx_w, lnx_b):
    row = lambda p: p.reshape(1, -1).astype(F32)
    padrows = lambda m: jnp.pad(m, ((0, LORA_PAD - m.shape[0]), (0, 0))).astype(BF16)
    r, lw, k, v, a, b, g, bonus = _rw_prep(
        u, _pad_lora_cols(row(mu), 1), row(w0), padrows(w2), row(a0), padrows(a2), g2.astype(BF16),
        row(k_k), row(k_a), row(r_k))
    lhs, add = _rw_chunks(r, lw, k, v, a, b)
    return _rw_scan(lhs, add, bonus, g, row(lnx_w), row(lnx_b))


def _lru_kernel(u_ref, cw_ref, cb_ref, wa_ref, ba_ref, wx_ref, bx_ref, sp_ref, y_ref, buf_ref, a_sc, b_sc, h_sc):
    i = pl.program_id(0)
    tm = u_ref.shape[0]

    @pl.when(i == 0)
    def _():
        buf_ref[0:8, :] = jnp.zeros((8, LRU_W), F32)
        h_sc[...] = jnp.zeros_like(h_sc)

    xb = u_ref[:, LRU_W:]
    buf_ref[8:8 + tm, :] = xb
    xc = cb_ref[...] + cw_ref[CONV_W - 1:CONV_W, :] * xb
    for kk in range(CONV_W - 1):
        xc = xc + cw_ref[kk:kk + 1, :] * buf_ref[pl.ds(8 - (CONV_W - 1) + kk, tm), :]
    buf_ref[0:8, :] = xb[tm - 8:tm, :]

    xcb = xc.astype(BF16)
    ra, ri = [], []
    for n in range(LRU_BLOCKS):
        sl = slice(n * LRU_BW, (n + 1) * LRU_BW)
        ra.append(_dot(xcb[:, sl], wa_ref[n]))
        ri.append(_dot(xcb[:, sl], wx_ref[n]))
    rg = _sigmoid(jnp.concatenate(ra, axis=1) + ba_ref[...])
    ig = _sigmoid(jnp.concatenate(ri, axis=1) + bx_ref[...])
    log_a = -LRU_C * rg * sp_ref[...]
    a_sc[...] = jnp.exp(log_a)
    b_sc[...] = jnp.sqrt(1.0 - jnp.exp(2.0 * log_a)) * ig * xc

    def step(t, h):
        h = a_sc[pl.ds(t, 1), :] * h + b_sc[pl.ds(t, 1), :]
        b_sc[pl.ds(t, 1), :] = h
        return h

    h_sc[...] = lax.fori_loop(0, tm, step, h_sc[...], unroll=8)
    gate = u_ref[:, 0:LRU_W]
    gelu = 0.5 * gate * (1.0 + jnp.tanh(0.7978845608028654 * (gate + 0.044715 * gate * gate * gate)))
    y_ref[...] = (b_sc[...] * gelu).astype(BF16)


def _lru_mixer(u, conv_w, conv_b, wa, ba, wx, bx, lam):
    S = u.shape[0]
    tm = _pick(S, (256, 128, 64))
    row = lambda p: p.reshape(1, -1).astype(F32)
    vec = pl.BlockSpec((1, LRU_W), lambda i: (0, 0))
    blk = pl.BlockSpec((LRU_BLOCKS, LRU_BW, LRU_BW), lambda i: (0, 0, 0))
    lamf = lam.astype(F32)
    softplus_neg_lam = row(jnp.maximum(-lamf, 0.0) + jnp.log1p(jnp.exp(-jnp.abs(lamf))))
    cw = jnp.concatenate([conv_w.astype(F32), jnp.zeros((8 - CONV_W, LRU_W), F32)], axis=0)
    return pl.pallas_call(
        _lru_kernel,
        grid=(S // tm,),
        in_specs=[pl.BlockSpec((tm, 2 * LRU_W), lambda i: (i, 0)), pl.BlockSpec((8, LRU_W), lambda i: (0, 0)),
                  vec, blk, vec, blk, vec, vec],
        out_specs=pl.BlockSpec((tm, LRU_W), lambda i: (i, 0)),
        out_shape=jax.ShapeDtypeStruct((S, LRU_W), BF16),
        scratch_shapes=[pltpu.VMEM((tm + 8, LRU_W), F32), pltpu.VMEM((tm, LRU_W), F32),
                        pltpu.VMEM((tm, LRU_W), F32), pltpu.VMEM((1, LRU_W), F32)],
        compiler_params=_cp("arbitrary"),
        name="rglru",
    )(u, cw, row(conv_b), wa.astype(BF16), row(ba), wx.astype(BF16), row(bx), softplus_neg_lam)


def _hgrn_kernel(u_ref, lower_ref, ng_ref, y_ref, st_ref):
    C = HG_CHUNK
    W = HG_KW

    @pl.when(pl.program_id(0) == 0)
    def _():
        st_ref[...] = jnp.zeros_like(st_ref)

    for cb in range(u_ref.shape[0] // C):
        _hgrn_chunk(u_ref, lower_ref, ng_ref, y_ref, st_ref, cb * C)


def _hgrn_chunk(u_ref, lower_ref, ng_ref, y_ref, st_ref, r0):
    C = HG_CHUNK
    W = HG_KW
    rs = slice(r0, r0 + C)
    lower = lower_ref[...]
    forget = lower + (1.0 - lower) * _sigmoid(u_ref[rs, W:2 * W])
    lf = jnp.log(forget)
    kk = 1.0 - forget
    qx = u_ref[rs, 0:W]
    q = qx * _sigmoid(qx)
    v = u_ref[rs, 2 * W:3 * W]
    gx = u_ref[rs, 3 * W:4 * W]

    ti = lax.broadcasted_iota(jnp.int32, (C, C), 0)
    si = lax.broadcasted_iota(jnp.int32, (C, C), 1)
    b = _dot_hi((si <= ti).astype(F32), lf)
    blast = b[C - 1:C, :]
    q_in = (q * jnp.exp(b)).astype(BF16)
    k_end = (kk * jnp.exp(blast - b)).astype(BF16)
    vb = v.astype(BF16)
    rowi = lax.broadcasted_iota(jnp.int32, (C, W), 0)

    n_sub = C // HG_SUB
    refs = [jnp.zeros((1, W), F32)] + [b[i * HG_SUB - 1:i * HG_SUB, :] for i in range(1, n_sub)]
    ref_rows = jnp.concatenate([jnp.broadcast_to(r_, (HG_SUB, W)) for r_ in refs], axis=0)
    q_sub = (q * jnp.exp(b - ref_rows)).astype(BF16)
    k_sub = []
    for i in range(n_sub):
        live = rowi < (i + 1) * HG_SUB
        k_sub.append(jnp.where(live, kk * jnp.exp(jnp.where(live, refs[i] - b, 0.0)), 0.0).astype(BF16))
    k_sub = jnp.concatenate(k_sub, axis=0)
    row_blk = ti // HG_SUB

    heads = range(HG_HEADS)
    sls = [slice(h * HG_DK, (h + 1) * HG_DK) for h in heads]
    states = [st_ref[:, sl] for sl in sls]
    inter = [_dot_nt(q_in[:, sls[h]], states[h].astype(BF16)) for h in heads]
    pair = [_dot_nt(q_sub[:, sl], k_sub[:, sl]) for sl in sls]
    att = []
    for h in heads:
        a_h = jnp.zeros((C, C), F32)
        for i in range(n_sub):
            a_h = jnp.where((row_blk == i) & (si <= ti), pair[h][:, i * C:(i + 1) * C], a_h)
        att.append(a_h.astype(BF16))
    intra = [_dot(att[h], vb[:, sls[h]]) for h in heads]
    upd = [_dot_tn(vb[:, sl], k_end[:, sl]) for sl in sls]
    for h in heads:
        sl = sls[h]
        st_ref[:, sl] = jnp.exp(blast[:, sl]) * states[h] + upd[h]
        o = inter[h] + intra[h]
        ms = jnp.mean(o * o, axis=-1, keepdims=True)
        on = o * lax.rsqrt(ms + NORM_EPS) * ng_ref[:, sl]
        gh = gx[:, sl]
        y_ref[rs, sl] = (on * (gh * _sigmoid(gh))).astype(BF16)


def _hgrn_mixer(u, lower, norm_g):
    S = u.shape[0]
    C = HG_CHUNK * _pick(S // HG_CHUNK, (4, 2, 1))
    vec = pl.BlockSpec((1, HG_KW), lambda c: (0, 0))
    return pl.pallas_call(
        _hgrn_kernel,
        grid=(S // C,),
        in_specs=[pl.BlockSpec((C, 4 * HG_KW), lambda c: (c, 0)), vec, vec],
        out_specs=pl.BlockSpec((C, HG_VW), lambda c: (c, 0)),
        out_shape=jax.ShapeDtypeStruct((S, HG_VW), BF16),
        scratch_shapes=[pltpu.VMEM((HG_DV, HG_KW), F32)],
        compiler_params=_cp("arbitrary"),
        name="hgrn2",
    )(u, lower.reshape(1, -1).astype(F32), norm_g.reshape(1, -1).astype(F32))


def kernel(x, norm_mix, norm_mlp, w_ff1, w_ff2, w_in_a, w_out_a, nsa_qk_gain, nsa_cmp_w, nsa_cmp_pe, rw_mu, rw_w0, rw_w2, rw_a0, rw_a2, rw_g2, rw_k_k, rw_k_a, rw_r_k, rw_lnx_w, rw_lnx_b, w_in_b, w_out_b, lru_conv_w, lru_conv_b, lru_wa, lru_ba, lru_wx, lru_bx, lru_lambda, hg_lb, hg_norm):
    B, S, D = x.shape
    depth = norm_mix.shape[0]
    lb_p = jax.nn.softmax(hg_lb.astype(F32), axis=0)
    lb_cum = jnp.cumsum(lb_p, axis=0)
    hg_lower = lb_cum - lb_cum[0:1]

    outs = []
    for bi in range(B):
        xb = x[bi].astype(F32)
        for layer in range(depth):
            gmix = norm_mix[layer].astype(F32)
            if layer % 2 == 0:
                e = layer // 2
                w_in = w_in_a[e]
                w_nsa = jnp.pad(w_in[:, :NSA_COLS], ((0, 0), (0, NSA_COLS_PAD - NSA_COLS))).astype(BF16)
                w_rw = _pad_lora_cols(w_in[:, NSA_COLS:], 1).astype(BF16)
                y_a = _nsa_mixer(_norm_matmul(xb, gmix, w_nsa), nsa_qk_gain[e], nsa_cmp_w[e], nsa_cmp_pe[e])
                y_b = _rwkv_mixer(_norm_matmul(xb, gmix, w_rw), rw_mu[e], rw_w0[e], rw_w2[e], rw_a0[e], rw_a2[e],
                                  rw_g2[e], rw_k_k[e], rw_k_a[e], rw_r_k[e], rw_lnx_w[e], rw_lnx_b[e])
                xb = _out_proj(y_a, y_b, w_out_a[e].astype(BF16), xb)
            else:
                o = layer // 2
                w_in = w_in_b[o].astype(BF16)
                y_c = _lru_mixer(_norm_matmul(xb, gmix, w_in[:, :2 * LRU_W]), lru_conv_w[o], lru_conv_b[o],
                                 lru_wa[o], lru_ba[o], lru_wx[o], lru_bx[o], lru_lambda[o])
                y_d = _hgrn_mixer(_norm_matmul(xb, gmix, w_in[:, 2 * LRU_W:]), hg_lower[layer], hg_norm[o])
                xb = _out_proj(y_c, y_d, w_out_b[o].astype(BF16), xb)
            xb = _mlp(xb, norm_mlp[layer].astype(F32), w_ff1, w_ff2, layer)
        outs.append(xb)
    return jnp.stack(outs, axis=0).astype(x.dtype)
```

```python
import functools

import jax
import jax.numpy as jnp
import numpy as np
from jax import lax
from jax.experimental import pallas as pl
from jax.experimental.pallas import tpu as pltpu

F32 = jnp.float32
BF16 = jnp.bfloat16
HI = lax.Precision.HIGHEST

NORM_EPS = 1e-6
NEG_INF = -1e30
ROPE_THETA = 10000.0
LOG2E = 1.4426950408889634

NSA_HEADS = 8
NSA_KV_HEADS = 2
NSA_GROUP = NSA_HEADS // NSA_KV_HEADS
NSA_DH = 128
CMP_LEN = 32
CMP_STRIDE = 16
SLC_LEN = 64
SLC_TOPN = 16
WIN = 512
Q_BLOCK = 128
SLC_FORCE = 1e4
NSA_W = NSA_HEADS * NSA_DH
NSA_KV_W = NSA_KV_HEADS * NSA_DH
NSA_COLS = NSA_W + 6 * NSA_KV_W + 3 * NSA_HEADS
NSA_COLS_PAD = NSA_W + 6 * NSA_KV_W + 128
SLC_TILE = 512
NSA_QPAIR = 2

RW_HEADS = 16
RW_DH = 64
RW_W = RW_HEADS * RW_DH
RW_DECAY_LORA = 96
RW_A_LORA = 96
RW_G_LORA = 256
RW_LNX_EPS = 64e-5
RW_COLS = 3 * RW_W + RW_DECAY_LORA + RW_A_LORA + RW_G_LORA
LORA_PAD = 128
RW_COLS_PAD = 3 * RW_W + 2 * LORA_PAD + RW_G_LORA
RW_CHUNK = 64
RW_PACK = 4

LRU_W = 1024
LRU_BLOCKS = 8
LRU_BW = LRU_W // LRU_BLOCKS
CONV_W = 4
LRU_C = 8.0

HG_HEADS = 8
HG_DK = 128
HG_DV = 128
HG_KW = HG_HEADS * HG_DK
HG_VW = HG_HEADS * HG_DV
HG_CHUNK = 64
HG_SUB = 16

MXU_WIDTH = 256
VMEM_LIMIT = 56 * 1024 * 1024


def _cp(*sem, flags=None):
    return pltpu.CompilerParams(dimension_semantics=sem, vmem_limit_bytes=VMEM_LIMIT, flags=flags)


def _pick(n, cands):
    for c in cands:
        if n % c == 0:
            return c
    return n


def _sigmoid(x):
    return 1.0 / (1.0 + jnp.exp(-x))


def _softplus(x):
    return jnp.maximum(x, 0.0) + jnp.log(1.0 + jnp.exp(-jnp.abs(x)))


def _split3(x):
    p1 = x.astype(BF16)
    r1 = x - p1.astype(F32)
    p2 = r1.astype(BF16)
    p3 = (r1 - p2.astype(F32)).astype(BF16)
    return p1, p2, p3


def _dot(a, b):
    return jnp.dot(a, b, preferred_element_type=F32)


def _dot_hi(a, b):
    return jnp.dot(a, b, preferred_element_type=F32, precision=HI)


def _dot_nt(a, b, precision=None):
    return lax.dot_general(a, b, (((1,), (1,)), ((), ())), preferred_element_type=F32, precision=precision)


def _dot_tn(a, b, precision=None):
    return lax.dot_general(a, b, (((0,), (0,)), ((), ())), preferred_element_type=F32, precision=precision)


def _norm_matmul_kernel(x_ref, g_ref, w_ref, o_ref, *, tn):
    x = x_ref[...]
    ms = jnp.mean(x * x, axis=-1, keepdims=True)
    hn = (x * lax.rsqrt(ms + NORM_EPS) * g_ref[...]).astype(BF16)
    n = w_ref.shape[1]
    for lo in range(0, n, tn):
        hi = min(lo + tn, n)
        o_ref[:, lo:hi] = _dot(hn, w_ref[:, lo:hi])


def _norm_matmul(x, g, w):
    S, D = x.shape
    N = w.shape[1]
    tm = _pick(S, (512, 256, 128))
    tn = 3 * MXU_WIDTH
    return pl.pallas_call(
        functools.partial(_norm_matmul_kernel, tn=tn),
        grid=(S // tm,),
        in_specs=[
            pl.BlockSpec((tm, D), lambda i: (i, 0)),
            pl.BlockSpec((1, D), lambda i: (0, 0)),
            pl.BlockSpec((D, N), lambda i: (0, 0), pipeline_mode=pl.Buffered(1)),
        ],
        out_specs=pl.BlockSpec((tm, N), lambda i: (i, 0)),
        out_shape=jax.ShapeDtypeStruct((S, N), F32),
        compiler_params=_cp("parallel"),
        name="norm_matmul",
    )(x, g.reshape(1, D), w)


def _matmul_kernel(a_ref, w_ref, o_ref):
    o_ref[...] = _dot(a_ref[...], w_ref[...])


def _matmul(a, w):
    M, K = a.shape
    N = w.shape[1]
    tm = _pick(M, (256, 128, 64, 32, 16, 8))
    return pl.pallas_call(
        _matmul_kernel,
        grid=(M // tm,),
        in_specs=[pl.BlockSpec((tm, K), lambda i: (i, 0)), pl.BlockSpec((K, N), lambda i: (0, 0))],
        out_specs=pl.BlockSpec((tm, N), lambda i: (i, 0)),
        out_shape=jax.ShapeDtypeStruct((M, N), F32),
        compiler_params=_cp("parallel"),
        name="matmul",
    )(a, w)


def _out_proj_kernel(a_ref, b_ref, wa_ref, wb_ref, r_ref, o_ref):
    o_ref[...] = r_ref[...] + _dot(a_ref[...], wa_ref[...]) + _dot(b_ref[...], wb_ref[...])


def _out_proj(ya, yb, w, res):
    S, Wa = ya.shape
    Wb = yb.shape[1]
    D = w.shape[1]
    tm = _pick(S, (512, 256, 128))
    return pl.pallas_call(
        _out_proj_kernel,
        grid=(S // tm,),
        in_specs=[
            pl.BlockSpec((tm, Wa), lambda i: (i, 0)),
            pl.BlockSpec((tm, Wb), lambda i: (i, 0)),
            pl.BlockSpec((Wa, D), lambda i: (0, 0)),
            pl.BlockSpec((Wb, D), lambda i: (0, 0)),
            pl.BlockSpec((tm, D), lambda i: (i, 0)),
        ],
        out_specs=pl.BlockSpec((tm, D), lambda i: (i, 0)),
        out_shape=jax.ShapeDtypeStruct((S, D), F32),
        compiler_params=_cp("parallel"),
        name="out_proj",
    )(ya, yb, w[:Wa], w[Wa:], res)


def _mlp_kernel(x_ref, g_ref, w1_ref, w2_ref, o_ref, hn_ref):
    f = pl.program_id(1)

    @pl.when(f == 0)
    def _():
        x = x_ref[...]
        ms = jnp.mean(x * x, axis=-1, keepdims=True)
        hn_ref[...] = (x * lax.rsqrt(ms + NORM_EPS) * g_ref[...]).astype(BF16)
        o_ref[...] = x

    hn = hn_ref[...]
    half = w1_ref.shape[1] // 2
    zs = [jnp.maximum(_dot(hn, w1_ref[:, c * half:(c + 1) * half]), 0.0) for c in range(2)]
    acc = _dot((zs[0] * zs[0]).astype(BF16), w2_ref[0:half, :])
    acc = acc + _dot((zs[1] * zs[1]).astype(BF16), w2_ref[half:2 * half, :])
    o_ref[...] += acc


def _mlp(x, g, w1, w2, layer):
    S, D = x.shape
    Fdim = w1.shape[2]
    tm = _pick(S, (1024, 512, 256, 128))
    tf = _pick(Fdim, (512, 256, 128))
    return pl.pallas_call(
        _mlp_kernel,
        grid=(S // tm, Fdim // tf),
        in_specs=[
            pl.BlockSpec((tm, D), lambda i, f: (i, 0)),
            pl.BlockSpec((1, D), lambda i, f: (0, 0)),
            pl.BlockSpec((None, D, tf), lambda i, f: (layer, 0, f)),
            pl.BlockSpec((None, tf, D), lambda i, f: (layer, f, 0)),
        ],
        out_specs=pl.BlockSpec((tm, D), lambda i, f: (i, 0)),
        out_shape=jax.ShapeDtypeStruct((S, D), F32),
        scratch_shapes=[pltpu.VMEM((tm, D), BF16)],
        compiler_params=_cp("parallel", "arbitrary"),
        name="mlp",
    )(x, g.reshape(1, D), w1, w2)


def _head_norm_rope(x, gain, cosf, sinf):
    ms = jnp.mean(x * x, axis=-1, keepdims=True)
    y = x * lax.rsqrt(ms + NORM_EPS) * gain
    return y * cosf + pltpu.roll(y, NSA_DH // 2, 1) * sinf


def _nsa_prep_kernel(u_ref, gain_ref, cos_ref, sin_ref,
                     q_ref, kc_ref, vc_ref, ks_ref, vs_ref, kw_ref, vw_ref, gate_ref):
    cosf = cos_ref[...]
    sinf = sin_ref[...]
    scale = NSA_DH ** -0.5 * LOG2E
    for h in range(NSA_HEADS):
        sl = slice(h * NSA_DH, (h + 1) * NSA_DH)
        q_ref[:, sl] = (_head_norm_rope(u_ref[:, sl], gain_ref[0:1, :], cosf, sinf) * scale).astype(BF16)
    base = NSA_W
    kc_ref[...] = u_ref[:, base:base + NSA_KV_W].astype(BF16)
    vc_ref[...] = u_ref[:, base + NSA_KV_W:base + 2 * NSA_KV_W].astype(BF16)
    vs_ref[...] = u_ref[:, base + 3 * NSA_KV_W:base + 4 * NSA_KV_W].astype(BF16)
    vw_ref[...] = u_ref[:, base + 5 * NSA_KV_W:base + 6 * NSA_KV_W].astype(BF16)
    for g in range(NSA_KV_HEADS):
        sl = slice(g * NSA_DH, (g + 1) * NSA_DH)
        o_s = base + 2 * NSA_KV_W + g * NSA_DH
        o_w = base + 4 * NSA_KV_W + g * NSA_DH
        ks_ref[:, sl] = _head_norm_rope(u_ref[:, o_s:o_s + NSA_DH], gain_ref[2:3, :], cosf, sinf).astype(BF16)
        kw_ref[:, sl] = _head_norm_rope(u_ref[:, o_w:o_w + NSA_DH], gain_ref[3:4, :], cosf, sinf).astype(BF16)
    gate_ref[...] = _sigmoid(u_ref[:, base + 6 * NSA_KV_W:base + 6 * NSA_KV_W + 128])


def _nsa_proj_prep_kernel(x_ref, g_ref, w_ref, gain_ref, cos_ref, sin_ref, *rest):
    outs, u_scr = rest[:-1], rest[-1]
    _norm_matmul_kernel(x_ref, g_ref, w_ref, u_scr, tn=3 * MXU_WIDTH)
    _nsa_prep_kernel(u_scr, gain_ref, cos_ref, sin_ref, *outs)


def _nsa_proj_prep(x, g, w, gain8, cosf, sinf):
    S, D = x.shape
    tm = _pick(S, (512, 256, 128))
    row = lambda w_: pl.BlockSpec((tm, w_), lambda i: (i, 0))
    kv = jax.ShapeDtypeStruct((S, NSA_KV_W), BF16)
    return pl.pallas_call(
        _nsa_proj_prep_kernel,
        grid=(S // tm,),
        in_specs=[row(D), pl.BlockSpec((1, D), lambda i: (0, 0)),
                  pl.BlockSpec((D, NSA_COLS_PAD), lambda i: (0, 0), pipeline_mode=pl.Buffered(1)),
                  pl.BlockSpec((8, NSA_DH), lambda i: (0, 0)), row(NSA_DH), row(NSA_DH)],
        out_specs=[row(NSA_W)] + [row(NSA_KV_W)] * 6 + [row(128)],
        out_shape=[jax.ShapeDtypeStruct((S, NSA_W), BF16)] + [kv] * 6 + [jax.ShapeDtypeStruct((S, 128), F32)],
        scratch_shapes=[pltpu.VMEM((tm, NSA_COLS_PAD), F32)],
        compiler_params=_cp("parallel"),
        name="nsa_proj_prep",
    )(x, g.reshape(1, D), w, gain8, cosf, sinf)


def _nsa_cmp_finish_kernel(ak_ref, av_ref, pek_ref, pev_ref, gain_ref, cos_ref, sin_ref,
                           kcmp_ref, vcmp_ref, buf_ref):
    n = ak_ref.shape[0]
    buf_ref[n:n + 8, :] = jnp.zeros((8, NSA_KV_W), F32)

    def combine(a_ref, pe_ref):
        buf_ref[0:n, :] = a_ref[:, NSA_KV_W:]
        pe = pe_ref[0:1, 0:NSA_KV_W] + pe_ref[1:2, NSA_KV_W:]
        return a_ref[:, 0:NSA_KV_W] + buf_ref[pl.ds(1, n), :] + pe

    kc = combine(ak_ref, pek_ref)
    for g in range(NSA_KV_HEADS):
        sl = slice(g * NSA_DH, (g + 1) * NSA_DH)
        kcmp_ref[:, sl] = _head_norm_rope(kc[:, sl], gain_ref[1:2, :], cos_ref[...], sin_ref[...]).astype(BF16)
    vcmp_ref[...] = combine(av_ref, pev_ref).astype(BF16)


def _nsa_compress(kc, vc, cmp_w, cmp_pe, gain8, ccos, csin):
    S = kc.shape[0]
    n = S // CMP_STRIDE
    half = CMP_LEN // 2
    eye = jnp.eye(NSA_KV_HEADS, dtype=F32)

    def expand(w):
        lo = jnp.einsum('lde,gh->lgdhe', w[:half], eye).reshape(half * NSA_KV_W, NSA_KV_W)
        hi = jnp.einsum('lde,gh->lgdhe', w[half:], eye).reshape(half * NSA_KV_W, NSA_KV_W)
        return jnp.concatenate([lo, hi], axis=1).astype(BF16)

    def expand_pe(pe):
        lo = jnp.broadcast_to(pe[:half, None, :], (half, NSA_KV_HEADS, NSA_DH)).reshape(1, -1)
        hi = jnp.broadcast_to(pe[half:, None, :], (half, NSA_KV_HEADS, NSA_DH)).reshape(1, -1)
        return jnp.concatenate([lo, hi, jnp.zeros((6, lo.shape[1]), F32)], axis=0).astype(BF16)

    wk, wv = expand(cmp_w[0]), expand(cmp_w[1])
    ak = _matmul(kc.reshape(n, CMP_STRIDE * NSA_KV_W), wk)
    av = _matmul(vc.reshape(n, CMP_STRIDE * NSA_KV_W), wv)
    pek = _matmul(expand_pe(cmp_pe[0]), wk)
    pev = _matmul(expand_pe(cmp_pe[1]), wv)
    full = lambda shp: pl.BlockSpec(shp, lambda i: (0, 0))
    return pl.pallas_call(
        _nsa_cmp_finish_kernel,
        grid=(1,),
        in_specs=[full((n, 2 * NSA_KV_W)), full((n, 2 * NSA_KV_W)), full((8, 2 * NSA_KV_W)), full((8, 2 * NSA_KV_W)),
                  full((8, NSA_DH)), full((n, NSA_DH)), full((n, NSA_DH))],
        out_specs=[full((n, NSA_KV_W)), full((n, NSA_KV_W))],
        out_shape=[jax.ShapeDtypeStruct((n, NSA_KV_W), BF16)] * 2,
        scratch_shapes=[pltpu.VMEM((n + 8, NSA_KV_W), F32)],
        compiler_params=_cp("arbitrary"),
        name="nsa_cmp_finish",
    )(ak, av, pek, pev, gain8, ccos, csin)


def _nsa_attn_kernel(q_ref, kcmp_ref, vcmpT_ref, ks_ref, vsT_ref, kw_ref, vwT_ref, gate_ref, khot_ref,
                     o_ref, qT_ref, psum_ref, sel_ref, oc_ref, m_ref, l_ref, acc_ref, s_ref, *, top_n):
    g = pl.program_id(0)
    pb = pl.program_id(1)
    rows = NSA_GROUP * Q_BLOCK
    n_cmp = kcmp_ref.shape[0]
    n_slc = sel_ref.shape[1]
    streams = range(NSA_QPAIR)
    t0s = [(pb * NSA_QPAIR + i) * Q_BLOCK for i in streams]

    lane = lax.broadcasted_iota(jnp.int32, (1, rows), 1)
    q_io = lax.broadcasted_iota(jnp.int32, (1, Q_BLOCK), 1)
    n_io = lax.broadcasted_iota(jnp.int32, (n_cmp, 1), 0)
    m_io = lax.broadcasted_iota(jnp.int32, (n_slc, 1), 0)
    for i in streams:
        t0 = t0s[i]
        t_lane = t0 + (lane & (Q_BLOCK - 1))
        qf = q_ref[i * Q_BLOCK:(i + 1) * Q_BLOCK, :].astype(F32)
        qT = jnp.concatenate([qf[:, j * NSA_DH:(j + 1) * NSA_DH].T for j in range(NSA_GROUP)], axis=1).astype(BF16)
        qT_ref[i, 0:NSA_DH, :] = qT
        qT_ref[i, NSA_DH:2 * NSA_DH, :] = jnp.zeros((NSA_DH, rows), BF16)

        s = _dot(kcmp_ref[...], qT)
        valid = (n_io * CMP_STRIDE + (CMP_LEN - 1)) <= t_lane
        sm = jnp.where(valid, s, NEG_INF)
        mx = jnp.max(sm, axis=0, keepdims=True)
        e = jnp.where(valid, jnp.exp2(sm - mx), 0.0)
        den = jnp.sum(e, axis=0, keepdims=True)
        p = e * jnp.where(den > 0.0, 1.0 / den, 0.0)
        oc_ref[i] = _dot(vcmpT_ref[...], p.astype(BF16))
        psum = p[:, 0:Q_BLOCK]
        for j in range(1, NSA_GROUP):
            psum = psum + p[:, j * Q_BLOCK:(j + 1) * Q_BLOCK]
        psum_ref[i, 0:8, :] = jnp.zeros((8, Q_BLOCK), F32)
        psum_ref[i, 8:8 + n_cmp, :] = psum
        ratio = SLC_LEN // CMP_STRIDE
        imp = 0.5 * (psum_ref[i, pl.ds(7, n_slc, stride=ratio), :] + psum_ref[i, pl.ds(8 + ratio - 1, n_slc, stride=ratio), :])
        for c in range(ratio - 1):
            imp = imp + psum_ref[i, pl.ds(8 + c, n_slc, stride=ratio), :]

        cur = jnp.right_shift(t0 + q_io, 6)
        forced = (m_io == 0) | (m_io == cur) | (m_io == cur - 1)
        score = jnp.where(forced, SLC_FORCE, jnp.where(m_io <= cur, imp, -1.0))
        for _ in range(top_n):
            best = jnp.max(score, axis=0, keepdims=True)
            idx = jnp.min(jnp.where(score == best, m_io, n_slc), axis=0, keepdims=True)
            score = jnp.where(m_io == idx, -3e38, score)
        sel_ref[i] = jnp.where(score < -1e38, 0.0, NEG_INF)

    m_ref[...] = jnp.full(m_ref.shape, NEG_INF, F32)
    l_ref[...] = jnp.zeros(l_ref.shape, F32)
    acc_ref[...] = jnp.zeros(acc_ref.shape, F32)
    blocks_per_tile = SLC_TILE // SLC_LEN

    sub = SLC_TILE // 4

    def scores_into(kt, slot):
        k_tile = ks_ref[pl.ds(pl.multiple_of(kt * SLC_TILE, SLC_TILE), SLC_TILE), :]
        k_aug = jnp.concatenate([k_tile, khot_ref[slot]], axis=1)
        band = NSA_DH + 16 * slot
        for i in streams:
            selt = sel_ref[i, pl.ds(pl.multiple_of(kt * blocks_per_tile, blocks_per_tile), blocks_per_tile), :]
            rows8 = jnp.concatenate([selt] * NSA_GROUP, axis=1)
            qT_ref[i, band:band + 16, :] = jnp.concatenate([rows8, jnp.zeros_like(rows8)], axis=0).astype(BF16)
        for i in streams:
            for c in range(4):
                s_ref[i, slot, c * sub:(c + 1) * sub, :] = _dot(k_aug[c * sub:(c + 1) * sub, :], qT_ref[i])

    def absorb(kt, slot, causal):
        v_tile = vsT_ref[kt]
        for i in streams:
            sm = s_ref[i, slot]
            if causal:
                kpos = kt * SLC_TILE + lax.broadcasted_iota(jnp.int32, (SLC_TILE, rows), 0)
                sm = jnp.where(kpos <= t0s[i] + (lane & (Q_BLOCK - 1)), sm, NEG_INF)
            m_old = m_ref[i]
            m_new = jnp.maximum(m_old, jnp.max(sm, axis=0, keepdims=True))
            p = jnp.exp2(sm - m_new)
            alpha = jnp.exp2(m_old - m_new)
            l_ref[i] = alpha * l_ref[i] + jnp.sum(p, axis=0, keepdims=True)
            acc_ref[i] = alpha * acc_ref[i] + _dot(v_tile, p.astype(BF16))
            m_ref[i] = m_new

    last_kt = (t0s[0] + Q_BLOCK - 1) // SLC_TILE
    scores_into(0, 0)

    def slc_pair(kk, carry):
        kt = 2 * kk
        scores_into(kt + 1, 1)
        absorb(kt, 0, False)
        scores_into(kt + 2, 0)
        absorb(kt + 1, 1, False)
        return carry

    lax.fori_loop(0, last_kt // 2, slc_pair, 0)

    @pl.when(last_kt % 2 == 0)
    def _():
        absorb(last_kt, 0, True)

    @pl.when(last_kt % 2 == 1)
    def _():
        scores_into(last_kt, 1)
        absorb(last_kt - 1, 0, False)
        absorb(last_kt, 1, True)

    gT = gate_ref[...].T
    wlen = WIN + Q_BLOCK
    for i in streams:
        t0 = t0s[i]
        t_lane = t0 + (lane & (Q_BLOCK - 1))
        o_s = acc_ref[i] * (1.0 / l_ref[i])

        start = pl.multiple_of(jnp.maximum(t0 - WIN, 0), Q_BLOCK)
        s = _dot(kw_ref[pl.ds(start, wlen), :], qT_ref[i, 0:NSA_DH, :])
        kpos = start + lax.broadcasted_iota(jnp.int32, (wlen, 1), 0)
        d = t_lane - kpos
        mask = (d >= 0) & (d < WIN)
        sm = jnp.where(mask, s, NEG_INF)
        mx = jnp.max(sm, axis=0, keepdims=True)
        e = jnp.where(mask, jnp.exp2(sm - mx), 0.0)
        p = (e * (1.0 / jnp.sum(e, axis=0, keepdims=True))).astype(BF16)
        tile0 = start // Q_BLOCK
        o_w = _dot(vwT_ref[tile0], p[0:Q_BLOCK, :])
        for w in range(1, wlen // Q_BLOCK):
            o_w = o_w + _dot(vwT_ref[tile0 + w], p[w * Q_BLOCK:(w + 1) * Q_BLOCK, :])

        for j in range(NSA_GROUP):
            sl = slice(j * Q_BLOCK, (j + 1) * Q_BLOCK)

            def grow(b):
                r0 = b * NSA_HEADS + j
                r1 = b * NSA_HEADS + NSA_GROUP + j
                cols = slice(i * Q_BLOCK, (i + 1) * Q_BLOCK)
                return jnp.where(g == 0, gT[r0:r0 + 1, cols], gT[r1:r1 + 1, cols])

            o = grow(0) * oc_ref[i, :, sl] + grow(1) * o_s[:, sl] + grow(2) * o_w[:, sl]
            o_ref[i * Q_BLOCK:(i + 1) * Q_BLOCK, j * NSA_DH:(j + 1) * NSA_DH] = o.T.astype(BF16)


def _nsa_attention(q, kcmp, vcmp, ks, vs, kw, vw, gate):
    S = q.shape[0]
    n_cmp = kcmp.shape[0]
    n_slc = S // SLC_LEN
    n_qb = S // Q_BLOCK
    top_n = min(SLC_TOPN, n_slc)
    rows = NSA_GROUP * Q_BLOCK

    nq = NSA_QPAIR
    assert n_qb % nq == 0 and SLC_TILE % (nq * Q_BLOCK) == 0 and n_cmp * CMP_STRIDE == n_slc * SLC_LEN

    def tiles_T(v, width):
        return v.reshape(S // width, width, NSA_KV_HEADS, NSA_DH).transpose(2, 0, 3, 1)

    blk_of_key = np.arange(SLC_TILE)[:, None] // SLC_LEN
    khot = jnp.asarray(np.stack([blk_of_key + 16 * s == np.arange(NSA_DH)[None, :] for s in range(2)]), dtype=BF16)
    vcmpT = vcmp.reshape(n_cmp, NSA_KV_HEADS, NSA_DH).transpose(1, 2, 0)
    vsT = tiles_T(vs, SLC_TILE)
    vwT = tiles_T(vw, Q_BLOCK)

    return pl.pallas_call(
        functools.partial(_nsa_attn_kernel, top_n=top_n),
        grid=(NSA_KV_HEADS, n_qb // nq),
        in_specs=[
            pl.BlockSpec((nq * Q_BLOCK, rows), lambda g, b: (b, g)),
            pl.BlockSpec((n_cmp, NSA_DH), lambda g, b: (0, g)),
            pl.BlockSpec((None, NSA_DH, n_cmp), lambda g, b: (g, 0, 0)),
            pl.BlockSpec((S, NSA_DH), lambda g, b: (0, g)),
            pl.BlockSpec((None, S // SLC_TILE, NSA_DH, SLC_TILE), lambda g, b: (g, 0, 0, 0)),
            pl.BlockSpec((S, NSA_DH), lambda g, b: (0, g)),
            pl.BlockSpec((None, S // Q_BLOCK, NSA_DH, Q_BLOCK), lambda g, b: (g, 0, 0, 0)),
            pl.BlockSpec((nq * Q_BLOCK, 128), lambda g, b: (b, 0)),
            pl.BlockSpec((2, SLC_TILE, NSA_DH), lambda g, b: (0, 0, 0)),
        ],
        out_specs=pl.BlockSpec((nq * Q_BLOCK, rows), lambda g, b: (b, g)),
        out_shape=jax.ShapeDtypeStruct((S, NSA_W), BF16),
        scratch_shapes=[
            pltpu.VMEM((nq, 2 * NSA_DH, rows), BF16),
            pltpu.VMEM((nq, n_cmp + 8, Q_BLOCK), F32),
            pltpu.VMEM((nq, n_slc, Q_BLOCK), F32),
            pltpu.VMEM((nq, NSA_DH, rows), F32),
            pltpu.VMEM((nq, 1, rows), F32),
            pltpu.VMEM((nq, 1, rows), F32),
            pltpu.VMEM((nq, NSA_DH, rows), F32),
            pltpu.VMEM((nq, 2, SLC_TILE, rows), F32),
        ],
        compiler_params=_cp("arbitrary", "arbitrary"),
        name="nsa_attn",
    )(q, kcmp, vcmpT, ks, vsT, kw, vwT, gate, khot)


def _rope_tables(pos):
    inv = ROPE_THETA ** (-(jnp.arange(0, NSA_DH, 2, dtype=F32) / NSA_DH))
    ang = pos[:, None] * inv[None, :]
    c, s = jnp.cos(ang), jnp.sin(ang)
    return jnp.concatenate([c, c], axis=-1), jnp.concatenate([-s, s], axis=-1)


def _nsa_mixer(x, g, w, qk_gain, cmp_w, cmp_pe):
    S = x.shape[0]
    gain8 = jnp.concatenate([qk_gain.astype(F32), jnp.zeros((4, NSA_DH), F32)], axis=0)
    cosf, sinf = _rope_tables(jnp.arange(S, dtype=F32))
    q, kc, vc, ks, vs, kw, vw, gate = _nsa_proj_prep(x, g, w, gain8, cosf, sinf)
    n = S // CMP_STRIDE
    ccos, csin = _rope_tables((jnp.arange(n) * CMP_STRIDE + CMP_LEN - 1).astype(F32))
    kcmp, vcmp = _nsa_compress(kc, vc, cmp_w, cmp_pe, gain8, ccos, csin)
    return _nsa_attention(q, kcmp, vcmp, ks, vs, kw, vw, gate)


def _head_ones():
    return jnp.asarray(np.kron(np.eye(128 // RW_DH), np.ones((RW_DH, RW_DH))), dtype=BF16)


def _seg_sum(x, ones_bd):
    outs = []
    for c in range(x.shape[1] // 128):
        p1, p2, p3 = _split3(x[:, c * 128:(c + 1) * 128])
        outs.append(_dot(p1, ones_bd) + _dot(p2, ones_bd) + _dot(p3, ones_bd))
    return jnp.concatenate(outs, axis=1)


def _rw_proj_prep_kernel(x_ref, gx_ref, w_ref, mu_ref, w0_ref, w2_ref, a0_ref, a2_ref, g2_ref, kk_ref, ka_ref, rk_ref,
                         bd_ref, r_ref, lw_ref, k_ref, v_ref, a_ref, b_ref, g_ref, bonus_ref, buf_ref):
    tm = x_ref.shape[0]
    W = RW_W

    @pl.when(pl.program_id(0) == 0)
    def _():
        buf_ref[0:8, :] = jnp.zeros((8, RW_COLS_PAD), F32)

    x = x_ref[...]
    ms = jnp.mean(x * x, axis=-1, keepdims=True)
    hn = (x * lax.rsqrt(ms + NORM_EPS) * gx_ref[...]).astype(BF16)

    def shifted(lo, hi):
        buf_ref[8:8 + tm, lo:hi] = _dot(hn, w_ref[:, lo:hi])
        u = buf_ref[8:8 + tm, lo:hi]
        return u + (buf_ref[pl.ds(7, tm), lo:hi] - u) * mu_ref[:, lo:hi]

    xl = shifted(3 * W, RW_COLS_PAD)
    wl, al, gl = xl[:, 0:LORA_PAD], xl[:, LORA_PAD:2 * LORA_PAD], xl[:, 2 * LORA_PAD:]
    xk = shifted(W, 2 * W)
    wx = w0_ref[...] + _dot(jnp.tanh(wl).astype(BF16), w2_ref[...])
    lw_ref[...] = -jnp.exp(-_softplus(-wx) - 0.5)
    a = _sigmoid(a0_ref[...] + _dot(al.astype(BF16), a2_ref[...]))
    g_ref[...] = _dot(_sigmoid(gl).astype(BF16), g2_ref[...])
    xr = shifted(0, W)
    bd = bd_ref[...]
    kk = xk * kk_ref[...]
    kk = kk / jnp.maximum(jnp.sqrt(_seg_sum(kk * kk, bd)), 1e-12)
    kf = xk * (1.0 + (a - 1.0) * ka_ref[...])
    k_ref[...] = kf
    a_ref[...] = -kk
    b_ref[...] = kk * a
    xv = shifted(2 * W, 3 * W)
    r_ref[...] = xr
    v_ref[...] = xv
    bonus_ref[...] = _seg_sum(xr * kf * rk_ref[...], bd) * xv
    buf_ref[0:8, :] = buf_ref[tm:tm + 8, :]


def _rw_proj_prep(x, g, w, mu, w0, w2, a0, a2, g2, k_k, k_a, r_k):
    S, D = x.shape
    tm = _pick(S, (256, 128, 64))
    W = RW_W
    vec = lambda n: pl.BlockSpec((1, n), lambda i: (0, 0))
    mat = lambda a, b: pl.BlockSpec((a, b), lambda i: (0, 0))
    row = pl.BlockSpec((tm, W), lambda i: (i, 0))
    out = jax.ShapeDtypeStruct((S, W), F32)
    return pl.pallas_call(
        _rw_proj_prep_kernel,
        grid=(S // tm,),
        in_specs=[
            pl.BlockSpec((tm, D), lambda i: (i, 0)), vec(D),
            pl.BlockSpec((D, RW_COLS_PAD), lambda i: (0, 0), pipeline_mode=pl.Buffered(1)),
            vec(RW_COLS_PAD), vec(W), mat(LORA_PAD, W), vec(W), mat(LORA_PAD, W), mat(RW_G_LORA, W),
            vec(W), vec(W), vec(W), mat(128, 128),
        ],
        out_specs=[row] * 8,
        out_shape=[out] * 8,
        scratch_shapes=[pltpu.VMEM((tm + 8, RW_COLS_PAD), F32)],
        compiler_params=_cp("arbitrary"),
        name="rw_proj_prep",
    )(x, g.reshape(1, D), w, mu, w0, w2, a0, a2, g2, k_k, k_a, r_k, _head_ones())


def _bd_rows(x, lane_head):
    lh = lane_head[0:x.shape[0], :]
    return jnp.concatenate([jnp.where(lh == h, x, jnp.zeros_like(x)) for h in range(RW_PACK)], axis=0)


def _diag_pack(full, lane_head):
    out = None
    for h in range(RW_PACK):
        blk = jnp.where(lane_head == h, full[h * RW_DH:(h + 1) * RW_DH, :], 0.0)
        out = blk if out is None else out + blk
    return out


def _rw_chunk_kernel(r_ref, lw_ref, k_ref, v_ref, a_ref, b_ref, lhs_ref, add_ref):
    for cb in range(r_ref.shape[0] // RW_CHUNK):
        _rw_chunk_body(r_ref, lw_ref, k_ref, v_ref, a_ref, b_ref, lhs_ref, add_ref, cb * RW_CHUNK)


def _rw_chunk_body(r_ref, lw_ref, k_ref, v_ref, a_ref, b_ref, lhs_ref, add_ref, r0):
    C = RW_CHUNK
    rs = slice(r0, r0 + C)
    ti = lax.broadcasted_iota(jnp.int32, (C, C), 0)
    si = lax.broadcasted_iota(jnp.int32, (C, C), 1)
    tri_incl = (si <= ti).astype(F32)
    GW = RW_PACK * RW_DH
    lane_head = lax.broadcasted_iota(jnp.int32, (C, GW), 1) // RW_DH
    lane_pos = lax.broadcasted_iota(jnp.int32, (C, GW), 1) % RW_DH
    row_t = lax.broadcasted_iota(jnp.int32, (C, GW), 0)
    eye_p = (lane_pos == row_t).astype(F32)
    strict = jnp.concatenate([lane_pos < row_t] * 2, axis=1)
    incl = jnp.concatenate([lane_pos <= row_t] * 2, axis=1)

    lw = lw_ref[rs, :]
    cs = _dot_hi(tri_incl, lw)
    cprev = cs - lw
    clast = cs[C - 1:C, :]
    e_neg = jnp.exp(-cs)
    e_end = jnp.exp(clast - cs)
    At = (a_ref[rs, :] * jnp.exp(cprev)).astype(BF16)
    Rt32 = r_ref[rs, :] * jnp.exp(cs)
    Rt = Rt32.astype(BF16)
    Bt = (b_ref[rs, :] * e_neg).astype(BF16)
    Kt = (k_ref[rs, :] * e_neg).astype(BF16)
    Bh = (b_ref[rs, :] * e_end).astype(BF16)
    Kh = (k_ref[rs, :] * e_end).astype(BF16)
    V = v_ref[rs, :].astype(BF16)
    gam = jnp.exp(clast)

    groups = range(RW_HEADS // RW_PACK)
    sls = [slice(gi * GW, (gi + 1) * GW) for gi in groups]
    bd = functools.partial(_bd_rows, lane_head=lane_head)
    G = [_dot_nt(jnp.concatenate([At[:, sl], Rt[:, sl]], axis=0),
                 jnp.concatenate([bd(Bt[:, sl]), bd(Kt[:, sl])], axis=0)) for sl in sls]
    top = [jnp.where(strict, g_[0:C, :], 0.0) for g_ in G]
    bot = [jnp.where(incl, g_[C:2 * C, :], 0.0).astype(BF16) for g_ in G]
    L = [t_[:, 0:GW] for t_ in top]
    AV = [_dot(jnp.concatenate([top[gi][:, GW:].astype(BF16), bot[gi][:, GW:]], axis=0), bd(V[:, sls[gi]])) for gi in groups]
    T = [eye_p + l_ for l_ in L]
    P = [_dot(l_.astype(BF16), bd(l_.astype(BF16))) for l_ in L]
    for _ in range(4):
        TP = [_dot(jnp.concatenate([T[gi].astype(BF16), P[gi].astype(BF16)], axis=0), bd(P[gi].astype(BF16)))
              for gi in groups]
        T = [T[gi] + TP[gi][0:C, :] for gi in groups]
        P = [tp[C:2 * C, :] for tp in TP]
    T = [T[gi] + _dot(T[gi].astype(BF16), bd(P[gi].astype(BF16))) for gi in groups]
    AU = [_dot(T[gi].astype(BF16), jnp.concatenate([bd(At[:, sls[gi]]), bd(AV[gi][0:C, :].astype(BF16))], axis=1))
          for gi in groups]
    Ahat = [au[:, 0:GW].astype(BF16) for au in AU]
    Uhat = [au[:, GW:].astype(BF16) for au in AU]
    RY = [_dot(bot[gi][:, 0:GW], jnp.concatenate([bd(Ahat[gi]), bd(Uhat[gi])], axis=1)) for gi in groups]
    phi_full = [_dot_tn(Bh[:, sls[gi]], Ahat[gi]) for gi in groups]
    hh_full = [_dot_tn(jnp.concatenate([Bh[:, sls[gi]], Kh[:, sls[gi]]], axis=0),
                       jnp.concatenate([Uhat[gi], V[:, sls[gi]]], axis=0)) for gi in groups]
    for gi in groups:
        sl = sls[gi]
        o0 = 2 * r0
        lhs_ref[o0:o0 + C, sl] = Rt32[:, sl] + RY[gi][:, 0:GW]
        add_ref[o0:o0 + C, sl] = RY[gi][:, GW:] + AV[gi][C:2 * C, :]
        lhs_ref[o0 + C:o0 + 2 * C, sl] = eye_p * gam[:, sl] + _diag_pack(phi_full[gi], lane_head)
        add_ref[o0 + C:o0 + 2 * C, sl] = _diag_pack(hh_full[gi], lane_head)


def _rw_chunks(r, lw, k, v, a, b):
    S = r.shape[0]
    C = RW_CHUNK * _pick(S // RW_CHUNK, (2, 1))
    row = pl.BlockSpec((C, RW_W), lambda c: (c, 0))
    out = pl.BlockSpec((2 * C, RW_W), lambda c: (c, 0))
    shp = jax.ShapeDtypeStruct((2 * S, RW_W), F32)
    return pl.pallas_call(
        _rw_chunk_kernel,
        grid=(S // C,),
        in_specs=[row] * 6,
        out_specs=[out, out],
        out_shape=[shp, shp],
        compiler_params=_cp("parallel"),
        name="rw_chunk",
    )(r, lw, k, v, a, b)


def _rw_scan_kernel(lhs_ref, add_ref, bonus_ref, g_ref, lnw_ref, lnb_ref, bd_ref, y_ref, h_ref):
    C = RW_CHUNK

    @pl.when(pl.program_id(0) == 0)
    def _():
        h_ref[...] = jnp.zeros_like(h_ref)

    GW = RW_PACK * RW_DH
    lane_head = lax.broadcasted_iota(jnp.int32, (RW_DH, GW), 1) // RW_DH
    groups = range(RW_HEADS // RW_PACK)
    gsls = [slice(gi * GW, (gi + 1) * GW) for gi in groups]
    bd = bd_ref[...]
    ys = []
    for cb in range(y_ref.shape[0] // C):
        two = slice(2 * C * cb, 2 * C * (cb + 1))
        allres = [_dot(lhs_ref[two, gsl].astype(BF16), _bd_rows(h_ref[:, gsl].astype(BF16), lane_head))
                  + add_ref[two, gsl] for gsl in gsls]
        for gi in groups:
            h_ref[:, gsls[gi]] = allres[gi][C:2 * C, :]
        ys.append(jnp.concatenate([allres[gi][0:C, :] for gi in groups], axis=1))
    y = jnp.concatenate(ys, axis=0)
    yc = y - _seg_sum(y, bd) * (1.0 / RW_DH)
    var = _seg_sum(yc * yc, bd) * (1.0 / RW_DH)
    yn = yc * lax.rsqrt(var + RW_LNX_EPS) * lnw_ref[...] + lnb_ref[...]
    y_ref[...] = ((yn + bonus_ref[...]) * g_ref[...]).astype(BF16)


def _rw_scan(lhs, add, bonus, g, lnx_w, lnx_b):
    S = bonus.shape[0]
    C = RW_CHUNK * _pick(S // RW_CHUNK, (4, 2, 1))
    row = pl.BlockSpec((C, RW_W), lambda c: (c, 0))
    two = pl.BlockSpec((2 * C, RW_W), lambda c: (c, 0))
    vec = pl.BlockSpec((1, RW_W), lambda c: (0, 0))
    return pl.pallas_call(
        _rw_scan_kernel,
        grid=(S // C,),
        in_specs=[two, two, row, row, vec, vec, pl.BlockSpec((128, 128), lambda c: (0, 0))],
        out_specs=row,
        out_shape=jax.ShapeDtypeStruct((S, RW_W), BF16),
        scratch_shapes=[pltpu.VMEM((RW_DH, RW_W), F32)],
        compiler_params=_cp("arbitrary"),
        name="rw_scan",
    )(lhs, add, bonus, g, lnx_w, lnx_b, _head_ones())


def _pad_lora_cols(x, axis):
    W3 = 3 * RW_W
    parts = [lax.slice_in_dim(x, 0, W3, axis=axis),
             lax.slice_in_dim(x, W3, W3 + RW_DECAY_LORA, axis=axis),
             lax.slice_in_dim(x, W3 + RW_DECAY_LORA, W3 + RW_DECAY_LORA + RW_A_LORA, axis=axis),
             lax.slice_in_dim(x, W3 + RW_DECAY_LORA + RW_A_LORA, RW_COLS, axis=axis)]

    def padto(p, n):
        cfg = [(0, 0)] * x.ndim
        cfg[axis] = (0, n - p.shape[axis])
        return jnp.pad(p, cfg)

    return jnp.concatenate([parts[0], padto(parts[1], LORA_PAD), padto(parts[2], LORA_PAD), parts[3]], axis=axis)


def _rwkv_mixer(x, gx, w, mu, w0, w2, a0, a2, g2, k_k, k_a, r_k, lnx_w, lnx_b):
    row = lambda p: p.reshape(1, -1).astype(F32)
    padrows = lambda m: jnp.pad(m, ((0, LORA_PAD - m.shape[0]), (0, 0))).astype(BF16)
    r, lw, k, v, a, b, g, bonus = _rw_proj_prep(
        x, gx, w, _pad_lora_cols(row(mu), 1), row(w0), padrows(w2), row(a0), padrows(a2), g2.astype(BF16),
        row(k_k), row(k_a), row(r_k))
    lhs, add = _rw_chunks(r, lw, k, v, a, b)
    return _rw_scan(lhs, add, bonus, g, row(lnx_w), row(lnx_b))


def _lru_kernel(u_ref, cw_ref, cb_ref, wa_ref, ba_ref, wx_ref, bx_ref, sp_ref, y_ref, buf_ref, a_sc, b_sc, h_sc):
    i = pl.program_id(0)
    tm = u_ref.shape[0]

    @pl.when(i == 0)
    def _():
        buf_ref[0:8, :] = jnp.zeros((8, LRU_W), F32)
        h_sc[...] = jnp.zeros_like(h_sc)

    xb = u_ref[:, LRU_W:]
    buf_ref[8:8 + tm, :] = xb
    xc = cb_ref[...] + cw_ref[CONV_W - 1:CONV_W, :] * xb
    for kk in range(CONV_W - 1):
        xc = xc + cw_ref[kk:kk + 1, :] * buf_ref[pl.ds(8 - (CONV_W - 1) + kk, tm), :]
    buf_ref[0:8, :] = xb[tm - 8:tm, :]

    xcb = xc.astype(BF16)
    ra, ri = [], []
    for n in range(LRU_BLOCKS):
        sl = slice(n * LRU_BW, (n + 1) * LRU_BW)
        ra.append(_dot(xcb[:, sl], wa_ref[n]))
        ri.append(_dot(xcb[:, sl], wx_ref[n]))
    rg = _sigmoid(jnp.concatenate(ra, axis=1) + ba_ref[...])
    ig = _sigmoid(jnp.concatenate(ri, axis=1) + bx_ref[...])
    log_a = -LRU_C * rg * sp_ref[...]
    a_sc[...] = jnp.exp(log_a)
    b_sc[...] = jnp.sqrt(1.0 - jnp.exp(2.0 * log_a)) * ig * xc

    def step(t, h):
        h = a_sc[pl.ds(t, 1), :] * h + b_sc[pl.ds(t, 1), :]
        b_sc[pl.ds(t, 1), :] = h
        return h

    h_sc[...] = lax.fori_loop(0, tm, step, h_sc[...], unroll=8)
    gate = u_ref[:, 0:LRU_W]
    gelu = 0.5 * gate * (1.0 + jnp.tanh(0.7978845608028654 * (gate + 0.044715 * gate * gate * gate)))
    y_ref[...] = (b_sc[...] * gelu).astype(BF16)


def _lru_mixer(u, conv_w, conv_b, wa, ba, wx, bx, lam):
    S = u.shape[0]
    tm = _pick(S, (256, 128, 64))
    row = lambda p: p.reshape(1, -1).astype(F32)
    vec = pl.BlockSpec((1, LRU_W), lambda i: (0, 0))
    blk = pl.BlockSpec((LRU_BLOCKS, LRU_BW, LRU_BW), lambda i: (0, 0, 0))
    lamf = lam.astype(F32)
    softplus_neg_lam = row(jnp.maximum(-lamf, 0.0) + jnp.log1p(jnp.exp(-jnp.abs(lamf))))
    cw = jnp.concatenate([conv_w.astype(F32), jnp.zeros((8 - CONV_W, LRU_W), F32)], axis=0)
    return pl.pallas_call(
        _lru_kernel,
        grid=(S // tm,),
        in_specs=[pl.BlockSpec((tm, 2 * LRU_W), lambda i: (i, 0)), pl.BlockSpec((8, LRU_W), lambda i: (0, 0)),
                  vec, blk, vec, blk, vec, vec],
        out_specs=pl.BlockSpec((tm, LRU_W), lambda i: (i, 0)),
        out_shape=jax.ShapeDtypeStruct((S, LRU_W), BF16),
        scratch_shapes=[pltpu.VMEM((tm + 8, LRU_W), F32), pltpu.VMEM((tm, LRU_W), F32),
                        pltpu.VMEM((tm, LRU_W), F32), pltpu.VMEM((1, LRU_W), F32)],
        compiler_params=_cp("arbitrary"),
        name="rglru",
    )(u, cw, row(conv_b), wa.astype(BF16), row(ba), wx.astype(BF16), row(bx), softplus_neg_lam)


def _hgrn_kernel(u_ref, lower_ref, ng_ref, y_ref, st_ref):
    C = HG_CHUNK
    W = HG_KW

    @pl.when(pl.program_id(0) == 0)
    def _():
        st_ref[...] = jnp.zeros_like(st_ref)

    for cb in range(u_ref.shape[0] // C):
        _hgrn_chunk(u_ref, lower_ref, ng_ref, y_ref, st_ref, cb * C)


def _hgrn_chunk(u_ref, lower_ref, ng_ref, y_ref, st_ref, r0):
    C = HG_CHUNK
    W = HG_KW
    rs = slice(r0, r0 + C)
    lower = lower_ref[...]
    forget = lower + (1.0 - lower) * _sigmoid(u_ref[rs, W:2 * W])
    lf = jnp.log(forget)
    kk = 1.0 - forget
    qx = u_ref[rs, 0:W]
    q = qx * _sigmoid(qx)
    v = u_ref[rs, 2 * W:3 * W]
    gx = u_ref[rs, 3 * W:4 * W]

    ti = lax.broadcasted_iota(jnp.int32, (C, C), 0)
    si = lax.broadcasted_iota(jnp.int32, (C, C), 1)
    b = _dot_hi((si <= ti).astype(F32), lf)
    blast = b[C - 1:C, :]
    q_in = (q * jnp.exp(b)).astype(BF16)
    k_end = (kk * jnp.exp(blast - b)).astype(BF16)
    vb = v.astype(BF16)
    rowi = lax.broadcasted_iota(jnp.int32, (C, W), 0)

    n_sub = C // HG_SUB
    refs = [jnp.zeros((1, W), F32)] + [b[i * HG_SUB - 1:i * HG_SUB, :] for i in range(1, n_sub)]
    ref_rows = jnp.concatenate([jnp.broadcast_to(r_, (HG_SUB, W)) for r_ in refs], axis=0)
    q_sub = (q * jnp.exp(b - ref_rows)).astype(BF16)
    k_sub = []
    for i in range(n_sub):
        live = rowi < (i + 1) * HG_SUB
        k_sub.append(jnp.where(live, kk * jnp.exp(jnp.where(live, refs[i] - b, 0.0)), 0.0).astype(BF16))
    k_sub = jnp.concatenate(k_sub, axis=0)
    row_blk = ti // HG_SUB

    heads = range(HG_HEADS)
    sls = [slice(h * HG_DK, (h + 1) * HG_DK) for h in heads]
    states = [st_ref[:, sl] for sl in sls]
    inter = [_dot_nt(q_in[:, sls[h]], states[h].astype(BF16)) for h in heads]
    pair = [_dot_nt(q_sub[:, sl], k_sub[:, sl]) for sl in sls]
    att = []
    for h in heads:
        a_h = jnp.zeros((C, C), F32)
        for i in range(n_sub):
            a_h = jnp.where((row_blk == i) & (si <= ti), pair[h][:, i * C:(i + 1) * C], a_h)
        att.append(a_h.astype(BF16))
    intra = [_dot(att[h], vb[:, sls[h]]) for h in heads]
    upd = [_dot_tn(vb[:, sl], k_end[:, sl]) for sl in sls]
    for h in heads:
        sl = sls[h]
        st_ref[:, sl] = jnp.exp(blast[:, sl]) * states[h] + upd[h]
        o = inter[h] + intra[h]
        ms = jnp.mean(o * o, axis=-1, keepdims=True)
        on = o * lax.rsqrt(ms + NORM_EPS) * ng_ref[:, sl]
        gh = gx[:, sl]
        y_ref[rs, sl] = (on * (gh * _sigmoid(gh))).astype(BF16)


def _hgrn_mixer(u, lower, norm_g):
    S = u.shape[0]
    C = HG_CHUNK * _pick(S // HG_CHUNK, (4, 2, 1))
    vec = pl.BlockSpec((1, HG_KW), lambda c: (0, 0))
    return pl.pallas_call(
        _hgrn_kernel,
        grid=(S // C,),
        in_specs=[pl.BlockSpec((C, 4 * HG_KW), lambda c: (c, 0)), vec, vec],
        out_specs=pl.BlockSpec((C, HG_VW), lambda c: (c, 0)),
        out_shape=jax.ShapeDtypeStruct((S, HG_VW), BF16),
        scratch_shapes=[pltpu.VMEM((HG_DV, HG_KW), F32)],
        compiler_params=_cp("arbitrary"),
        name="hgrn2",
    )(u, lower.reshape(1, -1).astype(F32), norm_g.reshape(1, -1).astype(F32))


def kernel(x, norm_mix, norm_mlp, w_ff1, w_ff2, w_in_a, w_out_a, nsa_qk_gain, nsa_cmp_w, nsa_cmp_pe, rw_mu, rw_w0, rw_w2, rw_a0, rw_a2, rw_g2, rw_k_k, rw_k_a, rw_r_k, rw_lnx_w, rw_lnx_b, w_in_b, w_out_b, lru_conv_w, lru_conv_b, lru_wa, lru_ba, lru_wx, lru_bx, lru_lambda, hg_lb, hg_norm):
    B, S, D = x.shape
    depth = norm_mix.shape[0]
    lb_p = jax.nn.softmax(hg_lb.astype(F32), axis=0)
    lb_cum = jnp.cumsum(lb_p, axis=0)
    hg_lower = lb_cum - lb_cum[0:1]
    w_ff1_b, w_ff2_b = w_ff1.astype(BF16), w_ff2.astype(BF16)

    outs = []
    for bi in range(B):
        xb = x[bi].astype(F32)
        for layer in range(depth):
            gmix = norm_mix[layer].astype(F32)
            if layer % 2 == 0:
                e = layer // 2
                w_in = w_in_a[e]
                w_nsa = jnp.pad(w_in[:, :NSA_COLS], ((0, 0), (0, NSA_COLS_PAD - NSA_COLS))).astype(BF16)
                w_rw = _pad_lora_cols(w_in[:, NSA_COLS:], 1).astype(BF16)
                y_a = _nsa_mixer(xb, gmix, w_nsa, nsa_qk_gain[e], nsa_cmp_w[e], nsa_cmp_pe[e])
                y_b = _rwkv_mixer(xb, gmix, w_rw, rw_mu[e], rw_w0[e], rw_w2[e], rw_a0[e], rw_a2[e],
                                  rw_g2[e], rw_k_k[e], rw_k_a[e], rw_r_k[e], rw_lnx_w[e], rw_lnx_b[e])
                xb = _out_proj(y_a, y_b, w_out_a[e].astype(BF16), xb)
            else:
                o = layer // 2
                w_in = w_in_b[o].astype(BF16)
                y_c = _lru_mixer(_norm_matmul(xb, gmix, w_in[:, :2 * LRU_W]), lru_conv_w[o], lru_conv_b[o],
                                 lru_wa[o], lru_ba[o], lru_wx[o], lru_bx[o], lru_lambda[o])
                y_d = _hgrn_mixer(_norm_matmul(xb, gmix, w_in[:, 2 * LRU_W:]), hg_lower[layer], hg_norm[o])
                xb = _out_proj(y_c, y_d, w_out_b[o].astype(BF16), xb)
            xb = _mlp(xb, norm_mlp[layer].astype(F32), w_ff1_b, w_ff2_b, layer)
        outs.append(xb)
    return jnp.stack(outs, axis=0).astype(x.dtype)
```

```python
import functools

import jax
import jax.numpy as jnp
import numpy as np
from jax import lax
from jax.experimental import pallas as pl
from jax.experimental.pallas import tpu as pltpu

F32 = jnp.float32
BF16 = jnp.bfloat16
HI = lax.Precision.HIGHEST

NORM_EPS = 1e-6
NEG_INF = -1e30
ROPE_THETA = 10000.0
LOG2E = 1.4426950408889634

NSA_HEADS = 8
NSA_KV_HEADS = 2
NSA_GROUP = NSA_HEADS // NSA_KV_HEADS
NSA_DH = 128
CMP_LEN = 32
CMP_STRIDE = 16
SLC_LEN = 64
SLC_TOPN = 16
WIN = 512
Q_BLOCK = 128
NSA_W = NSA_HEADS * NSA_DH
NSA_KV_W = NSA_KV_HEADS * NSA_DH
NSA_COLS = NSA_W + 6 * NSA_KV_W + 3 * NSA_HEADS
NSA_COLS_PAD = NSA_W + 6 * NSA_KV_W + 128
SLC_TILE = 512
NSA_QPAIR = 2

RW_HEADS = 16
RW_DH = 64
RW_W = RW_HEADS * RW_DH
RW_DECAY_LORA = 96
RW_A_LORA = 96
RW_G_LORA = 256
RW_LNX_EPS = 64e-5
RW_COLS = 3 * RW_W + RW_DECAY_LORA + RW_A_LORA + RW_G_LORA
LORA_PAD = 128
RW_COLS_PAD = 3 * RW_W + 2 * LORA_PAD + RW_G_LORA
RW_CHUNK = 64
RW_PACK = 4

LRU_W = 1024
LRU_BLOCKS = 8
LRU_BW = LRU_W // LRU_BLOCKS
CONV_W = 4
LRU_C = 8.0

HG_HEADS = 8
HG_DK = 128
HG_DV = 128
HG_KW = HG_HEADS * HG_DK
HG_VW = HG_HEADS * HG_DV
HG_CHUNK = 64
HG_SUB = 16

MXU_WIDTH = 256
VMEM_LIMIT = 56 * 1024 * 1024


def _cp(*sem, flags=None):
    return pltpu.CompilerParams(dimension_semantics=sem, vmem_limit_bytes=VMEM_LIMIT, flags=flags)


def _pick(n, cands):
    for c in cands:
        if n % c == 0:
            return c
    return n


def _sigmoid(x):
    return 1.0 / (1.0 + jnp.exp(-x))


def _softplus(x):
    return jnp.maximum(x, 0.0) + jnp.log(1.0 + jnp.exp(-jnp.abs(x)))


def _split3(x):
    p1 = x.astype(BF16)
    r1 = x - p1.astype(F32)
    p2 = r1.astype(BF16)
    p3 = (r1 - p2.astype(F32)).astype(BF16)
    return p1, p2, p3


def _dot(a, b):
    return jnp.dot(a, b, preferred_element_type=F32)


def _dot_hi(a, b):
    return jnp.dot(a, b, preferred_element_type=F32, precision=HI)


def _dot_nt(a, b, precision=None):
    return lax.dot_general(a, b, (((1,), (1,)), ((), ())), preferred_element_type=F32, precision=precision)


def _dot_tn(a, b, precision=None):
    return lax.dot_general(a, b, (((0,), (0,)), ((), ())), preferred_element_type=F32, precision=precision)


def _norm_matmul_kernel(x_ref, g_ref, w_ref, o_ref, *, tn):
    x = x_ref[...]
    ms = jnp.mean(x * x, axis=-1, keepdims=True)
    hn = (x * lax.rsqrt(ms + NORM_EPS) * g_ref[...]).astype(BF16)
    n = w_ref.shape[1]
    for lo in range(0, n, tn):
        hi = min(lo + tn, n)
        o_ref[:, lo:hi] = _dot(hn, w_ref[:, lo:hi])


def _norm_matmul(x, g, w):
    S, D = x.shape
    N = w.shape[1]
    tm = _pick(S, (512, 256, 128))
    tn = 3 * MXU_WIDTH
    return pl.pallas_call(
        functools.partial(_norm_matmul_kernel, tn=tn),
        grid=(S // tm,),
        in_specs=[
            pl.BlockSpec((tm, D), lambda i: (i, 0)),
            pl.BlockSpec((1, D), lambda i: (0, 0)),
            pl.BlockSpec((D, N), lambda i: (0, 0), pipeline_mode=pl.Buffered(1)),
        ],
        out_specs=pl.BlockSpec((tm, N), lambda i: (i, 0)),
        out_shape=jax.ShapeDtypeStruct((S, N), F32),
        compiler_params=_cp("parallel"),
        name="norm_matmul",
    )(x, g.reshape(1, D), w)


def _matmul_kernel(a_ref, w_ref, o_ref):
    o_ref[...] = _dot(a_ref[...], w_ref[...])


def _matmul(a, w):
    M, K = a.shape
    N = w.shape[1]
    tm = _pick(M, (256, 128, 64, 32, 16, 8))
    return pl.pallas_call(
        _matmul_kernel,
        grid=(M // tm,),
        in_specs=[pl.BlockSpec((tm, K), lambda i: (i, 0)), pl.BlockSpec((K, N), lambda i: (0, 0))],
        out_specs=pl.BlockSpec((tm, N), lambda i: (i, 0)),
        out_shape=jax.ShapeDtypeStruct((M, N), F32),
        compiler_params=_cp("parallel"),
        name="matmul",
    )(a, w)


def _out_proj_kernel(a_ref, b_ref, wa_ref, wb_ref, r_ref, o_ref):
    o_ref[...] = r_ref[...] + _dot(a_ref[...], wa_ref[...]) + _dot(b_ref[...], wb_ref[...])


def _out_proj(ya, yb, w, res):
    S, Wa = ya.shape
    Wb = yb.shape[1]
    D = w.shape[1]
    tm = _pick(S, (512, 256, 128))
    return pl.pallas_call(
        _out_proj_kernel,
        grid=(S // tm,),
        in_specs=[
            pl.BlockSpec((tm, Wa), lambda i: (i, 0)),
            pl.BlockSpec((tm, Wb), lambda i: (i, 0)),
            pl.BlockSpec((Wa, D), lambda i: (0, 0)),
            pl.BlockSpec((Wb, D), lambda i: (0, 0)),
            pl.BlockSpec((tm, D), lambda i: (i, 0)),
        ],
        out_specs=pl.BlockSpec((tm, D), lambda i: (i, 0)),
        out_shape=jax.ShapeDtypeStruct((S, D), F32),
        compiler_params=_cp("parallel"),
        name="out_proj",
    )(ya, yb, w[:Wa], w[Wa:], res)


def _mlp_kernel(x_ref, g_ref, w1_ref, w2_ref, o_ref, hn_ref):
    f = pl.program_id(1)

    @pl.when(f == 0)
    def _():
        x = x_ref[...]
        ms = jnp.mean(x * x, axis=-1, keepdims=True)
        hn_ref[...] = (x * lax.rsqrt(ms + NORM_EPS) * g_ref[...]).astype(BF16)
        o_ref[...] = x

    hn = hn_ref[...]
    half = w1_ref.shape[1] // 2
    zs = [jnp.maximum(_dot(hn, w1_ref[:, c * half:(c + 1) * half]), 0.0) for c in range(2)]
    acc = _dot((zs[0] * zs[0]).astype(BF16), w2_ref[0:half, :])
    acc = acc + _dot((zs[1] * zs[1]).astype(BF16), w2_ref[half:2 * half, :])
    o_ref[...] += acc


def _mlp(x, g, w1, w2, layer):
    S, D = x.shape
    Fdim = w1.shape[2]
    tm = _pick(S, (1024, 512, 256, 128))
    tf = _pick(Fdim, (512, 256, 128))
    return pl.pallas_call(
        _mlp_kernel,
        grid=(S // tm, Fdim // tf),
        in_specs=[
            pl.BlockSpec((tm, D), lambda i, f: (i, 0)),
            pl.BlockSpec((1, D), lambda i, f: (0, 0)),
            pl.BlockSpec((None, D, tf), lambda i, f: (layer, 0, f)),
            pl.BlockSpec((None, tf, D), lambda i, f: (layer, f, 0)),
        ],
        out_specs=pl.BlockSpec((tm, D), lambda i, f: (i, 0)),
        out_shape=jax.ShapeDtypeStruct((S, D), F32),
        scratch_shapes=[pltpu.VMEM((tm, D), BF16)],
        compiler_params=_cp("parallel", "arbitrary"),
        name="mlp",
    )(x, g.reshape(1, D), w1, w2)


def _head_norm_rope(x, gain, cosf, sinf):
    ms = jnp.mean(x * x, axis=-1, keepdims=True)
    y = x * lax.rsqrt(ms + NORM_EPS) * gain
    return y * cosf + pltpu.roll(y, NSA_DH // 2, 1) * sinf


def _nsa_prep_kernel(u_ref, gain_ref, cos_ref, sin_ref,
                     q_ref, kc_ref, vc_ref, ks_ref, vs_ref, kw_ref, vw_ref, gate_ref):
    cosf = cos_ref[...]
    sinf = sin_ref[...]
    scale = NSA_DH ** -0.5 * LOG2E
    for h in range(NSA_HEADS):
        sl = slice(h * NSA_DH, (h + 1) * NSA_DH)
        q_ref[:, sl] = (_head_norm_rope(u_ref[:, sl], gain_ref[0:1, :], cosf, sinf) * scale).astype(BF16)
    base = NSA_W
    kc_ref[...] = u_ref[:, base:base + NSA_KV_W].astype(BF16)
    vc_ref[...] = u_ref[:, base + NSA_KV_W:base + 2 * NSA_KV_W].astype(BF16)
    vs_ref[...] = u_ref[:, base + 3 * NSA_KV_W:base + 4 * NSA_KV_W].astype(BF16)
    vw_ref[...] = u_ref[:, base + 5 * NSA_KV_W:base + 6 * NSA_KV_W].astype(BF16)
    for g in range(NSA_KV_HEADS):
        sl = slice(g * NSA_DH, (g + 1) * NSA_DH)
        o_s = base + 2 * NSA_KV_W + g * NSA_DH
        o_w = base + 4 * NSA_KV_W + g * NSA_DH
        ks_ref[:, sl] = _head_norm_rope(u_ref[:, o_s:o_s + NSA_DH], gain_ref[2:3, :], cosf, sinf).astype(BF16)
        kw_ref[:, sl] = _head_norm_rope(u_ref[:, o_w:o_w + NSA_DH], gain_ref[3:4, :], cosf, sinf).astype(BF16)
    gate_ref[...] = _sigmoid(u_ref[:, base + 6 * NSA_KV_W:base + 6 * NSA_KV_W + 128])


def _nsa_proj_prep_kernel(x_ref, g_ref, w_ref, gain_ref, cos_ref, sin_ref, *rest):
    outs, u_scr = rest[:-1], rest[-1]
    _norm_matmul_kernel(x_ref, g_ref, w_ref, u_scr, tn=3 * MXU_WIDTH)
    _nsa_prep_kernel(u_scr, gain_ref, cos_ref, sin_ref, *outs)


def _nsa_proj_prep(x, g, w, gain8, cosf, sinf):
    S, D = x.shape
    tm = _pick(S, (512, 256, 128))
    row = lambda w_: pl.BlockSpec((tm, w_), lambda i: (i, 0))
    kv = jax.ShapeDtypeStruct((S, NSA_KV_W), BF16)
    return pl.pallas_call(
        _nsa_proj_prep_kernel,
        grid=(S // tm,),
        in_specs=[row(D), pl.BlockSpec((1, D), lambda i: (0, 0)),
                  pl.BlockSpec((D, NSA_COLS_PAD), lambda i: (0, 0), pipeline_mode=pl.Buffered(1)),
                  pl.BlockSpec((8, NSA_DH), lambda i: (0, 0)), row(NSA_DH), row(NSA_DH)],
        out_specs=[row(NSA_W)] + [row(NSA_KV_W)] * 6 + [row(128)],
        out_shape=[jax.ShapeDtypeStruct((S, NSA_W), BF16)] + [kv] * 6 + [jax.ShapeDtypeStruct((S, 128), F32)],
        scratch_shapes=[pltpu.VMEM((tm, NSA_COLS_PAD), F32)],
        compiler_params=_cp("parallel"),
        name="nsa_proj_prep",
    )(x, g.reshape(1, D), w, gain8, cosf, sinf)


def _nsa_cmp_finish_kernel(ak_ref, av_ref, pek_ref, pev_ref, gain_ref, cos_ref, sin_ref,
                           kcmp_ref, vcmp_ref, buf_ref):
    n = ak_ref.shape[0]
    buf_ref[n:n + 8, :] = jnp.zeros((8, NSA_KV_W), F32)

    def combine(a_ref, pe_ref):
        buf_ref[0:n, :] = a_ref[:, NSA_KV_W:]
        pe = pe_ref[0:1, 0:NSA_KV_W] + pe_ref[1:2, NSA_KV_W:]
        return a_ref[:, 0:NSA_KV_W] + buf_ref[pl.ds(1, n), :] + pe

    kc = combine(ak_ref, pek_ref)
    for g in range(NSA_KV_HEADS):
        sl = slice(g * NSA_DH, (g + 1) * NSA_DH)
        kcmp_ref[:, sl] = _head_norm_rope(kc[:, sl], gain_ref[1:2, :], cos_ref[...], sin_ref[...]).astype(BF16)
    vcmp_ref[...] = combine(av_ref, pev_ref).astype(BF16)


def _nsa_compress(kc, vc, cmp_w, cmp_pe, gain8, ccos, csin):
    S = kc.shape[0]
    n = S // CMP_STRIDE
    half = CMP_LEN // 2
    eye = jnp.eye(NSA_KV_HEADS, dtype=F32)

    def expand(w):
        lo = jnp.einsum('lde,gh->lgdhe', w[:half], eye).reshape(half * NSA_KV_W, NSA_KV_W)
        hi = jnp.einsum('lde,gh->lgdhe', w[half:], eye).reshape(half * NSA_KV_W, NSA_KV_W)
        return jnp.concatenate([lo, hi], axis=1).astype(BF16)

    def expand_pe(pe):
        lo = jnp.broadcast_to(pe[:half, None, :], (half, NSA_KV_HEADS, NSA_DH)).reshape(1, -1)
        hi = jnp.broadcast_to(pe[half:, None, :], (half, NSA_KV_HEADS, NSA_DH)).reshape(1, -1)
        return jnp.concatenate([lo, hi, jnp.zeros((6, lo.shape[1]), F32)], axis=0).astype(BF16)

    wk, wv = expand(cmp_w[0]), expand(cmp_w[1])
    ak = _matmul(kc.reshape(n, CMP_STRIDE * NSA_KV_W), wk)
    av = _matmul(vc.reshape(n, CMP_STRIDE * NSA_KV_W), wv)
    pek = _matmul(expand_pe(cmp_pe[0]), wk)
    pev = _matmul(expand_pe(cmp_pe[1]), wv)
    full = lambda shp: pl.BlockSpec(shp, lambda i: (0, 0))
    return pl.pallas_call(
        _nsa_cmp_finish_kernel,
        grid=(1,),
        in_specs=[full((n, 2 * NSA_KV_W)), full((n, 2 * NSA_KV_W)), full((8, 2 * NSA_KV_W)), full((8, 2 * NSA_KV_W)),
                  full((8, NSA_DH)), full((n, NSA_DH)), full((n, NSA_DH))],
        out_specs=[full((n, NSA_KV_W)), full((n, NSA_KV_W))],
        out_shape=[jax.ShapeDtypeStruct((n, NSA_KV_W), BF16)] * 2,
        scratch_shapes=[pltpu.VMEM((n + 8, NSA_KV_W), F32)],
        compiler_params=_cp("arbitrary"),
        name="nsa_cmp_finish",
    )(ak, av, pek, pev, gain8, ccos, csin)


def _nsa_attn_kernel(q_ref, kcmp_ref, vcmpT_ref, ks_ref, vsT_ref, kw_ref, vwT_ref, gate_ref, khot_ref,
                     o_ref, qT_ref, psum_ref, sel_ref, oc_ref, m_ref, l_ref, acc_ref, s_ref, *, top_n):
    g = pl.program_id(0)
    pb = pl.program_id(1)
    rows = NSA_GROUP * Q_BLOCK
    n_cmp = kcmp_ref.shape[0]
    n_slc = sel_ref.shape[1]
    streams = range(NSA_QPAIR)
    t0s = [(pb * NSA_QPAIR + i) * Q_BLOCK for i in streams]

    lane = lax.broadcasted_iota(jnp.int32, (1, rows), 1)
    q_io = lax.broadcasted_iota(jnp.int32, (1, Q_BLOCK), 1)
    n_io = lax.broadcasted_iota(jnp.int32, (n_cmp, 1), 0)
    m_io = lax.broadcasted_iota(jnp.int32, (n_slc, 1), 0)
    for i in streams:
        t0 = t0s[i]
        t_lane = t0 + (lane & (Q_BLOCK - 1))
        qf = q_ref[i * Q_BLOCK:(i + 1) * Q_BLOCK, :].astype(F32)
        qT = jnp.concatenate([qf[:, j * NSA_DH:(j + 1) * NSA_DH].T for j in range(NSA_GROUP)], axis=1).astype(BF16)
        qT_ref[i, 0:NSA_DH, :] = qT
        qT_ref[i, NSA_DH:2 * NSA_DH, :] = jnp.zeros((NSA_DH, rows), BF16)

        s = _dot(kcmp_ref[...], qT)
        valid = (n_io * CMP_STRIDE + (CMP_LEN - 1)) <= t_lane
        sm = jnp.where(valid, s, NEG_INF)
        mx = jnp.max(sm, axis=0, keepdims=True)
        e = jnp.where(valid, jnp.exp2(sm - mx), 0.0)
        den = jnp.sum(e, axis=0, keepdims=True)
        p = e * jnp.where(den > 0.0, 1.0 / den, 0.0)
        oc_ref[i] = _dot(vcmpT_ref[...], p.astype(BF16))
        psum = p[:, 0:Q_BLOCK]
        for j in range(1, NSA_GROUP):
            psum = psum + p[:, j * Q_BLOCK:(j + 1) * Q_BLOCK]
        psum_ref[i, 0:8, :] = jnp.zeros((8, Q_BLOCK), F32)
        psum_ref[i, 8:8 + n_cmp, :] = psum
        ratio = SLC_LEN // CMP_STRIDE
        imp = 0.5 * (psum_ref[i, pl.ds(7, n_slc, stride=ratio), :] + psum_ref[i, pl.ds(8 + ratio - 1, n_slc, stride=ratio), :])
        for c in range(ratio - 1):
            imp = imp + psum_ref[i, pl.ds(8 + c, n_slc, stride=ratio), :]

        cur = jnp.right_shift(t0 + q_io, 6)
        forced = (m_io == 0) | (m_io == cur) | (m_io == cur - 1)
        score = jnp.where(forced | (m_io > cur), -2e38, imp)
        for _ in range(top_n - 3):
            best = jnp.max(score, axis=0, keepdims=True)
            idx = jnp.min(jnp.where(score == best, m_io, n_slc), axis=0, keepdims=True)
            score = jnp.where(m_io == idx, -3e38, score)
        sel_ref[i] = jnp.where(forced | (score < -2.5e38), 0.0, NEG_INF)

    m_ref[...] = jnp.full(m_ref.shape, NEG_INF, F32)
    l_ref[...] = jnp.zeros(l_ref.shape, F32)
    acc_ref[...] = jnp.zeros(acc_ref.shape, F32)
    blocks_per_tile = SLC_TILE // SLC_LEN

    sub = SLC_TILE // 4

    def scores_into(kt, slot):
        k_tile = ks_ref[pl.ds(pl.multiple_of(kt * SLC_TILE, SLC_TILE), SLC_TILE), :]
        k_aug = jnp.concatenate([k_tile, khot_ref[slot]], axis=1)
        band = NSA_DH + 16 * slot
        for i in streams:
            selt = sel_ref[i, pl.ds(pl.multiple_of(kt * blocks_per_tile, blocks_per_tile), blocks_per_tile), :]
            rows8 = jnp.concatenate([selt] * NSA_GROUP, axis=1)
            qT_ref[i, band:band + 16, :] = jnp.concatenate([rows8, jnp.zeros_like(rows8)], axis=0).astype(BF16)
        for i in streams:
            for c in range(4):
                s_ref[i, slot, c * sub:(c + 1) * sub, :] = _dot(k_aug[c * sub:(c + 1) * sub, :], qT_ref[i])

    def absorb(kt, slot, causal):
        v_tile = vsT_ref[kt]
        for i in streams:
            sm = s_ref[i, slot]
            if causal:
                kpos = kt * SLC_TILE + lax.broadcasted_iota(jnp.int32, (SLC_TILE, rows), 0)
                sm = jnp.where(kpos <= t0s[i] + (lane & (Q_BLOCK - 1)), sm, NEG_INF)
            m_old = m_ref[i]
            m_new = jnp.maximum(m_old, jnp.max(sm, axis=0, keepdims=True))
            p = jnp.exp2(sm - m_new)
            alpha = jnp.exp2(m_old - m_new)
            l_ref[i] = alpha * l_ref[i] + jnp.sum(p, axis=0, keepdims=True)
            acc_ref[i] = alpha * acc_ref[i] + _dot(v_tile, p.astype(BF16))
            m_ref[i] = m_new

    last_kt = (t0s[0] + Q_BLOCK - 1) // SLC_TILE
    scores_into(0, 0)

    def slc_pair(kk, carry):
        kt = 2 * kk
        scores_into(kt + 1, 1)
        absorb(kt, 0, False)
        scores_into(kt + 2, 0)
        absorb(kt + 1, 1, False)
        return carry

    lax.fori_loop(0, last_kt // 2, slc_pair, 0)

    @pl.when(last_kt % 2 == 0)
    def _():
        absorb(last_kt, 0, True)

    @pl.when(last_kt % 2 == 1)
    def _():
        scores_into(last_kt, 1)
        absorb(last_kt - 1, 0, False)
        absorb(last_kt, 1, True)

    gT = gate_ref[...].T
    wlen = WIN + Q_BLOCK
    for i in streams:
        t0 = t0s[i]
        t_lane = t0 + (lane & (Q_BLOCK - 1))
        o_s = acc_ref[i] * (1.0 / l_ref[i])

        start = pl.multiple_of(jnp.maximum(t0 - WIN, 0), Q_BLOCK)
        s = _dot(kw_ref[pl.ds(start, wlen), :], qT_ref[i, 0:NSA_DH, :])
        kpos = start + lax.broadcasted_iota(jnp.int32, (wlen, 1), 0)
        d = t_lane - kpos
        mask = (d >= 0) & (d < WIN)
        sm = jnp.where(mask, s, NEG_INF)
        mx = jnp.max(sm, axis=0, keepdims=True)
        e = jnp.where(mask, jnp.exp2(sm - mx), 0.0)
        p = (e * (1.0 / jnp.sum(e, axis=0, keepdims=True))).astype(BF16)
        tile0 = start // Q_BLOCK
        o_w = _dot(vwT_ref[tile0], p[0:Q_BLOCK, :])
        for w in range(1, wlen // Q_BLOCK):
            o_w = o_w + _dot(vwT_ref[tile0 + w], p[w * Q_BLOCK:(w + 1) * Q_BLOCK, :])

        for j in range(NSA_GROUP):
            sl = slice(j * Q_BLOCK, (j + 1) * Q_BLOCK)

            def grow(b):
                r0 = b * NSA_HEADS + j
                r1 = b * NSA_HEADS + NSA_GROUP + j
                cols = slice(i * Q_BLOCK, (i + 1) * Q_BLOCK)
                return jnp.where(g == 0, gT[r0:r0 + 1, cols], gT[r1:r1 + 1, cols])

            o = grow(0) * oc_ref[i, :, sl] + grow(1) * o_s[:, sl] + grow(2) * o_w[:, sl]
            o_ref[i * Q_BLOCK:(i + 1) * Q_BLOCK, j * NSA_DH:(j + 1) * NSA_DH] = o.T.astype(BF16)


def _nsa_attention(q, kcmp, vcmp, ks, vs, kw, vw, gate):
    S = q.shape[0]
    n_cmp = kcmp.shape[0]
    n_slc = S // SLC_LEN
    n_qb = S // Q_BLOCK
    top_n = min(SLC_TOPN, n_slc)
    rows = NSA_GROUP * Q_BLOCK

    nq = NSA_QPAIR
    assert n_qb % nq == 0 and SLC_TILE % (nq * Q_BLOCK) == 0 and n_cmp * CMP_STRIDE == n_slc * SLC_LEN

    def tiles_T(v, width):
        return v.reshape(S // width, width, NSA_KV_HEADS, NSA_DH).transpose(2, 0, 3, 1)

    blk_of_key = np.arange(SLC_TILE)[:, None] // SLC_LEN
    khot = jnp.asarray(np.stack([blk_of_key + 16 * s == np.arange(NSA_DH)[None, :] for s in range(2)]), dtype=BF16)
    vcmpT = vcmp.reshape(n_cmp, NSA_KV_HEADS, NSA_DH).transpose(1, 2, 0)
    vsT = tiles_T(vs, SLC_TILE)
    vwT = tiles_T(vw, Q_BLOCK)

    return pl.pallas_call(
        functools.partial(_nsa_attn_kernel, top_n=top_n),
        grid=(NSA_KV_HEADS, n_qb // nq),
        in_specs=[
            pl.BlockSpec((nq * Q_BLOCK, rows), lambda g, b: (b, g)),
            pl.BlockSpec((n_cmp, NSA_DH), lambda g, b: (0, g)),
            pl.BlockSpec((None, NSA_DH, n_cmp), lambda g, b: (g, 0, 0)),
            pl.BlockSpec((S, NSA_DH), lambda g, b: (0, g)),
            pl.BlockSpec((None, S // SLC_TILE, NSA_DH, SLC_TILE), lambda g, b: (g, 0, 0, 0)),
            pl.BlockSpec((S, NSA_DH), lambda g, b: (0, g)),
            pl.BlockSpec((None, S // Q_BLOCK, NSA_DH, Q_BLOCK), lambda g, b: (g, 0, 0, 0)),
            pl.BlockSpec((nq * Q_BLOCK, 128), lambda g, b: (b, 0)),
            pl.BlockSpec((2, SLC_TILE, NSA_DH), lambda g, b: (0, 0, 0)),
        ],
        out_specs=pl.BlockSpec((nq * Q_BLOCK, rows), lambda g, b: (b, g)),
        out_shape=jax.ShapeDtypeStruct((S, NSA_W), BF16),
        scratch_shapes=[
            pltpu.VMEM((nq, 2 * NSA_DH, rows), BF16),
            pltpu.VMEM((nq, n_cmp + 8, Q_BLOCK), F32),
            pltpu.VMEM((nq, n_slc, Q_BLOCK), F32),
            pltpu.VMEM((nq, NSA_DH, rows), F32),
            pltpu.VMEM((nq, 1, rows), F32),
            pltpu.VMEM((nq, 1, rows), F32),
            pltpu.VMEM((nq, NSA_DH, rows), F32),
            pltpu.VMEM((nq, 2, SLC_TILE, rows), F32),
        ],
        compiler_params=_cp("arbitrary", "arbitrary"),
        name="nsa_attn",
    )(q, kcmp, vcmpT, ks, vsT, kw, vwT, gate, khot)


def _rope_tables(pos):
    inv = ROPE_THETA ** (-(jnp.arange(0, NSA_DH, 2, dtype=F32) / NSA_DH))
    ang = pos[:, None] * inv[None, :]
    c, s = jnp.cos(ang), jnp.sin(ang)
    return jnp.concatenate([c, c], axis=-1), jnp.concatenate([-s, s], axis=-1)


def _nsa_mixer(x, g, w, qk_gain, cmp_w, cmp_pe):
    S = x.shape[0]
    gain8 = jnp.concatenate([qk_gain.astype(F32), jnp.zeros((4, NSA_DH), F32)], axis=0)
    cosf, sinf = _rope_tables(jnp.arange(S, dtype=F32))
    q, kc, vc, ks, vs, kw, vw, gate = _nsa_proj_prep(x, g, w, gain8, cosf, sinf)
    n = S // CMP_STRIDE
    ccos, csin = _rope_tables((jnp.arange(n) * CMP_STRIDE + CMP_LEN - 1).astype(F32))
    kcmp, vcmp = _nsa_compress(kc, vc, cmp_w, cmp_pe, gain8, ccos, csin)
    return _nsa_attention(q, kcmp, vcmp, ks, vs, kw, vw, gate)


def _head_ones():
    return jnp.asarray(np.kron(np.eye(128 // RW_DH), np.ones((RW_DH, RW_DH))), dtype=BF16)


def _seg_sum(x, ones_bd):
    outs = []
    for c in range(x.shape[1] // 128):
        p1, p2, p3 = _split3(x[:, c * 128:(c + 1) * 128])
        outs.append(_dot(p1, ones_bd) + _dot(p2, ones_bd) + _dot(p3, ones_bd))
    return jnp.concatenate(outs, axis=1)


def _rw_proj_prep_kernel(x_ref, gx_ref, w_ref, mu_ref, w0_ref, w2_ref, a0_ref, a2_ref, g2_ref, kk_ref, ka_ref, rk_ref,
                         bd_ref, r_ref, lw_ref, k_ref, v_ref, a_ref, b_ref, g_ref, bonus_ref, buf_ref):
    tm = x_ref.shape[0]
    W = RW_W

    @pl.when(pl.program_id(0) == 0)
    def _():
        buf_ref[0:8, :] = jnp.zeros((8, RW_COLS_PAD), F32)

    x = x_ref[...]
    ms = jnp.mean(x * x, axis=-1, keepdims=True)
    hn = (x * lax.rsqrt(ms + NORM_EPS) * gx_ref[...]).astype(BF16)

    def shifted(lo, hi):
        buf_ref[8:8 + tm, lo:hi] = _dot(hn, w_ref[:, lo:hi])
        u = buf_ref[8:8 + tm, lo:hi]
        return u + (buf_ref[pl.ds(7, tm), lo:hi] - u) * mu_ref[:, lo:hi]

    xl = shifted(3 * W, RW_COLS_PAD)
    wl, al, gl = xl[:, 0:LORA_PAD], xl[:, LORA_PAD:2 * LORA_PAD], xl[:, 2 * LORA_PAD:]
    xk = shifted(W, 2 * W)
    wx = w0_ref[...] + _dot(jnp.tanh(wl).astype(BF16), w2_ref[...])
    lw_ref[...] = -jnp.exp(-_softplus(-wx) - 0.5)
    a = _sigmoid(a0_ref[...] + _dot(al.astype(BF16), a2_ref[...]))
    g_ref[...] = _dot(_sigmoid(gl).astype(BF16), g2_ref[...])
    xr = shifted(0, W)
    bd = bd_ref[...]
    kk = xk * kk_ref[...]
    kk = kk / jnp.maximum(jnp.sqrt(_seg_sum(kk * kk, bd)), 1e-12)
    kf = xk * (1.0 + (a - 1.0) * ka_ref[...])
    k_ref[...] = kf
    a_ref[...] = -kk
    b_ref[...] = kk * a
    xv = shifted(2 * W, 3 * W)
    r_ref[...] = xr
    v_ref[...] = xv
    bonus_ref[...] = _seg_sum(xr * kf * rk_ref[...], bd) * xv
    buf_ref[0:8, :] = buf_ref[tm:tm + 8, :]


def _rw_proj_prep(x, g, w, mu, w0, w2, a0, a2, g2, k_k, k_a, r_k):
    S, D = x.shape
    tm = _pick(S, (256, 128, 64))
    W = RW_W
    vec = lambda n: pl.BlockSpec((1, n), lambda i: (0, 0))
    mat = lambda a, b: pl.BlockSpec((a, b), lambda i: (0, 0))
    row = pl.BlockSpec((tm, W), lambda i: (i, 0))
    out = jax.ShapeDtypeStruct((S, W), F32)
    return pl.pallas_call(
        _rw_proj_prep_kernel,
        grid=(S // tm,),
        in_specs=[
            pl.BlockSpec((tm, D), lambda i: (i, 0)), vec(D),
            pl.BlockSpec((D, RW_COLS_PAD), lambda i: (0, 0), pipeline_mode=pl.Buffered(1)),
            vec(RW_COLS_PAD), vec(W), mat(LORA_PAD, W), vec(W), mat(LORA_PAD, W), mat(RW_G_LORA, W),
            vec(W), vec(W), vec(W), mat(128, 128),
        ],
        out_specs=[row] * 8,
        out_shape=[out] * 8,
        scratch_shapes=[pltpu.VMEM((tm + 8, RW_COLS_PAD), F32)],
        compiler_params=_cp("arbitrary"),
        name="rw_proj_prep",
    )(x, g.reshape(1, D), w, mu, w0, w2, a0, a2, g2, k_k, k_a, r_k, _head_ones())


def _bd_rows(x, lane_head):
    lh = lane_head[0:x.shape[0], :]
    return jnp.concatenate([jnp.where(lh == h, x, jnp.zeros_like(x)) for h in range(RW_PACK)], axis=0)


def _diag_pack(full, lane_head):
    out = None
    for h in range(RW_PACK):
        blk = jnp.where(lane_head == h, full[h * RW_DH:(h + 1) * RW_DH, :], 0.0)
        out = blk if out is None else out + blk
    return out


def _rw_chunk_kernel(r_ref, lw_ref, k_ref, v_ref, a_ref, b_ref, lhs_ref, add_ref):
    for cb in range(r_ref.shape[0] // RW_CHUNK):
        _rw_chunk_body(r_ref, lw_ref, k_ref, v_ref, a_ref, b_ref, lhs_ref, add_ref, cb * RW_CHUNK)


def _rw_chunk_body(r_ref, lw_ref, k_ref, v_ref, a_ref, b_ref, lhs_ref, add_ref, r0):
    C = RW_CHUNK
    rs = slice(r0, r0 + C)
    ti = lax.broadcasted_iota(jnp.int32, (C, C), 0)
    si = lax.broadcasted_iota(jnp.int32, (C, C), 1)
    tri_incl = (si <= ti).astype(F32)
    GW = RW_PACK * RW_DH
    lane_head = lax.broadcasted_iota(jnp.int32, (C, GW), 1) // RW_DH
    lane_pos = lax.broadcasted_iota(jnp.int32, (C, GW), 1) % RW_DH
    row_t = lax.broadcasted_iota(jnp.int32, (C, GW), 0)
    eye_p = (lane_pos == row_t).astype(F32)
    strict = jnp.concatenate([lane_pos < row_t] * 2, axis=1)
    incl = jnp.concatenate([lane_pos <= row_t] * 2, axis=1)

    lw = lw_ref[rs, :]
    cs = _dot_hi(tri_incl, lw)
    cprev = cs - lw
    clast = cs[C - 1:C, :]
    e_neg = jnp.exp(-cs)
    e_end = jnp.exp(clast - cs)
    At = (a_ref[rs, :] * jnp.exp(cprev)).astype(BF16)
    Rt32 = r_ref[rs, :] * jnp.exp(cs)
    Rt = Rt32.astype(BF16)
    Bt = (b_ref[rs, :] * e_neg).astype(BF16)
    Kt = (k_ref[rs, :] * e_neg).astype(BF16)
    Bh = (b_ref[rs, :] * e_end).astype(BF16)
    Kh = (k_ref[rs, :] * e_end).astype(BF16)
    V = v_ref[rs, :].astype(BF16)
    gam = jnp.exp(clast)

    groups = range(RW_HEADS // RW_PACK)
    sls = [slice(gi * GW, (gi + 1) * GW) for gi in groups]
    bd = functools.partial(_bd_rows, lane_head=lane_head)
    G = [_dot_nt(jnp.concatenate([At[:, sl], Rt[:, sl]], axis=0),
                 jnp.concatenate([bd(Bt[:, sl]), bd(Kt[:, sl])], axis=0)) for sl in sls]
    top = [jnp.where(strict, g_[0:C, :], 0.0) for g_ in G]
    bot = [jnp.where(incl, g_[C:2 * C, :], 0.0).astype(BF16) for g_ in G]
    L = [t_[:, 0:GW] for t_ in top]
    AV = [_dot(jnp.concatenate([top[gi][:, GW:].astype(BF16), bot[gi][:, GW:]], axis=0), bd(V[:, sls[gi]])) for gi in groups]
    T = [eye_p + l_ for l_ in L]
    P = [_dot(l_.astype(BF16), bd(l_.astype(BF16))) for l_ in L]
    for _ in range(4):
        TP = [_dot(jnp.concatenate([T[gi].astype(BF16), P[gi].astype(BF16)], axis=0), bd(P[gi].astype(BF16)))
              for gi in groups]
        T = [T[gi] + TP[gi][0:C, :] for gi in groups]
        P = [tp[C:2 * C, :] for tp in TP]
    T = [T[gi] + _dot(T[gi].astype(BF16), bd(P[gi].astype(BF16))) for gi in groups]
    AU = [_dot(T[gi].astype(BF16), jnp.concatenate([bd(At[:, sls[gi]]), bd(AV[gi][0:C, :].astype(BF16))], axis=1))
          for gi in groups]
    Ahat = [au[:, 0:GW].astype(BF16) for au in AU]
    Uhat = [au[:, GW:].astype(BF16) for au in AU]
    RY = [_dot(bot[gi][:, 0:GW], jnp.concatenate([bd(Ahat[gi]), bd(Uhat[gi])], axis=1)) for gi in groups]
    phi_full = [_dot_tn(Bh[:, sls[gi]], Ahat[gi]) for gi in groups]
    hh_full = [_dot_tn(jnp.concatenate([Bh[:, sls[gi]], Kh[:, sls[gi]]], axis=0),
                       jnp.concatenate([Uhat[gi], V[:, sls[gi]]], axis=0)) for gi in groups]
    for gi in groups:
        sl = sls[gi]
        o0 = 2 * r0
        lhs_ref[o0:o0 + C, sl] = Rt32[:, sl] + RY[gi][:, 0:GW]
        add_ref[o0:o0 + C, sl] = RY[gi][:, GW:] + AV[gi][C:2 * C, :]
        lhs_ref[o0 + C:o0 + 2 * C, sl] = eye_p * gam[:, sl] + _diag_pack(phi_full[gi], lane_head)
        add_ref[o0 + C:o0 + 2 * C, sl] = _diag_pack(hh_full[gi], lane_head)


def _rw_chunks(r, lw, k, v, a, b):
    S = r.shape[0]
    C = RW_CHUNK * _pick(S // RW_CHUNK, (2, 1))
    row = pl.BlockSpec((C, RW_W), lambda c: (c, 0))
    out = pl.BlockSpec((2 * C, RW_W), lambda c: (c, 0))
    shp = jax.ShapeDtypeStruct((2 * S, RW_W), F32)
    return pl.pallas_call(
        _rw_chunk_kernel,
        grid=(S // C,),
        in_specs=[row] * 6,
        out_specs=[out, out],
        out_shape=[shp, shp],
        compiler_params=_cp("parallel"),
        name="rw_chunk",
    )(r, lw, k, v, a, b)


def _rw_scan_kernel(lhs_ref, add_ref, bonus_ref, g_ref, lnw_ref, lnb_ref, bd_ref, y_ref, h_ref):
    C = RW_CHUNK

    @pl.when(pl.program_id(0) == 0)
    def _():
        h_ref[...] = jnp.zeros_like(h_ref)

    GW = RW_PACK * RW_DH
    lane_head = lax.broadcasted_iota(jnp.int32, (RW_DH, GW), 1) // RW_DH
    groups = range(RW_HEADS // RW_PACK)
    gsls = [slice(gi * GW, (gi + 1) * GW) for gi in groups]
    bd = bd_ref[...]
    ys = []
    for cb in range(y_ref.shape[0] // C):
        two = slice(2 * C * cb, 2 * C * (cb + 1))
        allres = [_dot(lhs_ref[two, gsl].astype(BF16), _bd_rows(h_ref[:, gsl].astype(BF16), lane_head))
                  + add_ref[two, gsl] for gsl in gsls]
        for gi in groups:
            h_ref[:, gsls[gi]] = allres[gi][C:2 * C, :]
        ys.append(jnp.concatenate([allres[gi][0:C, :] for gi in groups], axis=1))
    y = jnp.concatenate(ys, axis=0)
    yc = y - _seg_sum(y, bd) * (1.0 / RW_DH)
    var = _seg_sum(yc * yc, bd) * (1.0 / RW_DH)
    yn = yc * lax.rsqrt(var + RW_LNX_EPS) * lnw_ref[...] + lnb_ref[...]
    y_ref[...] = ((yn + bonus_ref[...]) * g_ref[...]).astype(BF16)


def _rw_scan(lhs, add, bonus, g, lnx_w, lnx_b):
    S = bonus.shape[0]
    C = RW_CHUNK * _pick(S // RW_CHUNK, (4, 2, 1))
    row = pl.BlockSpec((C, RW_W), lambda c: (c, 0))
    two = pl.BlockSpec((2 * C, RW_W), lambda c: (c, 0))
    vec = pl.BlockSpec((1, RW_W), lambda c: (0, 0))
    return pl.pallas_call(
        _rw_scan_kernel,
        grid=(S // C,),
        in_specs=[two, two, row, row, vec, vec, pl.BlockSpec((128, 128), lambda c: (0, 0))],
        out_specs=row,
        out_shape=jax.ShapeDtypeStruct((S, RW_W), BF16),
        scratch_shapes=[pltpu.VMEM((RW_DH, RW_W), F32)],
        compiler_params=_cp("arbitrary"),
        name="rw_scan",
    )(lhs, add, bonus, g, lnx_w, lnx_b, _head_ones())


def _pad_lora_cols(x, axis):
    W3 = 3 * RW_W
    parts = [lax.slice_in_dim(x, 0, W3, axis=axis),
             lax.slice_in_dim(x, W3, W3 + RW_DECAY_LORA, axis=axis),
             lax.slice_in_dim(x, W3 + RW_DECAY_LORA, W3 + RW_DECAY_LORA + RW_A_LORA, axis=axis),
             lax.slice_in_dim(x, W3 + RW_DECAY_LORA + RW_A_LORA, RW_COLS, axis=axis)]

    def padto(p, n):
        cfg = [(0, 0)] * x.ndim
        cfg[axis] = (0, n - p.shape[axis])
        return jnp.pad(p, cfg)

    return jnp.concatenate([parts[0], padto(parts[1], LORA_PAD), padto(parts[2], LORA_PAD), parts[3]], axis=axis)


def _rwkv_mixer(x, gx, w, mu, w0, w2, a0, a2, g2, k_k, k_a, r_k, lnx_w, lnx_b):
    row = lambda p: p.reshape(1, -1).astype(F32)
    padrows = lambda m: jnp.pad(m, ((0, LORA_PAD - m.shape[0]), (0, 0))).astype(BF16)
    r, lw, k, v, a, b, g, bonus = _rw_proj_prep(
        x, gx, w, _pad_lora_cols(row(mu), 1), row(w0), padrows(w2), row(a0), padrows(a2), g2.astype(BF16),
        row(k_k), row(k_a), row(r_k))
    lhs, add = _rw_chunks(r, lw, k, v, a, b)
    return _rw_scan(lhs, add, bonus, g, row(lnx_w), row(lnx_b))


def _lru_kernel(u_ref, cw_ref, cb_ref, wa_ref, ba_ref, wx_ref, bx_ref, sp_ref, y_ref, buf_ref, a_sc, b_sc, h_sc):
    i = pl.program_id(0)
    tm = u_ref.shape[0]

    @pl.when(i == 0)
    def _():
        buf_ref[0:8, :] = jnp.zeros((8, LRU_W), F32)
        h_sc[...] = jnp.zeros_like(h_sc)

    xb = u_ref[:, LRU_W:]
    buf_ref[8:8 + tm, :] = xb
    xc = cb_ref[...] + cw_ref[CONV_W - 1:CONV_W, :] * xb
    for kk in range(CONV_W - 1):
        xc = xc + cw_ref[kk:kk + 1, :] * buf_ref[pl.ds(8 - (CONV_W - 1) + kk, tm), :]
    buf_ref[0:8, :] = xb[tm - 8:tm, :]

    xcb = xc.astype(BF16)
    ra, ri = [], []
    for n in range(LRU_BLOCKS):
        sl = slice(n * LRU_BW, (n + 1) * LRU_BW)
        ra.append(_dot(xcb[:, sl], wa_ref[n]))
        ri.append(_dot(xcb[:, sl], wx_ref[n]))
    rg = _sigmoid(jnp.concatenate(ra, axis=1) + ba_ref[...])
    ig = _sigmoid(jnp.concatenate(ri, axis=1) + bx_ref[...])
    log_a = -LRU_C * rg * sp_ref[...]
    a_sc[...] = jnp.exp(log_a)
    b_sc[...] = jnp.sqrt(1.0 - jnp.exp(2.0 * log_a)) * ig * xc

    rowi = lax.broadcasted_iota(jnp.int32, (8, LRU_W), 0)

    def group(gidx, h):
        r0 = pl.multiple_of(gidx * 8, 8)
        a = a_sc[pl.ds(r0, 8), :]
        b = b_sc[pl.ds(r0, 8), :]
        for d in (1, 2, 4):
            keep = rowi >= d
            b = jnp.where(keep, a * pltpu.roll(b, d, 0) + b, b)
            a = jnp.where(keep, a * pltpu.roll(a, d, 0), a)
        hrows = a * h + b
        b_sc[pl.ds(r0, 8), :] = hrows
        return hrows[7:8, :]

    h_sc[...] = lax.fori_loop(0, tm // 8, group, h_sc[...], unroll=4)
    gate = u_ref[:, 0:LRU_W]
    gelu = 0.5 * gate * (1.0 + jnp.tanh(0.7978845608028654 * (gate + 0.044715 * gate * gate * gate)))
    y_ref[...] = (b_sc[...] * gelu).astype(BF16)


def _lru_mixer(u, conv_w, conv_b, wa, ba, wx, bx, lam):
    S = u.shape[0]
    tm = _pick(S, (256, 128, 64))
    row = lambda p: p.reshape(1, -1).astype(F32)
    vec = pl.BlockSpec((1, LRU_W), lambda i: (0, 0))
    blk = pl.BlockSpec((LRU_BLOCKS, LRU_BW, LRU_BW), lambda i: (0, 0, 0))
    lamf = lam.astype(F32)
    softplus_neg_lam = row(jnp.maximum(-lamf, 0.0) + jnp.log1p(jnp.exp(-jnp.abs(lamf))))
    cw = jnp.concatenate([conv_w.astype(F32), jnp.zeros((8 - CONV_W, LRU_W), F32)], axis=0)
    return pl.pallas_call(
        _lru_kernel,
        grid=(S // tm,),
        in_specs=[pl.BlockSpec((tm, 2 * LRU_W), lambda i: (i, 0)), pl.BlockSpec((8, LRU_W), lambda i: (0, 0)),
                  vec, blk, vec, blk, vec, vec],
        out_specs=pl.BlockSpec((tm, LRU_W), lambda i: (i, 0)),
        out_shape=jax.ShapeDtypeStruct((S, LRU_W), BF16),
        scratch_shapes=[pltpu.VMEM((tm + 8, LRU_W), F32), pltpu.VMEM((tm, LRU_W), F32),
                        pltpu.VMEM((tm, LRU_W), F32), pltpu.VMEM((1, LRU_W), F32)],
        compiler_params=_cp("arbitrary"),
        name="rglru",
    )(u, cw, row(conv_b), wa.astype(BF16), row(ba), wx.astype(BF16), row(bx), softplus_neg_lam)


def _hgrn_kernel(u_ref, lower_ref, ng_ref, y_ref, st_ref):
    C = HG_CHUNK
    W = HG_KW

    @pl.when(pl.program_id(0) == 0)
    def _():
        st_ref[...] = jnp.zeros_like(st_ref)

    for cb in range(u_ref.shape[0] // C):
        _hgrn_chunk(u_ref, lower_ref, ng_ref, y_ref, st_ref, cb * C)


def _hgrn_chunk(u_ref, lower_ref, ng_ref, y_ref, st_ref, r0):
    C = HG_CHUNK
    W = HG_KW
    rs = slice(r0, r0 + C)
    lower = lower_ref[...]
    forget = lower + (1.0 - lower) * _sigmoid(u_ref[rs, W:2 * W])
    lf = jnp.log(forget)
    kk = 1.0 - forget
    qx = u_ref[rs, 0:W]
    q = qx * _sigmoid(qx)
    v = u_ref[rs, 2 * W:3 * W]
    gx = u_ref[rs, 3 * W:4 * W]

    ti = lax.broadcasted_iota(jnp.int32, (C, C), 0)
    si = lax.broadcasted_iota(jnp.int32, (C, C), 1)
    b = _dot_hi((si <= ti).astype(F32), lf)
    blast = b[C - 1:C, :]
    q_in = (q * jnp.exp(b)).astype(BF16)
    k_end = (kk * jnp.exp(blast - b)).astype(BF16)
    vb = v.astype(BF16)
    rowi = lax.broadcasted_iota(jnp.int32, (C, W), 0)

    n_sub = C // HG_SUB
    refs = [jnp.zeros((1, W), F32)] + [b[i * HG_SUB - 1:i * HG_SUB, :] for i in range(1, n_sub)]
    ref_rows = jnp.concatenate([jnp.broadcast_to(r_, (HG_SUB, W)) for r_ in refs], axis=0)
    q_sub = (q * jnp.exp(b - ref_rows)).astype(BF16)
    k_sub = []
    for i in range(n_sub):
        live = rowi < (i + 1) * HG_SUB
        k_sub.append(jnp.where(live, kk * jnp.exp(jnp.where(live, refs[i] - b, 0.0)), 0.0).astype(BF16))
    k_sub = jnp.concatenate(k_sub, axis=0)
    row_blk = ti // HG_SUB

    heads = range(HG_HEADS)
    sls = [slice(h * HG_DK, (h + 1) * HG_DK) for h in heads]
    states = [st_ref[:, sl] for sl in sls]
    inter = [_dot_nt(q_in[:, sls[h]], states[h].astype(BF16)) for h in heads]
    pair = [_dot_nt(q_sub[:, sl], k_sub[:, sl]) for sl in sls]
    att = []
    for h in heads:
        a_h = jnp.zeros((C, C), F32)
        for i in range(n_sub):
            a_h = jnp.where((row_blk == i) & (si <= ti), pair[h][:, i * C:(i + 1) * C], a_h)
        att.append(a_h.astype(BF16))
    intra = [_dot(att[h], vb[:, sls[h]]) for h in heads]
    upd = [_dot_tn(vb[:, sl], k_end[:, sl]) for sl in sls]
    for h in heads:
        sl = sls[h]
        st_ref[:, sl] = jnp.exp(blast[:, sl]) * states[h] + upd[h]
        o = inter[h] + intra[h]
        ms = jnp.mean(o * o, axis=-1, keepdims=True)
        on = o * lax.rsqrt(ms + NORM_EPS) * ng_ref[:, sl]
        gh = gx[:, sl]
        y_ref[rs, sl] = (on * (gh * _sigmoid(gh))).astype(BF16)


def _hgrn_mixer(u, lower, norm_g):
    S = u.shape[0]
    C = HG_CHUNK * _pick(S // HG_CHUNK, (4, 2, 1))
    vec = pl.BlockSpec((1, HG_KW), lambda c: (0, 0))
    return pl.pallas_call(
        _hgrn_kernel,
        grid=(S // C,),
        in_specs=[pl.BlockSpec((C, 4 * HG_KW), lambda c: (c, 0)), vec, vec],
        out_specs=pl.BlockSpec((C, HG_VW), lambda c: (c, 0)),
        out_shape=jax.ShapeDtypeStruct((S, HG_VW), BF16),
        scratch_shapes=[pltpu.VMEM((HG_DV, HG_KW), F32)],
        compiler_params=_cp("arbitrary"),
        name="hgrn2",
    )(u, lower.reshape(1, -1).astype(F32), norm_g.reshape(1, -1).astype(F32))


def kernel(x, norm_mix, norm_mlp, w_ff1, w_ff2, w_in_a, w_out_a, nsa_qk_gain, nsa_cmp_w, nsa_cmp_pe, rw_mu, rw_w0, rw_w2, rw_a0, rw_a2, rw_g2, rw_k_k, rw_k_a, rw_r_k, rw_lnx_w, rw_lnx_b, w_in_b, w_out_b, lru_conv_w, lru_conv_b, lru_wa, lru_ba, lru_wx, lru_bx, lru_lambda, hg_lb, hg_norm):
    B, S, D = x.shape
    depth = norm_mix.shape[0]
    lb_p = jax.nn.softmax(hg_lb.astype(F32), axis=0)
    lb_cum = jnp.cumsum(lb_p, axis=0)
    hg_lower = lb_cum - lb_cum[0:1]
    w_ff1_b, w_ff2_b = w_ff1.astype(BF16), w_ff2.astype(BF16)

    outs = []
    for bi in range(B):
        xb = x[bi].astype(F32)
        for layer in range(depth):
            gmix = norm_mix[layer].astype(F32)
            if layer % 2 == 0:
                e = layer // 2
                w_in = w_in_a[e]
                w_nsa = jnp.pad(w_in[:, :NSA_COLS], ((0, 0), (0, NSA_COLS_PAD - NSA_COLS))).astype(BF16)
                w_rw = _pad_lora_cols(w_in[:, NSA_COLS:], 1).astype(BF16)
                y_a = _nsa_mixer(xb, gmix, w_nsa, nsa_qk_gain[e], nsa_cmp_w[e], nsa_cmp_pe[e])
                y_b = _rwkv_mixer(xb, gmix, w_rw, rw_mu[e], rw_w0[e], rw_w2[e], rw_a0[e], rw_a2[e],
                                  rw_g2[e], rw_k_k[e], rw_k_a[e], rw_r_k[e], rw_lnx_w[e], rw_lnx_b[e])
                xb = _out_proj(y_a, y_b, w_out_a[e].astype(BF16), xb)
            else:
                o = layer // 2
                w_in = w_in_b[o].astype(BF16)
                y_c = _lru_mixer(_norm_matmul(xb, gmix, w_in[:, :2 * LRU_W]), lru_conv_w[o], lru_conv_b[o],
                                 lru_wa[o], lru_ba[o], lru_wx[o], lru_bx[o], lru_lambda[o])
                y_d = _hgrn_mixer(_norm_matmul(xb, gmix, w_in[:, 2 * LRU_W:]), hg_lower[layer], hg_norm[o])
                xb = _out_proj(y_c, y_d, w_out_b[o].astype(BF16), xb)
            xb = _mlp(xb, norm_mlp[layer].astype(F32), w_ff1_b, w_ff2_b, layer)
        outs.append(xb.astype(x.dtype))
    return outs[0].reshape(1, S, D) if B == 1 else jnp.stack(outs, axis=0)
```

```python
import functools

import jax
import jax.numpy as jnp
import numpy as np
from jax import lax
from jax.experimental import pallas as pl
from jax.experimental.pallas import tpu as pltpu

F32 = jnp.float32
BF16 = jnp.bfloat16
HI = lax.Precision.HIGHEST

NORM_EPS = 1e-6
NEG_INF = -1e30
ROPE_THETA = 10000.0
LOG2E = 1.4426950408889634

NSA_HEADS = 8
NSA_KV_HEADS = 2
NSA_GROUP = NSA_HEADS // NSA_KV_HEADS
NSA_DH = 128
CMP_LEN = 32
CMP_STRIDE = 16
SLC_LEN = 64
SLC_TOPN = 16
WIN = 512
Q_BLOCK = 128
NSA_W = NSA_HEADS * NSA_DH
NSA_KV_W = NSA_KV_HEADS * NSA_DH
NSA_COLS = NSA_W + 6 * NSA_KV_W + 3 * NSA_HEADS
NSA_COLS_PAD = NSA_W + 6 * NSA_KV_W + 128
SLC_TILE = 512
NSA_QPAIR = 2

RW_HEADS = 16
RW_DH = 64
RW_W = RW_HEADS * RW_DH
RW_DECAY_LORA = 96
RW_A_LORA = 96
RW_G_LORA = 256
RW_LNX_EPS = 64e-5
RW_COLS = 3 * RW_W + RW_DECAY_LORA + RW_A_LORA + RW_G_LORA
LORA_PAD = 128
RW_COLS_PAD = 3 * RW_W + 2 * LORA_PAD + RW_G_LORA
RW_CHUNK = 64
RW_PACK = 4

LRU_W = 1024
LRU_BLOCKS = 8
LRU_BW = LRU_W // LRU_BLOCKS
CONV_W = 4
LRU_C = 8.0

HG_HEADS = 8
HG_DK = 128
HG_DV = 128
HG_KW = HG_HEADS * HG_DK
HG_VW = HG_HEADS * HG_DV
HG_CHUNK = 64
HG_SUB = 16

MXU_WIDTH = 256
VMEM_LIMIT = 56 * 1024 * 1024


def _cp(*sem):
    return pltpu.CompilerParams(dimension_semantics=sem, vmem_limit_bytes=VMEM_LIMIT)


def _pick(n, cands):
    for c in cands:
        if n % c == 0:
            return c
    return n


def _sigmoid(x):
    return 1.0 / (1.0 + jnp.exp(-x))


def _softplus(x):
    return jnp.maximum(x, 0.0) + jnp.log(1.0 + jnp.exp(-jnp.abs(x)))


def _split3(x):
    p1 = x.astype(BF16)
    r1 = x - p1.astype(F32)
    p2 = r1.astype(BF16)
    p3 = (r1 - p2.astype(F32)).astype(BF16)
    return p1, p2, p3


def _dot(a, b):
    return jnp.dot(a, b, preferred_element_type=F32)


def _dot_hi(a, b):
    return jnp.dot(a, b, preferred_element_type=F32, precision=HI)


def _dot_nt(a, b):
    return lax.dot_general(a, b, (((1,), (1,)), ((), ())), preferred_element_type=F32)


def _dot_tn(a, b):
    return lax.dot_general(a, b, (((0,), (0,)), ((), ())), preferred_element_type=F32)


def _norm_matmul_kernel(x_ref, g_ref, w_ref, o_ref, *, tn):
    x = x_ref[...]
    ms = jnp.mean(x * x, axis=-1, keepdims=True)
    hn = (x * lax.rsqrt(ms + NORM_EPS) * g_ref[...]).astype(BF16)
    n = w_ref.shape[1]
    for lo in range(0, n, tn):
        hi = min(lo + tn, n)
        o_ref[:, lo:hi] = _dot(hn, w_ref[:, lo:hi])


def _norm_matmul(x, g, w):
    S, D = x.shape
    N = w.shape[1]
    tm = _pick(S, (512, 256, 128))
    tn = 3 * MXU_WIDTH
    return pl.pallas_call(
        functools.partial(_norm_matmul_kernel, tn=tn),
        grid=(S // tm,),
        in_specs=[
            pl.BlockSpec((tm, D), lambda i: (i, 0)),
            pl.BlockSpec((1, D), lambda i: (0, 0)),
            pl.BlockSpec((D, N), lambda i: (0, 0), pipeline_mode=pl.Buffered(1)),
        ],
        out_specs=pl.BlockSpec((tm, N), lambda i: (i, 0)),
        out_shape=jax.ShapeDtypeStruct((S, N), F32),
        compiler_params=_cp("parallel"),
        name="norm_matmul",
    )(x, g.reshape(1, D), w)


def _matmul_kernel(a_ref, w_ref, o_ref):
    o_ref[...] = _dot(a_ref[...], w_ref[...])


def _matmul(a, w):
    M, K = a.shape
    N = w.shape[1]
    tm = _pick(M, (256, 128, 64, 32, 16, 8))
    return pl.pallas_call(
        _matmul_kernel,
        grid=(M // tm,),
        in_specs=[pl.BlockSpec((tm, K), lambda i: (i, 0)), pl.BlockSpec((K, N), lambda i: (0, 0))],
        out_specs=pl.BlockSpec((tm, N), lambda i: (i, 0)),
        out_shape=jax.ShapeDtypeStruct((M, N), F32),
        compiler_params=_cp("parallel"),
        name="matmul",
    )(a, w)


def _out_proj_kernel(a_ref, b_ref, wa_ref, wb_ref, r_ref, o_ref):
    o_ref[...] = r_ref[...] + _dot(a_ref[...], wa_ref[...]) + _dot(b_ref[...], wb_ref[...])


def _out_proj(ya, yb, w, res):
    S, Wa = ya.shape
    Wb = yb.shape[1]
    D = w.shape[1]
    tm = _pick(S, (512, 256, 128))
    return pl.pallas_call(
        _out_proj_kernel,
        grid=(S // tm,),
        in_specs=[
            pl.BlockSpec((tm, Wa), lambda i: (i, 0)),
            pl.BlockSpec((tm, Wb), lambda i: (i, 0)),
            pl.BlockSpec((Wa, D), lambda i: (0, 0)),
            pl.BlockSpec((Wb, D), lambda i: (0, 0)),
            pl.BlockSpec((tm, D), lambda i: (i, 0)),
        ],
        out_specs=pl.BlockSpec((tm, D), lambda i: (i, 0)),
        out_shape=jax.ShapeDtypeStruct((S, D), F32),
        compiler_params=_cp("parallel"),
        name="out_proj",
    )(ya, yb, w[:Wa], w[Wa:], res)


def _mlp_kernel(x_ref, g_ref, w1_ref, w2_ref, o_ref, hn_ref):
    f = pl.program_id(1)

    @pl.when(f == 0)
    def _():
        x = x_ref[...]
        ms = jnp.mean(x * x, axis=-1, keepdims=True)
        hn_ref[...] = (x * lax.rsqrt(ms + NORM_EPS) * g_ref[...]).astype(BF16)
        o_ref[...] = x

    hn = hn_ref[...]
    half = w1_ref.shape[1] // 2
    zs = [jnp.maximum(_dot(hn, w1_ref[:, c * half:(c + 1) * half]), 0.0) for c in range(2)]
    acc = _dot((zs[0] * zs[0]).astype(BF16), w2_ref[0:half, :])
    acc = acc + _dot((zs[1] * zs[1]).astype(BF16), w2_ref[half:2 * half, :])
    o_ref[...] += acc


def _mlp(x, g, w1, w2, layer):
    S, D = x.shape
    Fdim = w1.shape[2]
    tm = _pick(S, (1024, 512, 256, 128))
    tf = _pick(Fdim, (512, 256, 128))
    return pl.pallas_call(
        _mlp_kernel,
        grid=(S // tm, Fdim // tf),
        in_specs=[
            pl.BlockSpec((tm, D), lambda i, f: (i, 0)),
            pl.BlockSpec((1, D), lambda i, f: (0, 0)),
            pl.BlockSpec((None, D, tf), lambda i, f: (layer, 0, f)),
            pl.BlockSpec((None, tf, D), lambda i, f: (layer, f, 0)),
        ],
        out_specs=pl.BlockSpec((tm, D), lambda i, f: (i, 0)),
        out_shape=jax.ShapeDtypeStruct((S, D), F32),
        scratch_shapes=[pltpu.VMEM((tm, D), BF16)],
        compiler_params=_cp("parallel", "arbitrary"),
        name="mlp",
    )(x, g.reshape(1, D), w1, w2)


def _head_norm_rope(x, gain, cosf, sinf):
    ms = jnp.mean(x * x, axis=-1, keepdims=True)
    y = x * lax.rsqrt(ms + NORM_EPS) * gain
    return y * cosf + pltpu.roll(y, NSA_DH // 2, 1) * sinf


def _nsa_prep_kernel(u_ref, gain_ref, cos_ref, sin_ref,
                     q_ref, kc_ref, vc_ref, ks_ref, vs_ref, kw_ref, vw_ref, gate_ref):
    cosf = cos_ref[...]
    sinf = sin_ref[...]
    scale = NSA_DH ** -0.5 * LOG2E
    for h in range(NSA_HEADS):
        sl = slice(h * NSA_DH, (h + 1) * NSA_DH)
        q_ref[:, sl] = (_head_norm_rope(u_ref[:, sl], gain_ref[0:1, :], cosf, sinf) * scale).astype(BF16)
    base = NSA_W
    kc_ref[...] = u_ref[:, base:base + NSA_KV_W].astype(BF16)
    vc_ref[...] = u_ref[:, base + NSA_KV_W:base + 2 * NSA_KV_W].astype(BF16)
    tm = u_ref.shape[0]
    for g in range(NSA_KV_HEADS):
        c_s = base + 3 * NSA_KV_W + g * NSA_DH
        c_w = base + 5 * NSA_KV_W + g * NSA_DH
        for c in range(tm // SLC_TILE):
            vs_ref[g, c] = u_ref[c * SLC_TILE:(c + 1) * SLC_TILE, c_s:c_s + NSA_DH].T.astype(BF16)
        for c in range(tm // Q_BLOCK):
            vw_ref[g, c] = u_ref[c * Q_BLOCK:(c + 1) * Q_BLOCK, c_w:c_w + NSA_DH].T.astype(BF16)
    for g in range(NSA_KV_HEADS):
        sl = slice(g * NSA_DH, (g + 1) * NSA_DH)
        o_s = base + 2 * NSA_KV_W + g * NSA_DH
        o_w = base + 4 * NSA_KV_W + g * NSA_DH
        ks_ref[:, sl] = _head_norm_rope(u_ref[:, o_s:o_s + NSA_DH], gain_ref[2:3, :], cosf, sinf).astype(BF16)
        kw_ref[:, sl] = _head_norm_rope(u_ref[:, o_w:o_w + NSA_DH], gain_ref[3:4, :], cosf, sinf).astype(BF16)
    gate_ref[...] = _sigmoid(u_ref[:, base + 6 * NSA_KV_W:base + 6 * NSA_KV_W + 128])


def _nsa_proj_prep_kernel(x_ref, g_ref, w_ref, gain_ref, cos_ref, sin_ref, *rest):
    outs, u_scr = rest[:-1], rest[-1]
    _norm_matmul_kernel(x_ref, g_ref, w_ref, u_scr, tn=3 * MXU_WIDTH)
    _nsa_prep_kernel(u_scr, gain_ref, cos_ref, sin_ref, *outs)


def _nsa_proj_prep(x, g, w, gain8, cosf, sinf):
    S, D = x.shape
    tm = SLC_TILE
    assert S % tm == 0
    row = lambda w_: pl.BlockSpec((tm, w_), lambda i: (i, 0))
    kv = jax.ShapeDtypeStruct((S, NSA_KV_W), BF16)
    G = NSA_KV_HEADS

    def vT(width):
        return (pl.BlockSpec((G, tm // width, NSA_DH, width), lambda i: (0, i, 0, 0)),
                jax.ShapeDtypeStruct((G, S // width, NSA_DH, width), BF16))

    (vs_spec, vs_shape), (vw_spec, vw_shape) = vT(SLC_TILE), vT(Q_BLOCK)
    return pl.pallas_call(
        _nsa_proj_prep_kernel,
        grid=(S // tm,),
        in_specs=[row(D), pl.BlockSpec((1, D), lambda i: (0, 0)),
                  pl.BlockSpec((D, NSA_COLS_PAD), lambda i: (0, 0), pipeline_mode=pl.Buffered(1)),
                  pl.BlockSpec((8, NSA_DH), lambda i: (0, 0)), row(NSA_DH), row(NSA_DH)],
        out_specs=[row(NSA_W), row(NSA_KV_W), row(NSA_KV_W), row(NSA_KV_W), vs_spec, row(NSA_KV_W), vw_spec, row(128)],
        out_shape=[jax.ShapeDtypeStruct((S, NSA_W), BF16), kv, kv, kv, vs_shape, kv, vw_shape,
                   jax.ShapeDtypeStruct((S, 128), F32)],
        scratch_shapes=[pltpu.VMEM((tm, NSA_COLS_PAD), F32)],
        compiler_params=_cp("parallel"),
        name="nsa_proj_prep",
    )(x, g.reshape(1, D), w, gain8, cosf, sinf)


def _nsa_cmp_finish_kernel(ak_ref, av_ref, pek_ref, pev_ref, gain_ref, cos_ref, sin_ref,
                           kcmp_ref, vcmp_ref, buf_ref):
    n = ak_ref.shape[0]
    buf_ref[n:n + 8, :] = jnp.zeros((8, NSA_KV_W), F32)

    def combine(a_ref, pe_ref):
        buf_ref[0:n, :] = a_ref[:, NSA_KV_W:]
        pe = pe_ref[0:1, 0:NSA_KV_W] + pe_ref[1:2, NSA_KV_W:]
        return a_ref[:, 0:NSA_KV_W] + buf_ref[pl.ds(1, n), :] + pe

    kc = combine(ak_ref, pek_ref)
    for g in range(NSA_KV_HEADS):
        sl = slice(g * NSA_DH, (g + 1) * NSA_DH)
        kcmp_ref[:, sl] = _head_norm_rope(kc[:, sl], gain_ref[1:2, :], cos_ref[...], sin_ref[...]).astype(BF16)
    vcmp_ref[...] = combine(av_ref, pev_ref).astype(BF16)


def _nsa_compress(kc, vc, cmp_w, cmp_pe, gain8, ccos, csin):
    S = kc.shape[0]
    n = S // CMP_STRIDE
    half = CMP_LEN // 2
    eye = jnp.eye(NSA_KV_HEADS, dtype=F32)

    def expand(w):
        lo = jnp.einsum('lde,gh->lgdhe', w[:half], eye).reshape(half * NSA_KV_W, NSA_KV_W)
        hi = jnp.einsum('lde,gh->lgdhe', w[half:], eye).reshape(half * NSA_KV_W, NSA_KV_W)
        return jnp.concatenate([lo, hi], axis=1).astype(BF16)

    def expand_pe(pe):
        lo = jnp.broadcast_to(pe[:half, None, :], (half, NSA_KV_HEADS, NSA_DH)).reshape(1, -1)
        hi = jnp.broadcast_to(pe[half:, None, :], (half, NSA_KV_HEADS, NSA_DH)).reshape(1, -1)
        return jnp.concatenate([lo, hi, jnp.zeros((6, lo.shape[1]), F32)], axis=0).astype(BF16)

    wk, wv = expand(cmp_w[0]), expand(cmp_w[1])
    ak = _matmul(kc.reshape(n, CMP_STRIDE * NSA_KV_W), wk)
    av = _matmul(vc.reshape(n, CMP_STRIDE * NSA_KV_W), wv)
    pek = _matmul(expand_pe(cmp_pe[0]), wk)
    pev = _matmul(expand_pe(cmp_pe[1]), wv)
    full = lambda shp: pl.BlockSpec(shp, lambda i: (0, 0))
    return pl.pallas_call(
        _nsa_cmp_finish_kernel,
        grid=(1,),
        in_specs=[full((n, 2 * NSA_KV_W)), full((n, 2 * NSA_KV_W)), full((8, 2 * NSA_KV_W)), full((8, 2 * NSA_KV_W)),
                  full((8, NSA_DH)), full((n, NSA_DH)), full((n, NSA_DH))],
        out_specs=[full((n, NSA_KV_W)), full((n, NSA_KV_W))],
        out_shape=[jax.ShapeDtypeStruct((n, NSA_KV_W), BF16)] * 2,
        scratch_shapes=[pltpu.VMEM((n + 8, NSA_KV_W), F32)],
        compiler_params=_cp("arbitrary"),
        name="nsa_cmp_finish",
    )(ak, av, pek, pev, gain8, ccos, csin)


def _nsa_attn_kernel(q_ref, kcmp_ref, vcmpT_ref, ks_ref, vsT_ref, kw_ref, vwT_ref, gate_ref, khot_ref,
                     o_ref, qT_ref, psum_ref, sel_ref, oc_ref, m_ref, l_ref, acc_ref, s_ref, *, top_n):
    g = pl.program_id(0)
    pb = pl.program_id(1)
    rows = NSA_GROUP * Q_BLOCK
    n_cmp = kcmp_ref.shape[0]
    n_slc = sel_ref.shape[1]
    streams = range(NSA_QPAIR)
    t0s = [(pb * NSA_QPAIR + i) * Q_BLOCK for i in streams]

    lane = lax.broadcasted_iota(jnp.int32, (1, rows), 1)
    q_io = lax.broadcasted_iota(jnp.int32, (1, Q_BLOCK), 1)
    n_io = lax.broadcasted_iota(jnp.int32, (n_cmp, 1), 0)
    m_io = lax.broadcasted_iota(jnp.int32, (n_slc, 1), 0)
    for i in streams:
        t0 = t0s[i]
        t_lane = t0 + (lane & (Q_BLOCK - 1))
        qf = q_ref[i * Q_BLOCK:(i + 1) * Q_BLOCK, :].astype(F32)
        qT = jnp.concatenate([qf[:, j * NSA_DH:(j + 1) * NSA_DH].T for j in range(NSA_GROUP)], axis=1).astype(BF16)
        qT_ref[i, 0:NSA_DH, :] = qT
        qT_ref[i, NSA_DH:2 * NSA_DH, :] = jnp.zeros((NSA_DH, rows), BF16)

        s = _dot(kcmp_ref[...], qT)
        valid = (n_io * CMP_STRIDE + (CMP_LEN - 1)) <= t_lane
        sm = jnp.where(valid, s, NEG_INF)
        mx = jnp.max(sm, axis=0, keepdims=True)
        e = jnp.where(valid, jnp.exp2(sm - mx), 0.0)
        den = jnp.sum(e, axis=0, keepdims=True)
        p = e * jnp.where(den > 0.0, 1.0 / den, 0.0)
        oc_ref[i] = _dot(vcmpT_ref[...], p.astype(BF16))
        psum = p[:, 0:Q_BLOCK]
        for j in range(1, NSA_GROUP):
            psum = psum + p[:, j * Q_BLOCK:(j + 1) * Q_BLOCK]
        psum_ref[i, 0:8, :] = jnp.zeros((8, Q_BLOCK), F32)
        psum_ref[i, 8:8 + n_cmp, :] = psum
        ratio = SLC_LEN // CMP_STRIDE
        imp = 0.5 * (psum_ref[i, pl.ds(7, n_slc, stride=ratio), :] + psum_ref[i, pl.ds(8 + ratio - 1, n_slc, stride=ratio), :])
        for c in range(ratio - 1):
            imp = imp + psum_ref[i, pl.ds(8 + c, n_slc, stride=ratio), :]

        cur = jnp.right_shift(t0 + q_io, 6)
        forced = (m_io == 0) | (m_io == cur) | (m_io == cur - 1)
        score = jnp.where(forced | (m_io > cur), -2e38, imp)
        for _ in range(top_n - 3):
            best = jnp.max(score, axis=0, keepdims=True)
            idx = jnp.min(jnp.where(score == best, m_io, n_slc), axis=0, keepdims=True)
            score = jnp.where(m_io == idx, -3e38, score)
        sel_ref[i] = jnp.where(forced | (score < -2.5e38), 0.0, NEG_INF)

    m_ref[...] = jnp.full(m_ref.shape, NEG_INF, F32)
    l_ref[...] = jnp.zeros(l_ref.shape, F32)
    acc_ref[...] = jnp.zeros(acc_ref.shape, F32)
    blocks_per_tile = SLC_TILE // SLC_LEN

    sub = SLC_TILE // 4

    def scores_into(kt, slot):
        k_tile = ks_ref[pl.ds(pl.multiple_of(kt * SLC_TILE, SLC_TILE), SLC_TILE), :]
        k_aug = jnp.concatenate([k_tile, khot_ref[slot]], axis=1)
        band = NSA_DH + 16 * slot
        for i in streams:
            selt = sel_ref[i, pl.ds(pl.multiple_of(kt * blocks_per_tile, blocks_per_tile), blocks_per_tile), :]
            rows8 = jnp.concatenate([selt] * NSA_GROUP, axis=1)
            qT_ref[i, band:band + 16, :] = jnp.concatenate([rows8, jnp.zeros_like(rows8)], axis=0).astype(BF16)
        for i in streams:
            for c in range(4):
                s_ref[i, slot, c * sub:(c + 1) * sub, :] = _dot(k_aug[c * sub:(c + 1) * sub, :], qT_ref[i])

    def absorb(kt, slot, causal):
        v_tile = vsT_ref[kt]
        for i in streams:
            sm = s_ref[i, slot]
            if causal:
                kpos = kt * SLC_TILE + lax.broadcasted_iota(jnp.int32, (SLC_TILE, rows), 0)
                sm = jnp.where(kpos <= t0s[i] + (lane & (Q_BLOCK - 1)), sm, NEG_INF)
            m_old = m_ref[i]
            m_new = jnp.maximum(m_old, jnp.max(sm, axis=0, keepdims=True))
            p = jnp.exp2(sm - m_new)
            alpha = jnp.exp2(m_old - m_new)
            l_ref[i] = alpha * l_ref[i] + jnp.sum(p, axis=0, keepdims=True)
            acc_ref[i] = alpha * acc_ref[i] + _dot(v_tile, p.astype(BF16))
            m_ref[i] = m_new

    last_kt = (t0s[0] + Q_BLOCK - 1) // SLC_TILE
    scores_into(0, 0)

    def slc_pair(kk, carry):
        kt = 2 * kk
        scores_into(kt + 1, 1)
        absorb(kt, 0, False)
        scores_into(kt + 2, 0)
        absorb(kt + 1, 1, False)
        return carry

    lax.fori_loop(0, last_kt // 2, slc_pair, 0)

    @pl.when(last_kt % 2 == 0)
    def _():
        absorb(last_kt, 0, True)

    @pl.when(last_kt % 2 == 1)
    def _():
        scores_into(last_kt, 1)
        absorb(last_kt - 1, 0, False)
        absorb(last_kt, 1, True)

    gT = gate_ref[...].T
    wlen = WIN + Q_BLOCK
    for i in streams:
        t0 = t0s[i]
        t_lane = t0 + (lane & (Q_BLOCK - 1))
        o_s = acc_ref[i] * (1.0 / l_ref[i])

        start = pl.multiple_of(jnp.maximum(t0 - WIN, 0), Q_BLOCK)
        s = _dot(kw_ref[pl.ds(start, wlen), :], qT_ref[i, 0:NSA_DH, :])
        kpos = start + lax.broadcasted_iota(jnp.int32, (wlen, 1), 0)
        d = t_lane - kpos
        mask = (d >= 0) & (d < WIN)
        sm = jnp.where(mask, s, NEG_INF)
        mx = jnp.max(sm, axis=0, keepdims=True)
        e = jnp.where(mask, jnp.exp2(sm - mx), 0.0)
        p = (e * (1.0 / jnp.sum(e, axis=0, keepdims=True))).astype(BF16)
        tile0 = start // Q_BLOCK
        o_w = _dot(vwT_ref[tile0], p[0:Q_BLOCK, :])
        for w in range(1, wlen // Q_BLOCK):
            o_w = o_w + _dot(vwT_ref[tile0 + w], p[w * Q_BLOCK:(w + 1) * Q_BLOCK, :])

        for j in range(NSA_GROUP):
            sl = slice(j * Q_BLOCK, (j + 1) * Q_BLOCK)

            def grow(b):
                r0 = b * NSA_HEADS + j
                r1 = b * NSA_HEADS + NSA_GROUP + j
                cols = slice(i * Q_BLOCK, (i + 1) * Q_BLOCK)
                return jnp.where(g == 0, gT[r0:r0 + 1, cols], gT[r1:r1 + 1, cols])

            o = grow(0) * oc_ref[i, :, sl] + grow(1) * o_s[:, sl] + grow(2) * o_w[:, sl]
            o_ref[i * Q_BLOCK:(i + 1) * Q_BLOCK, j * NSA_DH:(j + 1) * NSA_DH] = o.T.astype(BF16)


def _nsa_attention(q, kcmp, vcmp, ks, vsT, kw, vwT, gate):
    S = q.shape[0]
    n_cmp = kcmp.shape[0]
    n_slc = S // SLC_LEN
    n_qb = S // Q_BLOCK
    top_n = min(SLC_TOPN, n_slc)
    rows = NSA_GROUP * Q_BLOCK

    nq = NSA_QPAIR
    assert n_qb % nq == 0 and SLC_TILE % (nq * Q_BLOCK) == 0 and n_cmp * CMP_STRIDE == n_slc * SLC_LEN

    blk_of_key = np.arange(SLC_TILE)[:, None] // SLC_LEN
    khot = jnp.asarray(np.stack([blk_of_key + 16 * s == np.arange(NSA_DH)[None, :] for s in range(2)]), dtype=BF16)
    vcmpT = vcmp.reshape(n_cmp, NSA_KV_HEADS, NSA_DH).transpose(1, 2, 0)

    return pl.pallas_call(
        functools.partial(_nsa_attn_kernel, top_n=top_n),
        grid=(NSA_KV_HEADS, n_qb // nq),
        in_specs=[
            pl.BlockSpec((nq * Q_BLOCK, rows), lambda g, b: (b, g)),
            pl.BlockSpec((n_cmp, NSA_DH), lambda g, b: (0, g)),
            pl.BlockSpec((None, NSA_DH, n_cmp), lambda g, b: (g, 0, 0)),
            pl.BlockSpec((S, NSA_DH), lambda g, b: (0, g)),
            pl.BlockSpec((None, S // SLC_TILE, NSA_DH, SLC_TILE), lambda g, b: (g, 0, 0, 0)),
            pl.BlockSpec((S, NSA_DH), lambda g, b: (0, g)),
            pl.BlockSpec((None, S // Q_BLOCK, NSA_DH, Q_BLOCK), lambda g, b: (g, 0, 0, 0)),
            pl.BlockSpec((nq * Q_BLOCK, 128), lambda g, b: (b, 0)),
            pl.BlockSpec((2, SLC_TILE, NSA_DH), lambda g, b: (0, 0, 0)),
        ],
        out_specs=pl.BlockSpec((nq * Q_BLOCK, rows), lambda g, b: (b, g)),
        out_shape=jax.ShapeDtypeStruct((S, NSA_W), BF16),
        scratch_shapes=[
            pltpu.VMEM((nq, 2 * NSA_DH, rows), BF16),
            pltpu.VMEM((nq, n_cmp + 8, Q_BLOCK), F32),
            pltpu.VMEM((nq, n_slc, Q_BLOCK), F32),
            pltpu.VMEM((nq, NSA_DH, rows), F32),
            pltpu.VMEM((nq, 1, rows), F32),
            pltpu.VMEM((nq, 1, rows), F32),
            pltpu.VMEM((nq, NSA_DH, rows), F32),
            pltpu.VMEM((nq, 2, SLC_TILE, rows), F32),
        ],
        compiler_params=_cp("arbitrary", "arbitrary"),
        name="nsa_attn",
    )(q, kcmp, vcmpT, ks, vsT, kw, vwT, gate, khot)


def _rope_tables(pos):
    inv = ROPE_THETA ** (-(jnp.arange(0, NSA_DH, 2, dtype=F32) / NSA_DH))
    ang = pos[:, None] * inv[None, :]
    c, s = jnp.cos(ang), jnp.sin(ang)
    return jnp.concatenate([c, c], axis=-1), jnp.concatenate([-s, s], axis=-1)


def _nsa_mixer(x, g, w, qk_gain, cmp_w, cmp_pe):
    S = x.shape[0]
    gain8 = jnp.concatenate([qk_gain.astype(F32), jnp.zeros((4, NSA_DH), F32)], axis=0)
    cosf, sinf = _rope_tables(jnp.arange(S, dtype=F32))
    q, kc, vc, ks, vs, kw, vw, gate = _nsa_proj_prep(x, g, w, gain8, cosf, sinf)
    n = S // CMP_STRIDE
    ccos, csin = _rope_tables((jnp.arange(n) * CMP_STRIDE + CMP_LEN - 1).astype(F32))
    kcmp, vcmp = _nsa_compress(kc, vc, cmp_w, cmp_pe, gain8, ccos, csin)
    return _nsa_attention(q, kcmp, vcmp, ks, vs, kw, vw, gate)


def _head_ones():
    return jnp.asarray(np.kron(np.eye(128 // RW_DH), np.ones((RW_DH, RW_DH))), dtype=BF16)


def _seg_sum(x, ones_bd):
    outs = []
    for c in range(x.shape[1] // 128):
        p1, p2, p3 = _split3(x[:, c * 128:(c + 1) * 128])
        outs.append(_dot(p1, ones_bd) + _dot(p2, ones_bd) + _dot(p3, ones_bd))
    return jnp.concatenate(outs, axis=1)


def _rw_proj_prep_kernel(x_ref, gx_ref, w_ref, mu_ref, w0_ref, w2_ref, a0_ref, a2_ref, g2_ref, kk_ref, ka_ref, rk_ref,
                         bd_ref, r_ref, lw_ref, k_ref, v_ref, a_ref, b_ref, g_ref, bonus_ref, buf_ref):
    tm = x_ref.shape[0]
    W = RW_W

    @pl.when(pl.program_id(0) == 0)
    def _():
        buf_ref[0:8, :] = jnp.zeros((8, RW_COLS_PAD), F32)

    x = x_ref[...]
    ms = jnp.mean(x * x, axis=-1, keepdims=True)
    hn = (x * lax.rsqrt(ms + NORM_EPS) * gx_ref[...]).astype(BF16)

    def shifted(lo, hi):
        buf_ref[8:8 + tm, lo:hi] = _dot(hn, w_ref[:, lo:hi])
        u = buf_ref[8:8 + tm, lo:hi]
        return u + (buf_ref[pl.ds(7, tm), lo:hi] - u) * mu_ref[:, lo:hi]

    xl = shifted(3 * W, RW_COLS_PAD)
    wl, al, gl = xl[:, 0:LORA_PAD], xl[:, LORA_PAD:2 * LORA_PAD], xl[:, 2 * LORA_PAD:]
    xk = shifted(W, 2 * W)
    wx = w0_ref[...] + _dot(jnp.tanh(wl).astype(BF16), w2_ref[...])
    lw_ref[...] = -jnp.exp(-_softplus(-wx) - 0.5)
    a = _sigmoid(a0_ref[...] + _dot(al.astype(BF16), a2_ref[...]))
    g_ref[...] = _dot(_sigmoid(gl).astype(BF16), g2_ref[...])
    xr = shifted(0, W)
    bd = bd_ref[...]
    kk = xk * kk_ref[...]
    kk = kk / jnp.maximum(jnp.sqrt(_seg_sum(kk * kk, bd)), 1e-12)
    kf = xk * (1.0 + (a - 1.0) * ka_ref[...])
    k_ref[...] = kf
    a_ref[...] = -kk
    b_ref[...] = kk * a
    xv = shifted(2 * W, 3 * W)
    r_ref[...] = xr
    v_ref[...] = xv
    bonus_ref[...] = _seg_sum(xr * kf * rk_ref[...], bd) * xv
    buf_ref[0:8, :] = buf_ref[tm:tm + 8, :]


def _rw_proj_prep(x, g, w, mu, w0, w2, a0, a2, g2, k_k, k_a, r_k):
    S, D = x.shape
    tm = _pick(S, (256, 128, 64))
    W = RW_W
    vec = lambda n: pl.BlockSpec((1, n), lambda i: (0, 0))
    mat = lambda a, b: pl.BlockSpec((a, b), lambda i: (0, 0))
    row = pl.BlockSpec((tm, W), lambda i: (i, 0))
    out = jax.ShapeDtypeStruct((S, W), F32)
    return pl.pallas_call(
        _rw_proj_prep_kernel,
        grid=(S // tm,),
        in_specs=[
            pl.BlockSpec((tm, D), lambda i: (i, 0)), vec(D),
            pl.BlockSpec((D, RW_COLS_PAD), lambda i: (0, 0), pipeline_mode=pl.Buffered(1)),
            vec(RW_COLS_PAD), vec(W), mat(LORA_PAD, W), vec(W), mat(LORA_PAD, W), mat(RW_G_LORA, W),
            vec(W), vec(W), vec(W), mat(128, 128),
        ],
        out_specs=[row] * 8,
        out_shape=[out] * 8,
        scratch_shapes=[pltpu.VMEM((tm + 8, RW_COLS_PAD), F32)],
        compiler_params=_cp("arbitrary"),
        name="rw_proj_prep",
    )(x, g.reshape(1, D), w, mu, w0, w2, a0, a2, g2, k_k, k_a, r_k, _head_ones())


def _bd_rows(x, lane_head):
    lh = lane_head[0:x.shape[0], :]
    return jnp.concatenate([jnp.where(lh == h, x, jnp.zeros_like(x)) for h in range(RW_PACK)], axis=0)


def _diag_pack(full, lane_head):
    out = None
    for h in range(RW_PACK):
        blk = jnp.where(lane_head == h, full[h * RW_DH:(h + 1) * RW_DH, :], 0.0)
        out = blk if out is None else out + blk
    return out


def _rw_chunk_kernel(r_ref, lw_ref, k_ref, v_ref, a_ref, b_ref, lhs_ref, add_ref):
    for cb in range(r_ref.shape[0] // RW_CHUNK):
        _rw_chunk_body(r_ref, lw_ref, k_ref, v_ref, a_ref, b_ref, lhs_ref, add_ref, cb * RW_CHUNK)


def _rw_chunk_body(r_ref, lw_ref, k_ref, v_ref, a_ref, b_ref, lhs_ref, add_ref, r0):
    C = RW_CHUNK
    rs = slice(r0, r0 + C)
    ti = lax.broadcasted_iota(jnp.int32, (C, C), 0)
    si = lax.broadcasted_iota(jnp.int32, (C, C), 1)
    tri_incl = (si <= ti).astype(F32)
    GW = RW_PACK * RW_DH
    lane_head = lax.broadcasted_iota(jnp.int32, (C, GW), 1) // RW_DH
    lane_pos = lax.broadcasted_iota(jnp.int32, (C, GW), 1) % RW_DH
    row_t = lax.broadcasted_iota(jnp.int32, (C, GW), 0)
    eye_p = (lane_pos == row_t).astype(F32)
    strict = jnp.concatenate([lane_pos < row_t] * 2, axis=1)
    incl = jnp.concatenate([lane_pos <= row_t] * 2, axis=1)

    lw = lw_ref[rs, :]
    cs = _dot_hi(tri_incl, lw)
    cprev = cs - lw
    clast = cs[C - 1:C, :]
    e_neg = jnp.exp(-cs)
    e_end = jnp.exp(clast - cs)
    At = (a_ref[rs, :] * jnp.exp(cprev)).astype(BF16)
    Rt32 = r_ref[rs, :] * jnp.exp(cs)
    Rt = Rt32.astype(BF16)
    Bt = (b_ref[rs, :] * e_neg).astype(BF16)
    Kt = (k_ref[rs, :] * e_neg).astype(BF16)
    Bh = (b_ref[rs, :] * e_end).astype(BF16)
    Kh = (k_ref[rs, :] * e_end).astype(BF16)
    V = v_ref[rs, :].astype(BF16)
    gam = jnp.exp(clast)

    groups = range(RW_HEADS // RW_PACK)
    sls = [slice(gi * GW, (gi + 1) * GW) for gi in groups]
    bd = functools.partial(_bd_rows, lane_head=lane_head)
    G = [_dot_nt(jnp.concatenate([At[:, sl], Rt[:, sl]], axis=0),
                 jnp.concatenate([bd(Bt[:, sl]), bd(Kt[:, sl])], axis=0)) for sl in sls]
    top = [jnp.where(strict, g_[0:C, :], 0.0) for g_ in G]
    bot = [jnp.where(incl, g_[C:2 * C, :], 0.0).astype(BF16) for g_ in G]
    L = [t_[:, 0:GW] for t_ in top]
    AV = [_dot(jnp.concatenate([top[gi][:, GW:].astype(BF16), bot[gi][:, GW:]], axis=0), bd(V[:, sls[gi]])) for gi in groups]
    T = [eye_p + l_ for l_ in L]
    P = [_dot(l_.astype(BF16), bd(l_.astype(BF16))) for l_ in L]
    for _ in range(4):
        TP = [_dot(jnp.concatenate([T[gi].astype(BF16), P[gi].astype(BF16)], axis=0), bd(P[gi].astype(BF16)))
              for gi in groups]
        T = [T[gi] + TP[gi][0:C, :] for gi in groups]
        P = [tp[C:2 * C, :] for tp in TP]
    T = [T[gi] + _dot(T[gi].astype(BF16), bd(P[gi].astype(BF16))) for gi in groups]
    AU = [_dot(T[gi].astype(BF16), jnp.concatenate([bd(At[:, sls[gi]]), bd(AV[gi][0:C, :].astype(BF16))], axis=1))
          for gi in groups]
    Ahat = [au[:, 0:GW].astype(BF16) for au in AU]
    Uhat = [au[:, GW:].astype(BF16) for au in AU]
    RY = [_dot(bot[gi][:, 0:GW], jnp.concatenate([bd(Ahat[gi]), bd(Uhat[gi])], axis=1)) for gi in groups]
    phi_full = [_dot_tn(Bh[:, sls[gi]], Ahat[gi]) for gi in groups]
    hh_full = [_dot_tn(jnp.concatenate([Bh[:, sls[gi]], Kh[:, sls[gi]]], axis=0),
                       jnp.concatenate([Uhat[gi], V[:, sls[gi]]], axis=0)) for gi in groups]
    for gi in groups:
        sl = sls[gi]
        o0 = 2 * r0
        lhs_ref[o0:o0 + C, sl] = Rt32[:, sl] + RY[gi][:, 0:GW]
        add_ref[o0:o0 + C, sl] = RY[gi][:, GW:] + AV[gi][C:2 * C, :]
        lhs_ref[o0 + C:o0 + 2 * C, sl] = eye_p * gam[:, sl] + _diag_pack(phi_full[gi], lane_head)
        add_ref[o0 + C:o0 + 2 * C, sl] = _diag_pack(hh_full[gi], lane_head)


def _rw_chunks(r, lw, k, v, a, b):
    S = r.shape[0]
    C = RW_CHUNK * _pick(S // RW_CHUNK, (2, 1))
    row = pl.BlockSpec((C, RW_W), lambda c: (c, 0))
    out = pl.BlockSpec((2 * C, RW_W), lambda c: (c, 0))
    shp = jax.ShapeDtypeStruct((2 * S, RW_W), F32)
    return pl.pallas_call(
        _rw_chunk_kernel,
        grid=(S // C,),
        in_specs=[row] * 6,
        out_specs=[out, out],
        out_shape=[shp, shp],
        compiler_params=_cp("parallel"),
        name="rw_chunk",
    )(r, lw, k, v, a, b)


def _rw_scan_kernel(lhs_ref, add_ref, bonus_ref, g_ref, lnw_ref, lnb_ref, bd_ref, y_ref, h_ref):
    C = RW_CHUNK

    @pl.when(pl.program_id(0) == 0)
    def _():
        h_ref[...] = jnp.zeros_like(h_ref)

    GW = RW_PACK * RW_DH
    lane_head = lax.broadcasted_iota(jnp.int32, (RW_DH, GW), 1) // RW_DH
    groups = range(RW_HEADS // RW_PACK)
    gsls = [slice(gi * GW, (gi + 1) * GW) for gi in groups]
    bd = bd_ref[...]
    ys = []
    for cb in range(y_ref.shape[0] // C):
        two = slice(2 * C * cb, 2 * C * (cb + 1))
        allres = [_dot(lhs_ref[two, gsl].astype(BF16), _bd_rows(h_ref[:, gsl].astype(BF16), lane_head))
                  + add_ref[two, gsl] for gsl in gsls]
        for gi in groups:
            h_ref[:, gsls[gi]] = allres[gi][C:2 * C, :]
        ys.append(jnp.concatenate([allres[gi][0:C, :] for gi in groups], axis=1))
    y = jnp.concatenate(ys, axis=0)
    yc = y - _seg_sum(y, bd) * (1.0 / RW_DH)
    var = _seg_sum(yc * yc, bd) * (1.0 / RW_DH)
    yn = yc * lax.rsqrt(var + RW_LNX_EPS) * lnw_ref[...] + lnb_ref[...]
    y_ref[...] = ((yn + bonus_ref[...]) * g_ref[...]).astype(BF16)


def _rw_scan(lhs, add, bonus, g, lnx_w, lnx_b):
    S = bonus.shape[0]
    C = RW_CHUNK * _pick(S // RW_CHUNK, (4, 2, 1))
    row = pl.BlockSpec((C, RW_W), lambda c: (c, 0))
    two = pl.BlockSpec((2 * C, RW_W), lambda c: (c, 0))
    vec = pl.BlockSpec((1, RW_W), lambda c: (0, 0))
    return pl.pallas_call(
        _rw_scan_kernel,
        grid=(S // C,),
        in_specs=[two, two, row, row, vec, vec, pl.BlockSpec((128, 128), lambda c: (0, 0))],
        out_specs=row,
        out_shape=jax.ShapeDtypeStruct((S, RW_W), BF16),
        scratch_shapes=[pltpu.VMEM((RW_DH, RW_W), F32)],
        compiler_params=_cp("arbitrary"),
        name="rw_scan",
    )(lhs, add, bonus, g, lnx_w, lnx_b, _head_ones())


def _pad_lora_cols(x, axis):
    W3 = 3 * RW_W
    parts = [lax.slice_in_dim(x, 0, W3, axis=axis),
             lax.slice_in_dim(x, W3, W3 + RW_DECAY_LORA, axis=axis),
             lax.slice_in_dim(x, W3 + RW_DECAY_LORA, W3 + RW_DECAY_LORA + RW_A_LORA, axis=axis),
             lax.slice_in_dim(x, W3 + RW_DECAY_LORA + RW_A_LORA, RW_COLS, axis=axis)]

    def padto(p, n):
        cfg = [(0, 0)] * x.ndim
        cfg[axis] = (0, n - p.shape[axis])
        return jnp.pad(p, cfg)

    return jnp.concatenate([parts[0], padto(parts[1], LORA_PAD), padto(parts[2], LORA_PAD), parts[3]], axis=axis)


def _rwkv_mixer(x, gx, w, mu, w0, w2, a0, a2, g2, k_k, k_a, r_k, lnx_w, lnx_b):
    row = lambda p: p.reshape(1, -1).astype(F32)
    padrows = lambda m: jnp.pad(m, ((0, LORA_PAD - m.shape[0]), (0, 0))).astype(BF16)
    r, lw, k, v, a, b, g, bonus = _rw_proj_prep(
        x, gx, w, _pad_lora_cols(row(mu), 1), row(w0), padrows(w2), row(a0), padrows(a2), g2.astype(BF16),
        row(k_k), row(k_a), row(r_k))
    lhs, add = _rw_chunks(r, lw, k, v, a, b)
    return _rw_scan(lhs, add, bonus, g, row(lnx_w), row(lnx_b))


def _lru_kernel(u_ref, cw_ref, cb_ref, wa_ref, ba_ref, wx_ref, bx_ref, sp_ref, y_ref, buf_ref, a_sc, b_sc, h_sc):
    i = pl.program_id(0)
    tm = u_ref.shape[0]

    @pl.when(i == 0)
    def _():
        buf_ref[0:8, :] = jnp.zeros((8, LRU_W), F32)
        h_sc[...] = jnp.zeros_like(h_sc)

    xb = u_ref[:, LRU_W:]
    buf_ref[8:8 + tm, :] = xb
    xc = cb_ref[...] + cw_ref[CONV_W - 1:CONV_W, :] * xb
    for kk in range(CONV_W - 1):
        xc = xc + cw_ref[kk:kk + 1, :] * buf_ref[pl.ds(8 - (CONV_W - 1) + kk, tm), :]
    buf_ref[0:8, :] = xb[tm - 8:tm, :]

    xcb = xc.astype(BF16)
    ra, ri = [], []
    for n in range(LRU_BLOCKS):
        sl = slice(n * LRU_BW, (n + 1) * LRU_BW)
        ra.append(_dot(xcb[:, sl], wa_ref[n]))
        ri.append(_dot(xcb[:, sl], wx_ref[n]))
    rg = _sigmoid(jnp.concatenate(ra, axis=1) + ba_ref[...])
    ig = _sigmoid(jnp.concatenate(ri, axis=1) + bx_ref[...])
    log_a = -LRU_C * rg * sp_ref[...]
    a_sc[...] = jnp.exp(log_a)
    b_sc[...] = jnp.sqrt(1.0 - jnp.exp(2.0 * log_a)) * ig * xc

    rowi = lax.broadcasted_iota(jnp.int32, (8, LRU_W), 0)

    def group(gidx, h):
        r0 = pl.multiple_of(gidx * 8, 8)
        a = a_sc[pl.ds(r0, 8), :]
        b = b_sc[pl.ds(r0, 8), :]
        for d in (1, 2, 4):
            keep = rowi >= d
            b = jnp.where(keep, a * pltpu.roll(b, d, 0) + b, b)
            a = jnp.where(keep, a * pltpu.roll(a, d, 0), a)
        hrows = a * h + b
        b_sc[pl.ds(r0, 8), :] = hrows
        return hrows[7:8, :]

    h_sc[...] = lax.fori_loop(0, tm // 8, group, h_sc[...], unroll=4)
    gate = u_ref[:, 0:LRU_W]
    gelu = 0.5 * gate * (1.0 + jnp.tanh(0.7978845608028654 * (gate + 0.044715 * gate * gate * gate)))
    y_ref[...] = (b_sc[...] * gelu).astype(BF16)


def _lru_mixer(u, conv_w, conv_b, wa, ba, wx, bx, lam):
    S = u.shape[0]
    tm = _pick(S, (256, 128, 64))
    row = lambda p: p.reshape(1, -1).astype(F32)
    vec = pl.BlockSpec((1, LRU_W), lambda i: (0, 0))
    blk = pl.BlockSpec((LRU_BLOCKS, LRU_BW, LRU_BW), lambda i: (0, 0, 0))
    lamf = lam.astype(F32)
    softplus_neg_lam = row(jnp.maximum(-lamf, 0.0) + jnp.log1p(jnp.exp(-jnp.abs(lamf))))
    cw = jnp.concatenate([conv_w.astype(F32), jnp.zeros((8 - CONV_W, LRU_W), F32)], axis=0)
    return pl.pallas_call(
        _lru_kernel,
        grid=(S // tm,),
        in_specs=[pl.BlockSpec((tm, 2 * LRU_W), lambda i: (i, 0)), pl.BlockSpec((8, LRU_W), lambda i: (0, 0)),
                  vec, blk, vec, blk, vec, vec],
        out_specs=pl.BlockSpec((tm, LRU_W), lambda i: (i, 0)),
        out_shape=jax.ShapeDtypeStruct((S, LRU_W), BF16),
        scratch_shapes=[pltpu.VMEM((tm + 8, LRU_W), F32), pltpu.VMEM((tm, LRU_W), F32),
                        pltpu.VMEM((tm, LRU_W), F32), pltpu.VMEM((1, LRU_W), F32)],
        compiler_params=_cp("arbitrary"),
        name="rglru",
    )(u, cw, row(conv_b), wa.astype(BF16), row(ba), wx.astype(BF16), row(bx), softplus_neg_lam)


def _hgrn_kernel(u_ref, lower_ref, ng_ref, y_ref, st_ref):
    C = HG_CHUNK
    W = HG_KW

    @pl.when(pl.program_id(0) == 0)
    def _():
        st_ref[...] = jnp.zeros_like(st_ref)

    for cb in range(u_ref.shape[0] // C):
        _hgrn_chunk(u_ref, lower_ref, ng_ref, y_ref, st_ref, cb * C)


def _hgrn_chunk(u_ref, lower_ref, ng_ref, y_ref, st_ref, r0):
    C = HG_CHUNK
    W = HG_KW
    rs = slice(r0, r0 + C)
    lower = lower_ref[...]
    forget = lower + (1.0 - lower) * _sigmoid(u_ref[rs, W:2 * W])
    lf = jnp.log(forget)
    kk = 1.0 - forget
    qx = u_ref[rs, 0:W]
    q = qx * _sigmoid(qx)
    v = u_ref[rs, 2 * W:3 * W]
    gx = u_ref[rs, 3 * W:4 * W]

    ti = lax.broadcasted_iota(jnp.int32, (C, C), 0)
    si = lax.broadcasted_iota(jnp.int32, (C, C), 1)
    b = _dot_hi((si <= ti).astype(F32), lf)
    blast = b[C - 1:C, :]
    q_in = (q * jnp.exp(b)).astype(BF16)
    k_end = (kk * jnp.exp(blast - b)).astype(BF16)
    vb = v.astype(BF16)
    rowi = lax.broadcasted_iota(jnp.int32, (C, W), 0)

    n_sub = C // HG_SUB
    refs = [jnp.zeros((1, W), F32)] + [b[i * HG_SUB - 1:i * HG_SUB, :] for i in range(1, n_sub)]
    ref_rows = jnp.concatenate([jnp.broadcast_to(r_, (HG_SUB, W)) for r_ in refs], axis=0)
    q_sub = (q * jnp.exp(b - ref_rows)).astype(BF16)
    k_sub = []
    for i in range(n_sub):
        live = rowi < (i + 1) * HG_SUB
        k_sub.append(jnp.where(live, kk * jnp.exp(jnp.where(live, refs[i] - b, 0.0)), 0.0).astype(BF16))
    k_sub = jnp.concatenate(k_sub, axis=0)
    row_blk = ti // HG_SUB

    heads = range(HG_HEADS)
    sls = [slice(h * HG_DK, (h + 1) * HG_DK) for h in heads]
    states = [st_ref[:, sl] for sl in sls]
    inter = [_dot_nt(q_in[:, sls[h]], states[h].astype(BF16)) for h in heads]
    pair = [_dot_nt(q_sub[:, sl], k_sub[:, sl]) for sl in sls]
    att = []
    for h in heads:
        a_h = jnp.zeros((C, C), F32)
        for i in range(n_sub):
            a_h = jnp.where((row_blk == i) & (si <= ti), pair[h][:, i * C:(i + 1) * C], a_h)
        att.append(a_h.astype(BF16))
    intra = [_dot(att[h], vb[:, sls[h]]) for h in heads]
    upd = [_dot_tn(vb[:, sl], k_end[:, sl]) for sl in sls]
    for h in heads:
        sl = sls[h]
        st_ref[:, sl] = jnp.exp(blast[:, sl]) * states[h] + upd[h]
        o = inter[h] + intra[h]
        ms = jnp.mean(o * o, axis=-1, keepdims=True)
        on = o * lax.rsqrt(ms + NORM_EPS) * ng_ref[:, sl]
        gh = gx[:, sl]
        y_ref[rs, sl] = (on * (gh * _sigmoid(gh))).astype(BF16)


def _hgrn_mixer(u, lower, norm_g):
    S = u.shape[0]
    C = HG_CHUNK * _pick(S // HG_CHUNK, (4, 2, 1))
    vec = pl.BlockSpec((1, HG_KW), lambda c: (0, 0))
    return pl.pallas_call(
        _hgrn_kernel,
        grid=(S // C,),
        in_specs=[pl.BlockSpec((C, 4 * HG_KW), lambda c: (c, 0)), vec, vec],
        out_specs=pl.BlockSpec((C, HG_VW), lambda c: (c, 0)),
        out_shape=jax.ShapeDtypeStruct((S, HG_VW), BF16),
        scratch_shapes=[pltpu.VMEM((HG_DV, HG_KW), F32)],
        compiler_params=_cp("arbitrary"),
        name="hgrn2",
    )(u, lower.reshape(1, -1).astype(F32), norm_g.reshape(1, -1).astype(F32))


def kernel(x, norm_mix, norm_mlp, w_ff1, w_ff2, w_in_a, w_out_a, nsa_qk_gain, nsa_cmp_w, nsa_cmp_pe, rw_mu, rw_w0, rw_w2, rw_a0, rw_a2, rw_g2, rw_k_k, rw_k_a, rw_r_k, rw_lnx_w, rw_lnx_b, w_in_b, w_out_b, lru_conv_w, lru_conv_b, lru_wa, lru_ba, lru_wx, lru_bx, lru_lambda, hg_lb, hg_norm):
    B, S, D = x.shape
    depth = norm_mix.shape[0]
    lb_p = jax.nn.softmax(hg_lb.astype(F32), axis=0)
    lb_cum = jnp.cumsum(lb_p, axis=0)
    hg_lower = lb_cum - lb_cum[0:1]
    w_ff1_b, w_ff2_b = w_ff1.astype(BF16), w_ff2.astype(BF16)

    outs = []
    for bi in range(B):
        xb = x[bi].astype(F32)
        for layer in range(depth):
            gmix = norm_mix[layer].astype(F32)
            if layer % 2 == 0:
                e = layer // 2
                w_in = w_in_a[e].astype(BF16)
                w_nsa = jnp.pad(w_in[:, :NSA_COLS], ((0, 0), (0, NSA_COLS_PAD - NSA_COLS)))
                w_rw = _pad_lora_cols(w_in[:, NSA_COLS:], 1)
                y_a = _nsa_mixer(xb, gmix, w_nsa, nsa_qk_gain[e], nsa_cmp_w[e], nsa_cmp_pe[e])
                y_b = _rwkv_mixer(xb, gmix, w_rw, rw_mu[e], rw_w0[e], rw_w2[e], rw_a0[e], rw_a2[e],
                                  rw_g2[e], rw_k_k[e], rw_k_a[e], rw_r_k[e], rw_lnx_w[e], rw_lnx_b[e])
                xb = _out_proj(y_a, y_b, w_out_a[e].astype(BF16), xb)
            else:
                o = layer // 2
                w_in = w_in_b[o].astype(BF16)
                y_c = _lru_mixer(_norm_matmul(xb, gmix, w_in[:, :2 * LRU_W]), lru_conv_w[o], lru_conv_b[o],
                                 lru_wa[o], lru_ba[o], lru_wx[o], lru_bx[o], lru_lambda[o])
                y_d = _hgrn_mixer(_norm_matmul(xb, gmix, w_in[:, 2 * LRU_W:]), hg_lower[layer], hg_norm[o])
                xb = _out_proj(y_c, y_d, w_out_b[o].astype(BF16), xb)
            xb = _mlp(xb, norm_mlp[layer].astype(F32), w_ff1_b, w_ff2_b, layer)
        outs.append(xb.astype(x.dtype))
    return outs[0].reshape(1, S, D) if B == 1 else jnp.stack(outs, axis=0)
```

```python
import functools

import jax
import jax.numpy as jnp
import numpy as np
from jax import lax
from jax.experimental import pallas as pl
from jax.experimental.pallas import tpu as pltpu

F32 = jnp.float32
BF16 = jnp.bfloat16
HI = lax.Precision.HIGHEST

NORM_EPS = 1e-6
NEG_INF = -1e30
ROPE_THETA = 10000.0
LOG2E = 1.4426950408889634

NSA_HEADS = 8
NSA_KV_HEADS = 2
NSA_GROUP = NSA_HEADS // NSA_KV_HEADS
NSA_DH = 128
CMP_LEN = 32
CMP_STRIDE = 16
SLC_LEN = 64
SLC_TOPN = 16
WIN = 512
Q_BLOCK = 128
NSA_W = NSA_HEADS * NSA_DH
NSA_KV_W = NSA_KV_HEADS * NSA_DH
NSA_COLS = NSA_W + 6 * NSA_KV_W + 3 * NSA_HEADS
NSA_COLS_PAD = NSA_W + 6 * NSA_KV_W + 128
SLC_TILE = 512
NSA_QPAIR = 2

RW_HEADS = 16
RW_DH = 64
RW_W = RW_HEADS * RW_DH
RW_DECAY_LORA = 96
RW_A_LORA = 96
RW_G_LORA = 256
RW_LNX_EPS = 64e-5
RW_COLS = 3 * RW_W + RW_DECAY_LORA + RW_A_LORA + RW_G_LORA
LORA_PAD = 128
RW_COLS_PAD = 3 * RW_W + 2 * LORA_PAD + RW_G_LORA
RW_CHUNK = 64
RW_PACK = 4

LRU_W = 1024
LRU_BLOCKS = 8
LRU_BW = LRU_W // LRU_BLOCKS
CONV_W = 4
LRU_C = 8.0

HG_HEADS = 8
HG_DK = 128
HG_DV = 128
HG_KW = HG_HEADS * HG_DK
HG_VW = HG_HEADS * HG_DV
HG_CHUNK = 64
HG_SUB = 16

MXU_WIDTH = 256
VMEM_LIMIT = 56 * 1024 * 1024


def _cp(*sem):
    return pltpu.CompilerParams(dimension_semantics=sem, vmem_limit_bytes=VMEM_LIMIT)


def _pick(n, cands):
    for c in cands:
        if n % c == 0:
            return c
    return n


def _sigmoid(x):
    return 1.0 / (1.0 + jnp.exp(-x))


def _softplus(x):
    return jnp.maximum(x, 0.0) + jnp.log(1.0 + jnp.exp(-jnp.abs(x)))


def _split3(x):
    p1 = x.astype(BF16)
    r1 = x - p1.astype(F32)
    p2 = r1.astype(BF16)
    p3 = (r1 - p2.astype(F32)).astype(BF16)
    return p1, p2, p3


def _dot(a, b):
    return jnp.dot(a, b, preferred_element_type=F32)


def _dot_hi(a, b):
    return jnp.dot(a, b, preferred_element_type=F32, precision=HI)


def _dot_nt(a, b):
    return lax.dot_general(a, b, (((1,), (1,)), ((), ())), preferred_element_type=F32)


def _dot_tn(a, b):
    return lax.dot_general(a, b, (((0,), (0,)), ((), ())), preferred_element_type=F32)


def _norm_matmul_kernel(x_ref, g_ref, w_ref, o_ref, *, tn):
    x = x_ref[...]
    ms = jnp.mean(x * x, axis=-1, keepdims=True)
    hn = (x * lax.rsqrt(ms + NORM_EPS) * g_ref[...]).astype(BF16)
    n = w_ref.shape[1]
    for lo in range(0, n, tn):
        hi = min(lo + tn, n)
        o_ref[:, lo:hi] = _dot(hn, w_ref[:, lo:hi])


def _norm_matmul(x, g, w):
    S, D = x.shape
    N = w.shape[1]
    tm = _pick(S, (512, 256, 128))
    tn = 3 * MXU_WIDTH
    return pl.pallas_call(
        functools.partial(_norm_matmul_kernel, tn=tn),
        grid=(S // tm,),
        in_specs=[
            pl.BlockSpec((tm, D), lambda i: (i, 0)),
            pl.BlockSpec((1, D), lambda i: (0, 0)),
            pl.BlockSpec((D, N), lambda i: (0, 0), pipeline_mode=pl.Buffered(1)),
        ],
        out_specs=pl.BlockSpec((tm, N), lambda i: (i, 0)),
        out_shape=jax.ShapeDtypeStruct((S, N), F32),
        compiler_params=_cp("parallel"),
        name="norm_matmul",
    )(x, g.reshape(1, D), w)


def _matmul_kernel(a_ref, w_ref, o_ref):
    o_ref[...] = _dot(a_ref[...], w_ref[...])


def _matmul(a, w):
    M, K = a.shape
    N = w.shape[1]
    tm = _pick(M, (256, 128, 64, 32, 16, 8))
    return pl.pallas_call(
        _matmul_kernel,
        grid=(M // tm,),
        in_specs=[pl.BlockSpec((tm, K), lambda i: (i, 0)), pl.BlockSpec((K, N), lambda i: (0, 0))],
        out_specs=pl.BlockSpec((tm, N), lambda i: (i, 0)),
        out_shape=jax.ShapeDtypeStruct((M, N), F32),
        compiler_params=_cp("parallel"),
        name="matmul",
    )(a, w)


def _out_proj_kernel(a_ref, b_ref, wa_ref, wb_ref, r_ref, o_ref):
    o_ref[...] = r_ref[...] + _dot(a_ref[...], wa_ref[...]) + _dot(b_ref[...], wb_ref[...])


def _out_proj(ya, yb, w, res):
    S, Wa = ya.shape
    Wb = yb.shape[1]
    D = w.shape[1]
    tm = _pick(S, (512, 256, 128))
    return pl.pallas_call(
        _out_proj_kernel,
        grid=(S // tm,),
        in_specs=[
            pl.BlockSpec((tm, Wa), lambda i: (i, 0)),
            pl.BlockSpec((tm, Wb), lambda i: (i, 0)),
            pl.BlockSpec((Wa, D), lambda i: (0, 0)),
            pl.BlockSpec((Wb, D), lambda i: (0, 0)),
            pl.BlockSpec((tm, D), lambda i: (i, 0)),
        ],
        out_specs=pl.BlockSpec((tm, D), lambda i: (i, 0)),
        out_shape=jax.ShapeDtypeStruct((S, D), F32),
        compiler_params=_cp("parallel"),
        name="out_proj",
    )(ya, yb, w[:Wa], w[Wa:], res)


def _mlp_kernel(x_ref, g_ref, w1_ref, w2_ref, o_ref, hn_ref):
    f = pl.program_id(1)

    @pl.when(f == 0)
    def _():
        x = x_ref[...]
        ms = jnp.mean(x * x, axis=-1, keepdims=True)
        hn_ref[...] = (x * lax.rsqrt(ms + NORM_EPS) * g_ref[...]).astype(BF16)
        o_ref[...] = x

    hn = hn_ref[...]
    half = w1_ref.shape[1] // 2
    zs = [jnp.maximum(_dot(hn, w1_ref[:, c * half:(c + 1) * half]), 0.0) for c in range(2)]
    acc = _dot((zs[0] * zs[0]).astype(BF16), w2_ref[0:half, :])
    acc = acc + _dot((zs[1] * zs[1]).astype(BF16), w2_ref[half:2 * half, :])
    o_ref[...] += acc


def _mlp(x, g, w1, w2, layer):
    S, D = x.shape
    Fdim = w1.shape[2]
    tm = _pick(S, (1024, 512, 256, 128))
    tf = _pick(Fdim, (512, 256, 128))
    return pl.pallas_call(
        _mlp_kernel,
        grid=(S // tm, Fdim // tf),
        in_specs=[
            pl.BlockSpec((tm, D), lambda i, f: (i, 0)),
            pl.BlockSpec((1, D), lambda i, f: (0, 0)),
            pl.BlockSpec((None, D, tf), lambda i, f: (layer, 0, f)),
            pl.BlockSpec((None, tf, D), lambda i, f: (layer, f, 0)),
        ],
        out_specs=pl.BlockSpec((tm, D), lambda i, f: (i, 0)),
        out_shape=jax.ShapeDtypeStruct((S, D), F32),
        scratch_shapes=[pltpu.VMEM((tm, D), BF16)],
        compiler_params=_cp("parallel", "arbitrary"),
        name="mlp",
    )(x, g.reshape(1, D), w1, w2)


def _head_norm_rope(x, gain, cosf, sinf):
    ms = jnp.mean(x * x, axis=-1, keepdims=True)
    y = x * lax.rsqrt(ms + NORM_EPS) * gain
    return y * cosf + pltpu.roll(y, NSA_DH // 2, 1) * sinf


def _nsa_prep_kernel(u_ref, gain_ref, cos_ref, sin_ref,
                     q_ref, kc_ref, vc_ref, ks_ref, vs_ref, kw_ref, vw_ref, gate_ref):
    cosf = cos_ref[...]
    sinf = sin_ref[...]
    scale = NSA_DH ** -0.5 * LOG2E
    for h in range(NSA_HEADS):
        sl = slice(h * NSA_DH, (h + 1) * NSA_DH)
        q_ref[:, sl] = (_head_norm_rope(u_ref[:, sl], gain_ref[0:1, :], cosf, sinf) * scale).astype(BF16)
    base = NSA_W
    kc_ref[...] = u_ref[:, base:base + NSA_KV_W].astype(BF16)
    vc_ref[...] = u_ref[:, base + NSA_KV_W:base + 2 * NSA_KV_W].astype(BF16)
    tm = u_ref.shape[0]
    for g in range(NSA_KV_HEADS):
        c_s = base + 3 * NSA_KV_W + g * NSA_DH
        c_w = base + 5 * NSA_KV_W + g * NSA_DH
        for c in range(tm // SLC_TILE):
            vs_ref[g, c] = u_ref[c * SLC_TILE:(c + 1) * SLC_TILE, c_s:c_s + NSA_DH].T.astype(BF16)
        for c in range(tm // Q_BLOCK):
            vw_ref[g, c] = u_ref[c * Q_BLOCK:(c + 1) * Q_BLOCK, c_w:c_w + NSA_DH].T.astype(BF16)
    for g in range(NSA_KV_HEADS):
        sl = slice(g * NSA_DH, (g + 1) * NSA_DH)
        o_s = base + 2 * NSA_KV_W + g * NSA_DH
        o_w = base + 4 * NSA_KV_W + g * NSA_DH
        ks_ref[:, sl] = _head_norm_rope(u_ref[:, o_s:o_s + NSA_DH], gain_ref[2:3, :], cosf, sinf).astype(BF16)
        kw_ref[:, sl] = _head_norm_rope(u_ref[:, o_w:o_w + NSA_DH], gain_ref[3:4, :], cosf, sinf).astype(BF16)
    gate_ref[...] = _sigmoid(u_ref[:, base + 6 * NSA_KV_W:base + 6 * NSA_KV_W + 128])


def _nsa_proj_prep_kernel(x_ref, g_ref, w_ref, gain_ref, cos_ref, sin_ref, *rest):
    outs, u_scr = rest[:-1], rest[-1]
    _norm_matmul_kernel(x_ref, g_ref, w_ref, u_scr, tn=3 * MXU_WIDTH)
    _nsa_prep_kernel(u_scr, gain_ref, cos_ref, sin_ref, *outs)


def _nsa_proj_prep(x, g, w, gain8, cosf, sinf):
    S, D = x.shape
    tm = SLC_TILE
    assert S % tm == 0
    row = lambda w_: pl.BlockSpec((tm, w_), lambda i: (i, 0))
    kv = jax.ShapeDtypeStruct((S, NSA_KV_W), BF16)
    G = NSA_KV_HEADS

    def vT(width):
        return (pl.BlockSpec((G, tm // width, NSA_DH, width), lambda i: (0, i, 0, 0)),
                jax.ShapeDtypeStruct((G, S // width, NSA_DH, width), BF16))

    (vs_spec, vs_shape), (vw_spec, vw_shape) = vT(SLC_TILE), vT(Q_BLOCK)
    return pl.pallas_call(
        _nsa_proj_prep_kernel,
        grid=(S // tm,),
        in_specs=[row(D), pl.BlockSpec((1, D), lambda i: (0, 0)),
                  pl.BlockSpec((D, NSA_COLS_PAD), lambda i: (0, 0), pipeline_mode=pl.Buffered(1)),
                  pl.BlockSpec((8, NSA_DH), lambda i: (0, 0)), row(NSA_DH), row(NSA_DH)],
        out_specs=[row(NSA_W), row(NSA_KV_W), row(NSA_KV_W), row(NSA_KV_W), vs_spec, row(NSA_KV_W), vw_spec, row(128)],
        out_shape=[jax.ShapeDtypeStruct((S, NSA_W), BF16), kv, kv, kv, vs_shape, kv, vw_shape,
                   jax.ShapeDtypeStruct((S, 128), F32)],
        scratch_shapes=[pltpu.VMEM((tm, NSA_COLS_PAD), F32)],
        compiler_params=_cp("parallel"),
        name="nsa_proj_prep",
    )(x, g.reshape(1, D), w, gain8, cosf, sinf)


def _nsa_cmp_finish_kernel(ak_ref, av_ref, pek_ref, pev_ref, gain_ref, cos_ref, sin_ref,
                           kcmp_ref, vcmp_ref, buf_ref):
    n = ak_ref.shape[0]
    buf_ref[n:n + 8, :] = jnp.zeros((8, NSA_KV_W), F32)

    def combine(a_ref, pe_ref):
        buf_ref[0:n, :] = a_ref[:, NSA_KV_W:]
        pe = pe_ref[0:1, 0:NSA_KV_W] + pe_ref[1:2, NSA_KV_W:]
        return a_ref[:, 0:NSA_KV_W] + buf_ref[pl.ds(1, n), :] + pe

    kc = combine(ak_ref, pek_ref)
    for g in range(NSA_KV_HEADS):
        sl = slice(g * NSA_DH, (g + 1) * NSA_DH)
        kcmp_ref[:, sl] = _head_norm_rope(kc[:, sl], gain_ref[1:2, :], cos_ref[...], sin_ref[...]).astype(BF16)
    vcmp_ref[...] = combine(av_ref, pev_ref).astype(BF16)


def _nsa_compress(kc, vc, cmp_w, cmp_pe, gain8, ccos, csin):
    S = kc.shape[0]
    n = S // CMP_STRIDE
    half = CMP_LEN // 2
    eye = jnp.eye(NSA_KV_HEADS, dtype=F32)

    def expand(w):
        lo = jnp.einsum('lde,gh->lgdhe', w[:half], eye).reshape(half * NSA_KV_W, NSA_KV_W)
        hi = jnp.einsum('lde,gh->lgdhe', w[half:], eye).reshape(half * NSA_KV_W, NSA_KV_W)
        return jnp.concatenate([lo, hi], axis=1).astype(BF16)

    def expand_pe(pe):
        lo = jnp.broadcast_to(pe[:half, None, :], (half, NSA_KV_HEADS, NSA_DH)).reshape(1, -1)
        hi = jnp.broadcast_to(pe[half:, None, :], (half, NSA_KV_HEADS, NSA_DH)).reshape(1, -1)
        return jnp.concatenate([lo, hi, jnp.zeros((6, lo.shape[1]), F32)], axis=0).astype(BF16)

    wk, wv = expand(cmp_w[0]), expand(cmp_w[1])
    ak = _matmul(kc.reshape(n, CMP_STRIDE * NSA_KV_W), wk)
    av = _matmul(vc.reshape(n, CMP_STRIDE * NSA_KV_W), wv)
    pek = _matmul(expand_pe(cmp_pe[0]), wk)
    pev = _matmul(expand_pe(cmp_pe[1]), wv)
    full = lambda shp: pl.BlockSpec(shp, lambda i: (0, 0))
    return pl.pallas_call(
        _nsa_cmp_finish_kernel,
        grid=(1,),
        in_specs=[full((n, 2 * NSA_KV_W)), full((n, 2 * NSA_KV_W)), full((8, 2 * NSA_KV_W)), full((8, 2 * NSA_KV_W)),
                  full((8, NSA_DH)), full((n, NSA_DH)), full((n, NSA_DH))],
        out_specs=[full((n, NSA_KV_W)), full((n, NSA_KV_W))],
        out_shape=[jax.ShapeDtypeStruct((n, NSA_KV_W), BF16)] * 2,
        scratch_shapes=[pltpu.VMEM((n + 8, NSA_KV_W), F32)],
        compiler_params=_cp("arbitrary"),
        name="nsa_cmp_finish",
    )(ak, av, pek, pev, gain8, ccos, csin)


def _nsa_attn_kernel(q_ref, kcmp_ref, vcmpT_ref, ks_ref, vsT_ref, kw_ref, vwT_ref, gate_ref, khot_ref,
                     o_ref, qT_ref, psum_ref, sel_ref, oc_ref, m_ref, l_ref, acc_ref, s_ref, *, top_n):
    g = pl.program_id(0)
    pb = pl.program_id(1)
    rows = NSA_GROUP * Q_BLOCK
    n_cmp = kcmp_ref.shape[0]
    n_slc = sel_ref.shape[1]
    streams = range(NSA_QPAIR)
    t0s = [(pb * NSA_QPAIR + i) * Q_BLOCK for i in streams]

    lane = lax.broadcasted_iota(jnp.int32, (1, rows), 1)
    q_io = lax.broadcasted_iota(jnp.int32, (1, Q_BLOCK), 1)
    n_io = lax.broadcasted_iota(jnp.int32, (n_cmp, 1), 0)
    m_io = lax.broadcasted_iota(jnp.int32, (n_slc, 1), 0)
    for i in streams:
        t0 = t0s[i]
        t_lane = t0 + (lane & (Q_BLOCK - 1))
        qf = q_ref[i * Q_BLOCK:(i + 1) * Q_BLOCK, :].astype(F32)
        qT = jnp.concatenate([qf[:, j * NSA_DH:(j + 1) * NSA_DH].T for j in range(NSA_GROUP)], axis=1).astype(BF16)
        qT_ref[i, 0:NSA_DH, :] = qT
        qT_ref[i, NSA_DH:2 * NSA_DH, :] = jnp.zeros((NSA_DH, rows), BF16)

        s = _dot(kcmp_ref[...], qT)
        valid = (n_io * CMP_STRIDE + (CMP_LEN - 1)) <= t_lane
        sm = jnp.where(valid, s, NEG_INF)
        mx = jnp.max(sm, axis=0, keepdims=True)
        e = jnp.where(valid, jnp.exp2(sm - mx), 0.0)
        den = jnp.sum(e, axis=0, keepdims=True)
        p = e * jnp.where(den > 0.0, 1.0 / den, 0.0)
        oc_ref[i] = _dot(vcmpT_ref[...], p.astype(BF16))
        psum = p[:, 0:Q_BLOCK]
        for j in range(1, NSA_GROUP):
            psum = psum + p[:, j * Q_BLOCK:(j + 1) * Q_BLOCK]
        psum_ref[i, 0:8, :] = jnp.zeros((8, Q_BLOCK), F32)
        psum_ref[i, 8:8 + n_cmp, :] = psum
        ratio = SLC_LEN // CMP_STRIDE
        imp = 0.5 * (psum_ref[i, pl.ds(7, n_slc, stride=ratio), :] + psum_ref[i, pl.ds(8 + ratio - 1, n_slc, stride=ratio), :])
        for c in range(ratio - 1):
            imp = imp + psum_ref[i, pl.ds(8 + c, n_slc, stride=ratio), :]

        cur = jnp.right_shift(t0 + q_io, 6)
        forced = (m_io == 0) | (m_io == cur) | (m_io == cur - 1)
        score = jnp.where(forced | (m_io > cur), -2e38, imp)
        for _ in range(top_n - 3):
            best = jnp.max(score, axis=0, keepdims=True)
            idx = jnp.min(jnp.where(score == best, m_io, n_slc), axis=0, keepdims=True)
            score = jnp.where(m_io == idx, -3e38, score)
        sel_ref[i] = jnp.where(forced | (score < -2.5e38), 0.0, NEG_INF)

    m_ref[...] = jnp.full(m_ref.shape, NEG_INF, F32)
    l_ref[...] = jnp.zeros(l_ref.shape, F32)
    acc_ref[...] = jnp.zeros(acc_ref.shape, F32)
    blocks_per_tile = SLC_TILE // SLC_LEN

    sub = SLC_TILE // 4

    def scores_into(kt, slot):
        k_tile = ks_ref[pl.ds(pl.multiple_of(kt * SLC_TILE, SLC_TILE), SLC_TILE), :]
        k_aug = jnp.concatenate([k_tile, khot_ref[slot]], axis=1)
        band = NSA_DH + 16 * slot
        for i in streams:
            selt = sel_ref[i, pl.ds(pl.multiple_of(kt * blocks_per_tile, blocks_per_tile), blocks_per_tile), :]
            rows8 = jnp.concatenate([selt] * NSA_GROUP, axis=1)
            qT_ref[i, band:band + 16, :] = jnp.concatenate([rows8, jnp.zeros_like(rows8)], axis=0).astype(BF16)
        for i in streams:
            for c in range(4):
                s_ref[i, slot, c * sub:(c + 1) * sub, :] = _dot(k_aug[c * sub:(c + 1) * sub, :], qT_ref[i])

    def absorb(kt, slot, causal):
        v_tile = vsT_ref[kt]
        for i in streams:
            sm = s_ref[i, slot]
            if causal:
                kpos = kt * SLC_TILE + lax.broadcasted_iota(jnp.int32, (SLC_TILE, rows), 0)
                sm = jnp.where(kpos <= t0s[i] + (lane & (Q_BLOCK - 1)), sm, NEG_INF)
            m_old = m_ref[i]
            m_new = jnp.maximum(m_old, jnp.max(sm, axis=0, keepdims=True))
            p = jnp.exp2(sm - m_new)
            alpha = jnp.exp2(m_old - m_new)
            l_ref[i] = alpha * l_ref[i] + jnp.sum(p, axis=0, keepdims=True)
            acc_ref[i] = alpha * acc_ref[i] + _dot(v_tile, p.astype(BF16))
            m_ref[i] = m_new

    last_kt = (t0s[0] + Q_BLOCK - 1) // SLC_TILE
    scores_into(0, 0)

    def slc_pair(kk, carry):
        kt = 2 * kk
        scores_into(kt + 1, 1)
        absorb(kt, 0, False)
        scores_into(kt + 2, 0)
        absorb(kt + 1, 1, False)
        return carry

    lax.fori_loop(0, last_kt // 2, slc_pair, 0)

    @pl.when(last_kt % 2 == 0)
    def _():
        absorb(last_kt, 0, True)

    @pl.when(last_kt % 2 == 1)
    def _():
        scores_into(last_kt, 1)
        absorb(last_kt - 1, 0, False)
        absorb(last_kt, 1, True)

    gT = gate_ref[...].T
    wlen = WIN + Q_BLOCK
    for i in streams:
        t0 = t0s[i]
        t_lane = t0 + (lane & (Q_BLOCK - 1))
        o_s = acc_ref[i] * (1.0 / l_ref[i])

        start = pl.multiple_of(jnp.maximum(t0 - WIN, 0), Q_BLOCK)
        s = _dot(kw_ref[pl.ds(start, wlen), :], qT_ref[i, 0:NSA_DH, :])
        kpos = start + lax.broadcasted_iota(jnp.int32, (wlen, 1), 0)
        d = t_lane - kpos
        mask = (d >= 0) & (d < WIN)
        sm = jnp.where(mask, s, NEG_INF)
        mx = jnp.max(sm, axis=0, keepdims=True)
        e = jnp.where(mask, jnp.exp2(sm - mx), 0.0)
        p = (e * (1.0 / jnp.sum(e, axis=0, keepdims=True))).astype(BF16)
        tile0 = start // Q_BLOCK
        o_w = _dot(vwT_ref[tile0], p[0:Q_BLOCK, :])
        for w in range(1, wlen // Q_BLOCK):
            o_w = o_w + _dot(vwT_ref[tile0 + w], p[w * Q_BLOCK:(w + 1) * Q_BLOCK, :])

        for j in range(NSA_GROUP):
            sl = slice(j * Q_BLOCK, (j + 1) * Q_BLOCK)

            def grow(b):
                r0 = b * NSA_HEADS + j
                r1 = b * NSA_HEADS + NSA_GROUP + j
                cols = slice(i * Q_BLOCK, (i + 1) * Q_BLOCK)
                return jnp.where(g == 0, gT[r0:r0 + 1, cols], gT[r1:r1 + 1, cols])

            o = grow(0) * oc_ref[i, :, sl] + grow(1) * o_s[:, sl] + grow(2) * o_w[:, sl]
            o_ref[i * Q_BLOCK:(i + 1) * Q_BLOCK, j * NSA_DH:(j + 1) * NSA_DH] = o.T.astype(BF16)


def _nsa_attention(q, kcmp, vcmp, ks, vsT, kw, vwT, gate):
    S = q.shape[0]
    n_cmp = kcmp.shape[0]
    n_slc = S // SLC_LEN
    n_qb = S // Q_BLOCK
    top_n = min(SLC_TOPN, n_slc)
    rows = NSA_GROUP * Q_BLOCK

    nq = NSA_QPAIR
    assert n_qb % nq == 0 and SLC_TILE % (nq * Q_BLOCK) == 0 and n_cmp * CMP_STRIDE == n_slc * SLC_LEN

    blk_of_key = np.arange(SLC_TILE)[:, None] // SLC_LEN
    khot = jnp.asarray(np.stack([blk_of_key + 16 * s == np.arange(NSA_DH)[None, :] for s in range(2)]), dtype=BF16)
    vcmpT = vcmp.reshape(n_cmp, NSA_KV_HEADS, NSA_DH).transpose(1, 2, 0)

    return pl.pallas_call(
        functools.partial(_nsa_attn_kernel, top_n=top_n),
        grid=(NSA_KV_HEADS, n_qb // nq),
        in_specs=[
            pl.BlockSpec((nq * Q_BLOCK, rows), lambda g, b: (b, g)),
            pl.BlockSpec((n_cmp, NSA_DH), lambda g, b: (0, g)),
            pl.BlockSpec((None, NSA_DH, n_cmp), lambda g, b: (g, 0, 0)),
            pl.BlockSpec((S, NSA_DH), lambda g, b: (0, g)),
            pl.BlockSpec((None, S // SLC_TILE, NSA_DH, SLC_TILE), lambda g, b: (g, 0, 0, 0)),
            pl.BlockSpec((S, NSA_DH), lambda g, b: (0, g)),
            pl.BlockSpec((None, S // Q_BLOCK, NSA_DH, Q_BLOCK), lambda g, b: (g, 0, 0, 0)),
            pl.BlockSpec((nq * Q_BLOCK, 128), lambda g, b: (b, 0)),
            pl.BlockSpec((2, SLC_TILE, NSA_DH), lambda g, b: (0, 0, 0)),
        ],
        out_specs=pl.BlockSpec((nq * Q_BLOCK, rows), lambda g, b: (b, g)),
        out_shape=jax.ShapeDtypeStruct((S, NSA_W), BF16),
        scratch_shapes=[
            pltpu.VMEM((nq, 2 * NSA_DH, rows), BF16),
            pltpu.VMEM((nq, n_cmp + 8, Q_BLOCK), F32),
            pltpu.VMEM((nq, n_slc, Q_BLOCK), F32),
            pltpu.VMEM((nq, NSA_DH, rows), F32),
            pltpu.VMEM((nq, 1, rows), F32),
            pltpu.VMEM((nq, 1, rows), F32),
            pltpu.VMEM((nq, NSA_DH, rows), F32),
            pltpu.VMEM((nq, 2, SLC_TILE, rows), F32),
        ],
        compiler_params=_cp("arbitrary", "arbitrary"),
        name="nsa_attn",
    )(q, kcmp, vcmpT, ks, vsT, kw, vwT, gate, khot)


def _rope_tables(pos):
    inv = ROPE_THETA ** (-(jnp.arange(0, NSA_DH, 2, dtype=F32) / NSA_DH))
    ang = pos[:, None] * inv[None, :]
    c, s = jnp.cos(ang), jnp.sin(ang)
    return jnp.concatenate([c, c], axis=-1), jnp.concatenate([-s, s], axis=-1)


def _nsa_mixer(x, g, w, qk_gain, cmp_w, cmp_pe):
    S = x.shape[0]
    gain8 = jnp.concatenate([qk_gain.astype(F32), jnp.zeros((4, NSA_DH), F32)], axis=0)
    cosf, sinf = _rope_tables(jnp.arange(S, dtype=F32))
    q, kc, vc, ks, vs, kw, vw, gate = _nsa_proj_prep(x, g, w, gain8, cosf, sinf)
    n = S // CMP_STRIDE
    ccos, csin = _rope_tables((jnp.arange(n) * CMP_STRIDE + CMP_LEN - 1).astype(F32))
    kcmp, vcmp = _nsa_compress(kc, vc, cmp_w, cmp_pe, gain8, ccos, csin)
    return _nsa_attention(q, kcmp, vcmp, ks, vs, kw, vw, gate)


def _head_ones():
    return jnp.asarray(np.kron(np.eye(128 // RW_DH), np.ones((RW_DH, RW_DH))), dtype=BF16)


def _seg_sum(x, ones_bd):
    outs = []
    for c in range(x.shape[1] // 128):
        p1, p2, p3 = _split3(x[:, c * 128:(c + 1) * 128])
        outs.append(_dot(p1, ones_bd) + _dot(p2, ones_bd) + _dot(p3, ones_bd))
    return jnp.concatenate(outs, axis=1)


def _rw_proj_prep_kernel(x_ref, gx_ref, w_ref, mu_ref, w0_ref, w2_ref, a0_ref, a2_ref, g2_ref, kk_ref, ka_ref, rk_ref,
                         bd_ref, r_ref, lw_ref, k_ref, v_ref, a_ref, b_ref, g_ref, bonus_ref, buf_ref):
    tm = x_ref.shape[0]
    W = RW_W

    @pl.when(pl.program_id(0) == 0)
    def _():
        buf_ref[0:8, :] = jnp.zeros((8, RW_COLS_PAD), F32)

    x = x_ref[...]
    ms = jnp.mean(x * x, axis=-1, keepdims=True)
    hn = (x * lax.rsqrt(ms + NORM_EPS) * gx_ref[...]).astype(BF16)

    def shifted(lo, hi):
        buf_ref[8:8 + tm, lo:hi] = _dot(hn, w_ref[:, lo:hi])
        u = buf_ref[8:8 + tm, lo:hi]
        return u + (buf_ref[pl.ds(7, tm), lo:hi] - u) * mu_ref[:, lo:hi]

    xl = shifted(3 * W, RW_COLS_PAD)
    wl, al, gl = xl[:, 0:LORA_PAD], xl[:, LORA_PAD:2 * LORA_PAD], xl[:, 2 * LORA_PAD:]
    xk = shifted(W, 2 * W)
    wx = w0_ref[...] + _dot(jnp.tanh(wl).astype(BF16), w2_ref[...])
    lw_ref[...] = -jnp.exp(-_softplus(-wx) - 0.5)
    a = _sigmoid(a0_ref[...] + _dot(al.astype(BF16), a2_ref[...]))
    g_ref[...] = _dot(_sigmoid(gl).astype(BF16), g2_ref[...])
    xr = shifted(0, W)
    bd = bd_ref[...]
    kk = xk * kk_ref[...]
    kk = kk / jnp.maximum(jnp.sqrt(_seg_sum(kk * kk, bd)), 1e-12)
    kf = xk * (1.0 + (a - 1.0) * ka_ref[...])
    k_ref[...] = kf
    a_ref[...] = -kk
    b_ref[...] = kk * a
    xv = shifted(2 * W, 3 * W)
    r_ref[...] = xr
    v_ref[...] = xv
    bonus_ref[...] = _seg_sum(xr * kf * rk_ref[...], bd) * xv
    buf_ref[0:8, :] = buf_ref[tm:tm + 8, :]


def _rw_proj_prep(x, g, w, mu, w0, w2, a0, a2, g2, k_k, k_a, r_k):
    S, D = x.shape
    tm = _pick(S, (256, 128, 64))
    W = RW_W
    vec = lambda n: pl.BlockSpec((1, n), lambda i: (0, 0))
    mat = lambda a, b: pl.BlockSpec((a, b), lambda i: (0, 0))
    row = pl.BlockSpec((tm, W), lambda i: (i, 0))
    out = jax.ShapeDtypeStruct((S, W), F32)
    return pl.pallas_call(
        _rw_proj_prep_kernel,
        grid=(S // tm,),
        in_specs=[
            pl.BlockSpec((tm, D), lambda i: (i, 0)), vec(D),
            pl.BlockSpec((D, RW_COLS_PAD), lambda i: (0, 0), pipeline_mode=pl.Buffered(1)),
            vec(RW_COLS_PAD), vec(W), mat(LORA_PAD, W), vec(W), mat(LORA_PAD, W), mat(RW_G_LORA, W),
            vec(W), vec(W), vec(W), mat(128, 128),
        ],
        out_specs=[row] * 8,
        out_shape=[out] * 8,
        scratch_shapes=[pltpu.VMEM((tm + 8, RW_COLS_PAD), F32)],
        compiler_params=_cp("arbitrary"),
        name="rw_proj_prep",
    )(x, g.reshape(1, D), w, mu, w0, w2, a0, a2, g2, k_k, k_a, r_k, _head_ones())


def _bd_rows(x, lane_head):
    lh = lane_head[0:x.shape[0], :]
    return jnp.concatenate([jnp.where(lh == h, x, jnp.zeros_like(x)) for h in range(RW_PACK)], axis=0)


def _diag_pack(full, lane_head):
    out = None
    for h in range(RW_PACK):
        blk = jnp.where(lane_head == h, full[h * RW_DH:(h + 1) * RW_DH, :], 0.0)
        out = blk if out is None else out + blk
    return out


def _rw_chunk_kernel(r_ref, lw_ref, k_ref, v_ref, a_ref, b_ref, lhs_ref, add_ref):
    for cb in range(r_ref.shape[0] // RW_CHUNK):
        _rw_chunk_body(r_ref, lw_ref, k_ref, v_ref, a_ref, b_ref, lhs_ref, add_ref, cb * RW_CHUNK)


def _rw_chunk_body(r_ref, lw_ref, k_ref, v_ref, a_ref, b_ref, lhs_ref, add_ref, r0):
    C = RW_CHUNK
    rs = slice(r0, r0 + C)
    ti = lax.broadcasted_iota(jnp.int32, (C, C), 0)
    si = lax.broadcasted_iota(jnp.int32, (C, C), 1)
    tri_incl = (si <= ti).astype(F32)
    GW = RW_PACK * RW_DH
    lane_head = lax.broadcasted_iota(jnp.int32, (C, GW), 1) // RW_DH
    lane_pos = lax.broadcasted_iota(jnp.int32, (C, GW), 1) % RW_DH
    row_t = lax.broadcasted_iota(jnp.int32, (C, GW), 0)
    eye_p = (lane_pos == row_t).astype(F32)
    strict = jnp.concatenate([lane_pos < row_t] * 2, axis=1)
    incl = jnp.concatenate([lane_pos <= row_t] * 2, axis=1)

    lw = lw_ref[rs, :]
    cs = _dot_hi(tri_incl, lw)
    cprev = cs - lw
    clast = cs[C - 1:C, :]
    e_neg = jnp.exp(-cs)
    e_end = jnp.exp(clast - cs)
    At = (a_ref[rs, :] * jnp.exp(cprev)).astype(BF16)
    Rt32 = r_ref[rs, :] * jnp.exp(cs)
    Rt = Rt32.astype(BF16)
    Bt = (b_ref[rs, :] * e_neg).astype(BF16)
    Kt = (k_ref[rs, :] * e_neg).astype(BF16)
    Bh = (b_ref[rs, :] * e_end).astype(BF16)
    Kh = (k_ref[rs, :] * e_end).astype(BF16)
    V = v_ref[rs, :].astype(BF16)
    gam = jnp.exp(clast)

    groups = range(RW_HEADS // RW_PACK)
    sls = [slice(gi * GW, (gi + 1) * GW) for gi in groups]
    bd = functools.partial(_bd_rows, lane_head=lane_head)
    G = [_dot_nt(jnp.concatenate([At[:, sl], Rt[:, sl]], axis=0),
                 jnp.concatenate([bd(Bt[:, sl]), bd(Kt[:, sl])], axis=0)) for sl in sls]
    top = [jnp.where(strict, g_[0:C, :], 0.0) for g_ in G]
    bot = [jnp.where(incl, g_[C:2 * C, :], 0.0).astype(BF16) for g_ in G]
    L = [t_[:, 0:GW] for t_ in top]
    AV = [_dot(jnp.concatenate([top[gi][:, GW:].astype(BF16), bot[gi][:, GW:]], axis=0), bd(V[:, sls[gi]])) for gi in groups]
    T = [eye_p + l_ for l_ in L]
    P = [_dot(l_.astype(BF16), bd(l_.astype(BF16))) for l_ in L]
    for _ in range(4):
        TP = [_dot(jnp.concatenate([T[gi].astype(BF16), P[gi].astype(BF16)], axis=0), bd(P[gi].astype(BF16)))
              for gi in groups]
        T = [T[gi] + TP[gi][0:C, :] for gi in groups]
        P = [tp[C:2 * C, :] for tp in TP]
    T = [T[gi] + _dot(T[gi].astype(BF16), bd(P[gi].astype(BF16))) for gi in groups]
    AU = [_dot(T[gi].astype(BF16), jnp.concatenate([bd(At[:, sls[gi]]), bd(AV[gi][0:C, :].astype(BF16))], axis=1))
          for gi in groups]
    Ahat = [au[:, 0:GW].astype(BF16) for au in AU]
    Uhat = [au[:, GW:].astype(BF16) for au in AU]
    RY = [_dot(bot[gi][:, 0:GW], jnp.concatenate([bd(Ahat[gi]), bd(Uhat[gi])], axis=1)) for gi in groups]
    phi_full = [_dot_tn(Bh[:, sls[gi]], Ahat[gi]) for gi in groups]
    hh_full = [_dot_tn(jnp.concatenate([Bh[:, sls[gi]], Kh[:, sls[gi]]], axis=0),
                       jnp.concatenate([Uhat[gi], V[:, sls[gi]]], axis=0)) for gi in groups]
    for gi in groups:
        sl = sls[gi]
        o0 = 2 * r0
        lhs_ref[o0:o0 + C, sl] = Rt32[:, sl] + RY[gi][:, 0:GW]
        add_ref[o0:o0 + C, sl] = RY[gi][:, GW:] + AV[gi][C:2 * C, :]
        lhs_ref[o0 + C:o0 + 2 * C, sl] = eye_p * gam[:, sl] + _diag_pack(phi_full[gi], lane_head)
        add_ref[o0 + C:o0 + 2 * C, sl] = _diag_pack(hh_full[gi], lane_head)


def _rw_chunks(r, lw, k, v, a, b):
    S = r.shape[0]
    C = RW_CHUNK * _pick(S // RW_CHUNK, (2, 1))
    row = pl.BlockSpec((C, RW_W), lambda c: (c, 0))
    out = pl.BlockSpec((2 * C, RW_W), lambda c: (c, 0))
    shp = jax.ShapeDtypeStruct((2 * S, RW_W), F32)
    return pl.pallas_call(
        _rw_chunk_kernel,
        grid=(S // C,),
        in_specs=[row] * 6,
        out_specs=[out, out],
        out_shape=[shp, shp],
        compiler_params=_cp("parallel"),
        name="rw_chunk",
    )(r, lw, k, v, a, b)


def _rw_scan_kernel(lhs_ref, add_ref, bonus_ref, g_ref, lnw_ref, lnb_ref, bd_ref, y_ref, h_ref):
    C = RW_CHUNK

    @pl.when(pl.program_id(0) == 0)
    def _():
        h_ref[...] = jnp.zeros_like(h_ref)

    GW = RW_PACK * RW_DH
    lane_head = lax.broadcasted_iota(jnp.int32, (RW_DH, GW), 1) // RW_DH
    groups = range(RW_HEADS // RW_PACK)
    gsls = [slice(gi * GW, (gi + 1) * GW) for gi in groups]
    bd = bd_ref[...]
    ys = []
    for cb in range(y_ref.shape[0] // C):
        two = slice(2 * C * cb, 2 * C * (cb + 1))
        allres = [_dot(lhs_ref[two, gsl].astype(BF16), _bd_rows(h_ref[:, gsl].astype(BF16), lane_head))
                  + add_ref[two, gsl] for gsl in gsls]
        for gi in groups:
            h_ref[:, gsls[gi]] = allres[gi][C:2 * C, :]
        ys.append(jnp.concatenate([allres[gi][0:C, :] for gi in groups], axis=1))
    y = jnp.concatenate(ys, axis=0)
    yc = y - _seg_sum(y, bd) * (1.0 / RW_DH)
    var = _seg_sum(yc * yc, bd) * (1.0 / RW_DH)
    yn = yc * lax.rsqrt(var + RW_LNX_EPS) * lnw_ref[...] + lnb_ref[...]
    y_ref[...] = ((yn + bonus_ref[...]) * g_ref[...]).astype(BF16)


def _rw_scan(lhs, add, bonus, g, lnx_w, lnx_b):
    S = bonus.shape[0]
    C = RW_CHUNK * _pick(S // RW_CHUNK, (4, 2, 1))
    row = pl.BlockSpec((C, RW_W), lambda c: (c, 0))
    two = pl.BlockSpec((2 * C, RW_W), lambda c: (c, 0))
    vec = pl.BlockSpec((1, RW_W), lambda c: (0, 0))
    return pl.pallas_call(
        _rw_scan_kernel,
        grid=(S // C,),
        in_specs=[two, two, row, row, vec, vec, pl.BlockSpec((128, 128), lambda c: (0, 0))],
        out_specs=row,
        out_shape=jax.ShapeDtypeStruct((S, RW_W), BF16),
        scratch_shapes=[pltpu.VMEM((RW_DH, RW_W), F32)],
        compiler_params=_cp("arbitrary"),
        name="rw_scan",
    )(lhs, add, bonus, g, lnx_w, lnx_b, _head_ones())


def _pad_lora_cols(x, axis):
    W3 = 3 * RW_W
    parts = [lax.slice_in_dim(x, 0, W3, axis=axis),
             lax.slice_in_dim(x, W3, W3 + RW_DECAY_LORA, axis=axis),
             lax.slice_in_dim(x, W3 + RW_DECAY_LORA, W3 + RW_DECAY_LORA + RW_A_LORA, axis=axis),
             lax.slice_in_dim(x, W3 + RW_DECAY_LORA + RW_A_LORA, RW_COLS, axis=axis)]

    def padto(p, n):
        cfg = [(0, 0)] * x.ndim
        cfg[axis] = (0, n - p.shape[axis])
        return jnp.pad(p, cfg)

    return jnp.concatenate([parts[0], padto(parts[1], LORA_PAD), padto(parts[2], LORA_PAD), parts[3]], axis=axis)


def _rwkv_mixer(x, gx, w, mu, w0, w2, a0, a2, g2, k_k, k_a, r_k, lnx_w, lnx_b):
    row = lambda p: p.reshape(1, -1).astype(F32)
    padrows = lambda m: jnp.pad(m, ((0, LORA_PAD - m.shape[0]), (0, 0))).astype(BF16)
    r, lw, k, v, a, b, g, bonus = _rw_proj_prep(
        x, gx, w, _pad_lora_cols(row(mu), 1), row(w0), padrows(w2), row(a0), padrows(a2), g2.astype(BF16),
        row(k_k), row(k_a), row(r_k))
    lhs, add = _rw_chunks(r, lw, k, v, a, b)
    return _rw_scan(lhs, add, bonus, g, row(lnx_w), row(lnx_b))


def _lru_kernel(u_ref, cw_ref, cb_ref, wa_ref, ba_ref, wx_ref, bx_ref, sp_ref, y_ref, buf_ref, a_sc, b_sc, h_sc):
    i = pl.program_id(0)
    tm = u_ref.shape[0]

    @pl.when(i == 0)
    def _():
        buf_ref[0:8, :] = jnp.zeros((8, LRU_W), F32)
        h_sc[...] = jnp.zeros_like(h_sc)

    xb = u_ref[:, LRU_W:]
    buf_ref[8:8 + tm, :] = xb
    xc = cb_ref[...] + cw_ref[CONV_W - 1:CONV_W, :] * xb
    for kk in range(CONV_W - 1):
        xc = xc + cw_ref[kk:kk + 1, :] * buf_ref[pl.ds(8 - (CONV_W - 1) + kk, tm), :]
    buf_ref[0:8, :] = xb[tm - 8:tm, :]

    xcb = xc.astype(BF16)
    ra, ri = [], []
    for n in range(LRU_BLOCKS):
        sl = slice(n * LRU_BW, (n + 1) * LRU_BW)
        ra.append(_dot(xcb[:, sl], wa_ref[n]))
        ri.append(_dot(xcb[:, sl], wx_ref[n]))
    rg = _sigmoid(jnp.concatenate(ra, axis=1) + ba_ref[...])
    ig = _sigmoid(jnp.concatenate(ri, axis=1) + bx_ref[...])
    log_a = -LRU_C * rg * sp_ref[...]
    a_sc[...] = jnp.exp(log_a)
    b_sc[...] = jnp.sqrt(1.0 - jnp.exp(2.0 * log_a)) * ig * xc

    rowi = lax.broadcasted_iota(jnp.int32, (8, LRU_W), 0)

    def group(gidx, h):
        r0 = pl.multiple_of(gidx * 8, 8)
        a = a_sc[pl.ds(r0, 8), :]
        b = b_sc[pl.ds(r0, 8), :]
        for d in (1, 2, 4):
            keep = rowi >= d
            b = jnp.where(keep, a * pltpu.roll(b, d, 0) + b, b)
            a = jnp.where(keep, a * pltpu.roll(a, d, 0), a)
        hrows = a * h + b
        b_sc[pl.ds(r0, 8), :] = hrows
        return hrows[7:8, :]

    h_sc[...] = lax.fori_loop(0, tm // 8, group, h_sc[...], unroll=4)
    gate = u_ref[:, 0:LRU_W]
    gelu = 0.5 * gate * (1.0 + jnp.tanh(0.7978845608028654 * (gate + 0.044715 * gate * gate * gate)))
    y_ref[...] = (b_sc[...] * gelu).astype(BF16)


def _lru_mixer(u, conv_w, conv_b, wa, ba, wx, bx, lam):
    S = u.shape[0]
    tm = _pick(S, (256, 128, 64))
    row = lambda p: p.reshape(1, -1).astype(F32)
    vec = pl.BlockSpec((1, LRU_W), lambda i: (0, 0))
    blk = pl.BlockSpec((LRU_BLOCKS, LRU_BW, LRU_BW), lambda i: (0, 0, 0))
    lamf = lam.astype(F32)
    softplus_neg_lam = row(jnp.maximum(-lamf, 0.0) + jnp.log1p(jnp.exp(-jnp.abs(lamf))))
    cw = jnp.concatenate([conv_w.astype(F32), jnp.zeros((8 - CONV_W, LRU_W), F32)], axis=0)
    return pl.pallas_call(
        _lru_kernel,
        grid=(S // tm,),
        in_specs=[pl.BlockSpec((tm, 2 * LRU_W), lambda i: (i, 0)), pl.BlockSpec((8, LRU_W), lambda i: (0, 0)),
                  vec, blk, vec, blk, vec, vec],
        out_specs=pl.BlockSpec((tm, LRU_W), lambda i: (i, 0)),
        out_shape=jax.ShapeDtypeStruct((S, LRU_W), BF16),
        scratch_shapes=[pltpu.VMEM((tm + 8, LRU_W), F32), pltpu.VMEM((tm, LRU_W), F32),
                        pltpu.VMEM((tm, LRU_W), F32), pltpu.VMEM((1, LRU_W), F32)],
        compiler_params=_cp("arbitrary"),
        name="rglru",
    )(u, cw, row(conv_b), wa.astype(BF16), row(ba), wx.astype(BF16), row(bx), softplus_neg_lam)


def _hgrn_kernel(u_ref, lower_ref, ng_ref, y_ref, st_ref):
    C = HG_CHUNK
    W = HG_KW

    @pl.when(pl.program_id(0) == 0)
    def _():
        st_ref[...] = jnp.zeros_like(st_ref)

    for cb in range(u_ref.shape[0] // C):
        _hgrn_chunk(u_ref, lower_ref, ng_ref, y_ref, st_ref, cb * C)


def _hgrn_chunk(u_ref, lower_ref, ng_ref, y_ref, st_ref, r0):
    C = HG_CHUNK
    W = HG_KW
    rs = slice(r0, r0 + C)
    lower = lower_ref[...]
    forget = lower + (1.0 - lower) * _sigmoid(u_ref[rs, W:2 * W])
    lf = jnp.log(forget)
    kk = 1.0 - forget
    qx = u_ref[rs, 0:W]
    q = qx * _sigmoid(qx)
    v = u_ref[rs, 2 * W:3 * W]
    gx = u_ref[rs, 3 * W:4 * W]

    ti = lax.broadcasted_iota(jnp.int32, (C, C), 0)
    si = lax.broadcasted_iota(jnp.int32, (C, C), 1)
    b = _dot_hi((si <= ti).astype(F32), lf)
    blast = b[C - 1:C, :]
    q_in = (q * jnp.exp(b)).astype(BF16)
    k_end = (kk * jnp.exp(blast - b)).astype(BF16)
    vb = v.astype(BF16)
    rowi = lax.broadcasted_iota(jnp.int32, (C, W), 0)

    n_sub = C // HG_SUB
    refs = [jnp.zeros((1, W), F32)] + [b[i * HG_SUB - 1:i * HG_SUB, :] for i in range(1, n_sub)]
    ref_rows = jnp.concatenate([jnp.broadcast_to(r_, (HG_SUB, W)) for r_ in refs], axis=0)
    q_sub = (q * jnp.exp(b - ref_rows)).astype(BF16)
    k_sub = []
    for i in range(n_sub):
        live = rowi < (i + 1) * HG_SUB
        k_sub.append(jnp.where(live, kk * jnp.exp(jnp.where(live, refs[i] - b, 0.0)), 0.0).astype(BF16))
    k_sub = jnp.concatenate(k_sub, axis=0)
    row_blk = ti // HG_SUB

    heads = range(HG_HEADS)
    sls = [slice(h * HG_DK, (h + 1) * HG_DK) for h in heads]
    states = [st_ref[:, sl] for sl in sls]
    inter = [_dot_nt(q_in[:, sls[h]], states[h].astype(BF16)) for h in heads]
    pair = [_dot_nt(q_sub[:, sl], k_sub[:, sl]) for sl in sls]
    att = []
    for h in heads:
        a_h = jnp.zeros((C, C), F32)
        for i in range(n_sub):
            a_h = jnp.where((row_blk == i) & (si <= ti), pair[h][:, i * C:(i + 1) * C], a_h)
        att.append(a_h.astype(BF16))
    intra = [_dot(att[h], vb[:, sls[h]]) for h in heads]
    upd = [_dot_tn(vb[:, sl], k_end[:, sl]) for sl in sls]
    for h in heads:
        sl = sls[h]
        st_ref[:, sl] = jnp.exp(blast[:, sl]) * states[h] + upd[h]
        o = inter[h] + intra[h]
        ms = jnp.mean(o * o, axis=-1, keepdims=True)
        on = o * lax.rsqrt(ms + NORM_EPS) * ng_ref[:, sl]
        gh = gx[:, sl]
        y_ref[rs, sl] = (on * (gh * _sigmoid(gh))).astype(BF16)


def _hgrn_mixer(u, lower, norm_g):
    S = u.shape[0]
    C = HG_CHUNK * _pick(S // HG_CHUNK, (4, 2, 1))
    vec = pl.BlockSpec((1, HG_KW), lambda c: (0, 0))
    return pl.pallas_call(
        _hgrn_kernel,
        grid=(S // C,),
        in_specs=[pl.BlockSpec((C, 4 * HG_KW), lambda c: (c, 0)), vec, vec],
        out_specs=pl.BlockSpec((C, HG_VW), lambda c: (c, 0)),
        out_shape=jax.ShapeDtypeStruct((S, HG_VW), BF16),
        scratch_shapes=[pltpu.VMEM((HG_DV, HG_KW), F32)],
        compiler_params=_cp("arbitrary"),
        name="hgrn2",
    )(u, lower.reshape(1, -1).astype(F32), norm_g.reshape(1, -1).astype(F32))


def kernel(x, norm_mix, norm_mlp, w_ff1, w_ff2, w_in_a, w_out_a, nsa_qk_gain, nsa_cmp_w, nsa_cmp_pe, rw_mu, rw_w0, rw_w2, rw_a0, rw_a2, rw_g2, rw_k_k, rw_k_a, rw_r_k, rw_lnx_w, rw_lnx_b, w_in_b, w_out_b, lru_conv_w, lru_conv_b, lru_wa, lru_ba, lru_wx, lru_bx, lru_lambda, hg_lb, hg_norm):
    B, S, D = x.shape
    depth = norm_mix.shape[0]
    lb_p = jax.nn.softmax(hg_lb.astype(F32), axis=0)
    lb_cum = jnp.cumsum(lb_p, axis=0)
    hg_lower = lb_cum - lb_cum[0:1]
    w_ff1_b, w_ff2_b = w_ff1.astype(BF16), w_ff2.astype(BF16)

    outs = []
    for bi in range(B):
        xb = (x.reshape(S, D) if B == 1 else x[bi]).astype(F32)
        for layer in range(depth):
            gmix = norm_mix[layer].astype(F32)
            if layer % 2 == 0:
                e = layer // 2
                w_in = w_in_a[e].astype(BF16)
                w_nsa = jnp.pad(w_in[:, :NSA_COLS], ((0, 0), (0, NSA_COLS_PAD - NSA_COLS)))
                w_rw = _pad_lora_cols(w_in[:, NSA_COLS:], 1)
                y_a = _nsa_mixer(xb, gmix, w_nsa, nsa_qk_gain[e], nsa_cmp_w[e], nsa_cmp_pe[e])
                y_b = _rwkv_mixer(xb, gmix, w_rw, rw_mu[e], rw_w0[e], rw_w2[e], rw_a0[e], rw_a2[e],
                                  rw_g2[e], rw_k_k[e], rw_k_a[e], rw_r_k[e], rw_lnx_w[e], rw_lnx_b[e])
                xb = _out_proj(y_a, y_b, w_out_a[e].astype(BF16), xb)
            else:
                o = layer // 2
                w_in = w_in_b[o].astype(BF16)
                y_c = _lru_mixer(_norm_matmul(xb, gmix, w_in[:, :2 * LRU_W]), lru_conv_w[o], lru_conv_b[o],
                                 lru_wa[o], lru_ba[o], lru_wx[o], lru_bx[o], lru_lambda[o])
                y_d = _hgrn_mixer(_norm_matmul(xb, gmix, w_in[:, 2 * LRU_W:]), hg_lower[layer], hg_norm[o])
                xb = _out_proj(y_c, y_d, w_out_b[o].astype(BF16), xb)
            xb = _mlp(xb, norm_mlp[layer].astype(F32), w_ff1_b, w_ff2_b, layer)
        outs.append(xb.astype(x.dtype))
    return outs[0].reshape(1, S, D) if B == 1 else jnp.stack(outs, axis=0)
```
